```python
import math
import jax
import jax.numpy as jnp
from jax import lax
import numpy as np

D_MODEL = 1024
BATCH = 8
SEQ = 2048
DEPTH = 4

N_MIXERS = 4
RMS_EPS = 1e-6
Q_BLOCK = 128

SSD_D_INNER = 2 * D_MODEL
SSD_HEAD_DIM = 64
SSD_HEADS = SSD_D_INNER // SSD_HEAD_DIM
SSD_GROUPS = 8
SSD_HEADS_PER_GROUP = SSD_HEADS // SSD_GROUPS
SSD_STATE = 128
SSD_CONV = 4
SSD_CHUNK = 128
SSD_CONV_DIM = SSD_D_INNER + 2 * SSD_GROUPS * SSD_STATE
SSD_IN_DIM = SSD_D_INNER + SSD_CONV_DIM + SSD_HEADS

SB_HEAD_DIM = 64
SB_HEADS = D_MODEL // SB_HEAD_DIM

LRU_WIDTH = 1408
LRU_BLOCKS = 16
LRU_BLOCK_DIM = LRU_WIDTH // LRU_BLOCKS
LRU_CONV = 4
LRU_C = 8.0

DIFF_HEAD_DIM = 64
DIFF_HEADS = D_MODEL // (2 * DIFF_HEAD_DIM)
ROPE_THETA = 10000.0

MOE_GROUPS = 4
MOE_EXPERTS_PER_GROUP = 8
MOE_EXPERTS = MOE_GROUPS * MOE_EXPERTS_PER_GROUP
MOE_TOP_K = 2
MOE_FF = 512
MOE_ROW_BLOCK = 128

kernel_name = 'hybrid_ssd_stickbreak_rglru_diffattn_hmoe'


def n_uses(m):
    return len(range(m, DEPTH, N_MIXERS))


def rms_norm(x, w):
    xf = x.astype(jnp.float32)
    y = xf * lax.rsqrt(jnp.mean(xf * xf, axis=-1, keepdims=True) + RMS_EPS)
    return (y * w.astype(jnp.float32)).astype(x.dtype)


def causal_dwconv(x, w, b):
    width, ch = w.shape
    y = lax.conv_general_dilated(x, w[:, None, :], window_strides=(1,), padding=[(width - 1, 0)],
                                 dimension_numbers=('NWC', 'WIO', 'NWC'), feature_group_count=ch)
    return y + b


def segsum(a):
    t = a.shape[-1]
    c = jnp.cumsum(a, axis=-1)
    s = c[..., :, None] - c[..., None, :]
    return jnp.where(jnp.tril(jnp.ones((t, t), dtype=bool)), s, -jnp.inf)


def rope_tables(positions):
    inv = ROPE_THETA ** (-jnp.arange(0, DIFF_HEAD_DIM, 2, dtype=jnp.float32) / DIFF_HEAD_DIM)
    ang = positions.astype(jnp.float32)[..., None] * inv
    return jnp.cos(ang), jnp.sin(ang)


def apply_rope(x, cos, sin):
    extra = x.ndim - 3
    c = cos.reshape(cos.shape[:2] + (1,) * extra + cos.shape[-1:])
    s = sin.reshape(sin.shape[:2] + (1,) * extra + sin.shape[-1:])
    x1, x2 = jnp.split(x, 2, axis=-1)
    return jnp.concatenate([x1 * c - x2 * s, x2 * c + x1 * s], axis=-1).astype(x.dtype)


def ssd_mixer(u, w_in, conv_w, conv_b, dt_bias, a_log, d_skip, norm_w, w_out):
    f32 = jnp.float32
    bsz, seqlen, _ = u.shape
    G, R, P, N, Q = SSD_GROUPS, SSD_HEADS_PER_GROUP, SSD_HEAD_DIM, SSD_STATE, SSD_CHUNK
    nc = seqlen // Q
    zxbcdt = u @ w_in
    z, xbc, dt = jnp.split(zxbcdt, [SSD_D_INNER, SSD_D_INNER + SSD_CONV_DIM], axis=-1)
    xbc = jax.nn.silu(causal_dwconv(xbc, conv_w, conv_b))
    xs, b_in, c_in = jnp.split(xbc, [SSD_D_INNER, SSD_D_INNER + G * N], axis=-1)
    dt = jax.nn.softplus(dt.astype(f32) + dt_bias.astype(f32))
    a = -jnp.exp(a_log.astype(f32)) * dt
    x_dt = xs.astype(f32).reshape(bsz, nc, Q, G, R, P) * dt.reshape(bsz, nc, Q, G, R, 1)
    bc = b_in.astype(f32).reshape(bsz, nc, Q, G, N)
    cc = c_in.astype(f32).reshape(bsz, nc, Q, G, N)
    a = a.reshape(bsz, nc, Q, G, R).transpose(0, 3, 4, 1, 2)
    a_cum = jnp.cumsum(a, axis=-1)
    decay = jnp.exp(segsum(a))
    cb = jnp.einsum('bclgn,bcsgn->bcgls', cc, bc)
    y_diag = jnp.einsum('bcgls,bgrcls,bcsgrp->bclgrp', cb, decay, x_dt)
    decay_states = jnp.exp(a_cum[..., -1:] - a_cum)
    states = jnp.einsum('bclgn,bgrcl,bclgrp->bcgrpn', bc, decay_states, x_dt)
    chunk_a = jnp.pad(a_cum[..., -1], ((0, 0), (0, 0), (0, 0), (1, 0)))
    decay_chunk = jnp.exp(segsum(chunk_a))
    states = jnp.concatenate([jnp.zeros_like(states[:, :1]), states], axis=1)
    states = jnp.einsum('bgrzc,bcgrpn->bzgrpn', decay_chunk, states)[:, :-1]
    y_off = jnp.einsum('bclgn,bcgrpn,bgrcl->bclgrp', cc, states, jnp.exp(a_cum))
    y = (y_diag + y_off).reshape(bsz, seqlen, SSD_HEADS, P)
    y = y + xs.astype(f32).reshape(bsz, seqlen, SSD_HEADS, P) * d_skip.astype(f32)[:, None]
    y = y.reshape(bsz, seqlen, SSD_D_INNER).astype(u.dtype)
    yg = (y * jax.nn.silu(z)).reshape(bsz, seqlen, G, SSD_D_INNER // G)
    yg = rms_norm(yg, norm_w.reshape(G, SSD_D_INNER // G)).reshape(bsz, seqlen, SSD_D_INNER)
    return yg @ w_out


def stick_breaking_mixer(u, w_in, w_out):
    bsz, seqlen, _ = u.shape
    qkv = (u @ w_in).reshape(bsz, seqlen, 3, SB_HEADS, SB_HEAD_DIM)
    q, k, v = qkv[:, :, 0], qkv[:, :, 1], qkv[:, :, 2]
    scale = SB_HEAD_DIM ** -0.5
    outs = []
    for start in range(0, seqlen, Q_BLOCK):
        end = start + Q_BLOCK
        z = jnp.einsum('bqhd,bkhd->bhqk', q[:, start:end], k[:, :end]).astype(jnp.float32) * scale
        t_idx = jnp.arange(start, end)[:, None]
        s_idx = jnp.arange(end)[None, :]
        past = s_idx < t_idx
        log_beta = jax.nn.log_sigmoid(z)
        log_keep = jnp.where(past, jax.nn.log_sigmoid(-z), 0.0)
        suffix = lax.cumsum(log_keep, axis=3, reverse=True) - log_keep
        wts = jnp.where(past, jnp.exp(log_beta + suffix), 0.0)
        outs.append(jnp.einsum('bhqk,bkhd->bqhd', wts.astype(v.dtype), v[:, :end]))
    o = jnp.concatenate(outs, axis=1).reshape(bsz, seqlen, SB_HEADS * SB_HEAD_DIM)
    return o @ w_out


def _linear_combine(left, right):
    a1, b1 = left
    a2, b2 = right
    return a1 * a2, a2 * b1 + b2


def rglru_mixer(u, w_in, conv_w, conv_b, w_rg, b_rg, w_ig, b_ig, lam, w_out):
    f32 = jnp.float32
    bsz, seqlen, _ = u.shape
    gate_branch, x = jnp.split(u @ w_in, 2, axis=-1)
    x = causal_dwconv(x, conv_w, conv_b)
    xb = x.reshape(bsz, seqlen, LRU_BLOCKS, LRU_BLOCK_DIM)
    r = jax.nn.sigmoid(jnp.einsum('blnd,nde->blne', xb, w_rg) + b_rg).reshape(bsz, seqlen, LRU_WIDTH)
    i = jax.nn.sigmoid(jnp.einsum('blnd,nde->blne', xb, w_ig) + b_ig).reshape(bsz, seqlen, LRU_WIDTH)
    log_a = (-LRU_C * jax.nn.softplus(-lam.astype(f32))) * r.astype(f32)
    a = jnp.exp(log_a)
    bterm = jnp.sqrt(-jnp.expm1(2.0 * log_a)) * (i * x).astype(f32)
    _, h = lax.associative_scan(_linear_combine, (a, bterm), axis=1)
    y = h.astype(u.dtype) * jax.nn.gelu(gate_branch)
    return y @ w_out


def diff_attention_mixer(u, cos, sin, w_in, q_norm, k_norm, lam_q1, lam_k1, lam_q2, lam_k2,
                         sub_norm, w_out, lambda_init):
    bsz, seqlen, _ = u.shape
    H, Dh = DIFF_HEADS, DIFF_HEAD_DIM
    q, k, v = jnp.split(u @ w_in, 3, axis=-1)
    q = apply_rope(rms_norm(q.reshape(bsz, seqlen, H, 2, Dh), q_norm), cos, sin)
    k = apply_rope(rms_norm(k.reshape(bsz, seqlen, H, 2, Dh), k_norm), cos, sin)
    v = v.reshape(bsz, seqlen, H, 2 * Dh)
    f32 = jnp.float32
    lam = (jnp.exp(jnp.sum(lam_q1.astype(f32) * lam_k1.astype(f32)))
           - jnp.exp(jnp.sum(lam_q2.astype(f32) * lam_k2.astype(f32))) + lambda_init)
    scale = Dh ** -0.5
    outs = []
    for start in range(0, seqlen, Q_BLOCK):
        end = start + Q_BLOCK
        sc = jnp.einsum('bqhmd,bkhmd->bhmqk', q[:, start:end], k[:, :end]).astype(f32) * scale
        causal = jnp.arange(end)[None, :] <= jnp.arange(start, end)[:, None]
        p = jax.nn.softmax(jnp.where(causal, sc, -jnp.inf), axis=-1)
        wts = p[:, :, 0] - lam * p[:, :, 1]
        outs.append(jnp.einsum('bhqk,bkhd->bqhd', wts.astype(v.dtype), v[:, :end]))
    o = jnp.concatenate(outs, axis=1)
    o = rms_norm(o, sub_norm) * (1.0 - lambda_init)
    return o.reshape(bsz, seqlen, H * 2 * Dh) @ w_out


def hier_moe(u, w_group, b_group, w_expert, b_expert, w_gate, w_up, w_down):
    f32 = jnp.float32
    bsz, seqlen, d = u.shape
    n_tok = bsz * seqlen
    xf = u.reshape(n_tok, d)
    g_logits = (xf @ w_group).astype(f32) + b_group.astype(f32)
    g_prob = jax.nn.softmax(g_logits, axis=-1)
    g_sel = jnp.argmax(g_logits, axis=-1)
    g_gate = jnp.take_along_axis(g_prob, g_sel[:, None], axis=1)[:, 0]
    e_logits = ((xf @ w_expert).astype(f32) + b_expert.astype(f32)).reshape(n_tok, MOE_GROUPS, MOE_EXPERTS_PER_GROUP)
    e_logits = jnp.take_along_axis(e_logits, g_sel[:, None, None], axis=1)[:, 0]
    top_p, top_i = lax.top_k(jax.nn.softmax(e_logits, axis=-1), MOE_TOP_K)
    top_w = g_gate[:, None] * top_p / jnp.sum(top_p, axis=-1, keepdims=True)
    expert_id = (g_sel[:, None] * MOE_EXPERTS_PER_GROUP + top_i).reshape(-1).astype(jnp.int32)
    weight = top_w.reshape(-1)
    token = jnp.repeat(jnp.arange(n_tok, dtype=jnp.int32), MOE_TOP_K)
    n_assign = n_tok * MOE_TOP_K
    n_rows = -(-(n_assign + MOE_EXPERTS * (MOE_ROW_BLOCK - 1)) // MOE_ROW_BLOCK) * MOE_ROW_BLOCK
    n_blocks = n_rows // MOE_ROW_BLOCK
    order = jnp.argsort(expert_id)
    sorted_e = expert_id[order]
    counts = jnp.bincount(expert_id, length=MOE_EXPERTS)
    padded = (counts + MOE_ROW_BLOCK - 1) // MOE_ROW_BLOCK * MOE_ROW_BLOCK
    pad_end = jnp.cumsum(padded)
    pad_start = pad_end - padded
    grp_start = jnp.cumsum(counts) - counts
    dest = pad_start[sorted_e] + (jnp.arange(n_assign) - grp_start[sorted_e])
    row_token = jnp.zeros((n_rows,), jnp.int32).at[dest].set(token[order])
    row_weight = jnp.zeros((n_rows,), weight.dtype).at[dest].set(weight[order])
    block_expert = jnp.minimum(
        jnp.searchsorted(pad_end, jnp.arange(n_blocks) * MOE_ROW_BLOCK, side='right'), MOE_EXPERTS - 1)
    x_rows = xf[row_token].reshape(n_blocks, MOE_ROW_BLOCK, d)

    def expert_rows(args):
        xb, e = args
        hid = jax.nn.silu(xb @ w_gate[e]) * (xb @ w_up[e])
        return hid @ w_down[e]

    y_rows = lax.map(expert_rows, (x_rows, block_expert)).reshape(n_rows, d)
    y = jnp.zeros_like(xf).at[row_token].add(y_rows * row_weight[:, None].astype(y_rows.dtype))
    return y.reshape(bsz, seqlen, d)


def setup_inputs(seed: int = 0) -> dict:
    key = jax.random.key(seed)
    ks = iter(jax.random.split(key, 64))
    f32 = jnp.float32

    def nrm(shape, scale):
        return jax.random.normal(next(ks), shape, f32) * scale

    def gain(shape):
        return 1.0 + 0.02 * jax.random.normal(next(ks), shape, f32)

    n0, n1, n2, n3 = n_uses(0), n_uses(1), n_uses(2), n_uses(3)
    D = D_MODEL
    x = jax.random.normal(next(ks), (BATCH, SEQ, D), f32)
    positions = jnp.broadcast_to(jnp.arange(SEQ, dtype=jnp.int32), (BATCH, SEQ))
    dt0 = jnp.exp(jax.random.uniform(next(ks), (n0, SSD_HEADS), f32, math.log(1e-3), math.log(1e-1)))
    a0 = jax.random.uniform(next(ks), (n2, LRU_WIDTH), f32, 0.9, 0.999) ** (1.0 / LRU_C)
    return {
        'x': x,
        'positions': positions,
        'norm_mix': gain((DEPTH, D)),
        'norm_ffn': gain((DEPTH, D)),
        'ssd_w_in': nrm((n0, D, SSD_IN_DIM), D ** -0.5),
        'ssd_conv_w': nrm((n0, SSD_CONV, SSD_CONV_DIM), SSD_CONV ** -0.5),
        'ssd_conv_b': nrm((n0, SSD_CONV_DIM), 0.02),
        'ssd_dt_bias': dt0 + jnp.log(-jnp.expm1(-dt0)),
        'ssd_a_log': jnp.log(jax.random.uniform(next(ks), (n0, SSD_HEADS), f32, 1.0, 16.0)),
        'ssd_d': gain((n0, SSD_HEADS)),
        'ssd_norm': gain((n0, SSD_D_INNER)),
        'ssd_w_out': nrm((n0, SSD_D_INNER, D), SSD_D_INNER ** -0.5),
        'sb_w_in': nrm((n1, D, 3 * SB_HEADS * SB_HEAD_DIM), D ** -0.5),
        'sb_w_out': nrm((n1, SB_HEADS * SB_HEAD_DIM, D), (SB_HEADS * SB_HEAD_DIM) ** -0.5),
        'lru_w_in': nrm((n2, D, 2 * LRU_WIDTH), D ** -0.5),
        'lru_conv_w': nrm((n2, LRU_CONV, LRU_WIDTH), LRU_CONV ** -0.5),
        'lru_conv_b': nrm((n2, LRU_WIDTH), 0.02),
        'lru_w_rg': nrm((n2, LRU_BLOCKS, LRU_BLOCK_DIM, LRU_BLOCK_DIM), LRU_BLOCK_DIM ** -0.5),
        'lru_b_rg': nrm((n2, LRU_BLOCKS, LRU_BLOCK_DIM), 0.02),
        'lru_w_ig': nrm((n2, LRU_BLOCKS, LRU_BLOCK_DIM, LRU_BLOCK_DIM), LRU_BLOCK_DIM ** -0.5),
        'lru_b_ig': nrm((n2, LRU_BLOCKS, LRU_BLOCK_DIM), 0.02),
        'lru_lambda': jnp.log(a0) - jnp.log1p(-a0),
        'lru_w_out': nrm((n2, LRU_WIDTH, D), LRU_WIDTH ** -0.5),
        'diff_w_in': nrm((n3, D, 3 * D), D ** -0.5),
        'diff_q_norm': gain((n3, DIFF_HEAD_DIM)),
        'diff_k_norm': gain((n3, DIFF_HEAD_DIM)),
        'diff_lam_q1': nrm((n3, DIFF_HEAD_DIM), 0.1),
        'diff_lam_k1': nrm((n3, DIFF_HEAD_DIM), 0.1),
        'diff_lam_q2': nrm((n3, DIFF_HEAD_DIM), 0.1),
        'diff_lam_k2': nrm((n3, DIFF_HEAD_DIM), 0.1),
        'diff_sub_norm': gain((n3, 2 * DIFF_HEAD_DIM)),
        'diff_w_out': nrm((n3, D, D), D ** -0.5),
        'moe_w_group': nrm((DEPTH, D, MOE_GROUPS), D ** -0.5),
        'moe_b_group': nrm((DEPTH, MOE_GROUPS), 0.01),
        'moe_w_expert': nrm((DEPTH, D, MOE_EXPERTS), D ** -0.5),
        'moe_b_expert': nrm((DEPTH, MOE_EXPERTS), 0.01),
        'moe_w_gate': nrm((DEPTH, MOE_EXPERTS, D, MOE_FF), D ** -0.5),
        'moe_w_up': nrm((DEPTH, MOE_EXPERTS, D, MOE_FF), D ** -0.5),
        'moe_w_down': nrm((DEPTH, MOE_EXPERTS, MOE_FF, D), MOE_FF ** -0.5),
    }


def reference(x, positions, norm_mix, norm_ffn,
              ssd_w_in, ssd_conv_w, ssd_conv_b, ssd_dt_bias, ssd_a_log, ssd_d, ssd_norm, ssd_w_out,
              sb_w_in, sb_w_out,
              lru_w_in, lru_conv_w, lru_conv_b, lru_w_rg, lru_b_rg, lru_w_ig, lru_b_ig, lru_lambda, lru_w_out,
              diff_w_in, diff_q_norm, diff_k_norm, diff_lam_q1, diff_lam_k1, diff_lam_q2, diff_lam_k2,
              diff_sub_norm, diff_w_out,
              moe_w_group, moe_b_group, moe_w_expert, moe_b_expert, moe_w_gate, moe_w_up, moe_w_down):
    cos, sin = rope_tables(positions)
    h = x
    for i in range(DEPTH):
        m, j = i % N_MIXERS, i // N_MIXERS
        u = rms_norm(h, norm_mix[i])
        if m == 0:
            mix = ssd_mixer(u, ssd_w_in[j], ssd_conv_w[j], ssd_conv_b[j], ssd_dt_bias[j], ssd_a_log[j],
                            ssd_d[j], ssd_norm[j], ssd_w_out[j])
        elif m == 1:
            mix = stick_breaking_mixer(u, sb_w_in[j], sb_w_out[j])
        elif m == 2:
            mix = rglru_mixer(u, lru_w_in[j], lru_conv_w[j], lru_conv_b[j], lru_w_rg[j], lru_b_rg[j],
                              lru_w_ig[j], lru_b_ig[j], lru_lambda[j], lru_w_out[j])
        else:
            lambda_init = 0.8 - 0.6 * math.exp(-0.3 * i)
            mix = diff_attention_mixer(u, cos, sin, diff_w_in[j], diff_q_norm[j], diff_k_norm[j],
                                       diff_lam_q1[j], diff_lam_k1[j], diff_lam_q2[j], diff_lam_k2[j],
                                       diff_sub_norm[j], diff_w_out[j], lambda_init)
        h = h + mix
        u = rms_norm(h, norm_ffn[i])
        h = h + hier_moe(u, moe_w_group[i], moe_b_group[i], moe_w_expert[i], moe_b_expert[i],
                         moe_w_gate[i], moe_w_up[i], moe_w_down[i])
    return h
```

```python
import functools
import math

import jax
import jax.numpy as jnp
from jax import lax
from jax.experimental import pallas as pl
from jax.experimental.pallas import tpu as pltpu

F32 = jnp.float32
BF16 = jnp.bfloat16
HIGHEST = lax.Precision.HIGHEST

RMS_EPS = 1e-6
N_MIXERS = 4

V7X_VMEM_BYTES = 64 * 1024 * 1024
V7X_LANES = 128
V7X_SUBLANES = 8
VMEM_LIMIT = (V7X_VMEM_BYTES * 7) // 8

SSD_HEAD_DIM = 64
SSD_GROUPS = 8
SSD_STATE = 128
SSD_CONV = 4
SSD_CHUNK = 128
SB_HEAD_DIM = 64
LRU_BLOCKS = 16
LRU_CONV = 4
LRU_C = 8.0
DIFF_HEAD_DIM = 64
ROPE_THETA = 10000.0
MOE_GROUPS = 4
MOE_EXPERTS_PER_GROUP = 8
MOE_EXPERTS = MOE_GROUPS * MOE_EXPERTS_PER_GROUP

ROW_TILE = 512
SSD_IN_TILE = 256
ATTN_TILE = 256
LRU_TIME_TILE = 64
MOE_TOKEN_TILE = 512
MOE_ROW_BLOCK = 256
CONV_HALO = V7X_SUBLANES


def _cparams(*sem):
    return pltpu.CompilerParams(dimension_semantics=sem, vmem_limit_bytes=VMEM_LIMIT)


def _const_spec(shape):
    nd = len(shape)
    return pl.BlockSpec(shape, lambda *_: (0,) * nd)


def _rms(x, w):
    ms = jnp.mean(x * x, axis=-1, keepdims=True)
    return x * lax.rsqrt(ms + RMS_EPS) * w


def _sigmoid(x):
    return 1.0 / (1.0 + jnp.exp(-x))


def _softplus(x):
    return jnp.maximum(x, 0.0) + jnp.log(1.0 + jnp.exp(-jnp.abs(x)))


def _dot(a, b, **kw):
    return jnp.dot(a, b, preferred_element_type=F32, **kw)


def _dot_nt(a, b):
    return lax.dot_general(a, b, (((1,), (1,)), ((), ())), preferred_element_type=F32)


def _dot_tn(a, b):
    return lax.dot_general(a, b, (((0,), (0,)), ((), ())), preferred_element_type=F32)


def _proj_residual_kernel(y_ref, w_ref, h_ref, o_ref):
    o_ref[...] = h_ref[...] + _dot(y_ref[...], w_ref[...])


def _proj_residual(y, w, h):
    t, k = y.shape
    d = w.shape[1]
    tm = ROW_TILE
    return pl.pallas_call(
        _proj_residual_kernel,
        grid=(t // tm,),
        in_specs=[pl.BlockSpec((tm, k), lambda i: (i, 0)),
                  _const_spec((k, d)),
                  pl.BlockSpec((tm, d), lambda i: (i, 0))],
        out_specs=pl.BlockSpec((tm, d), lambda i: (i, 0)),
        out_shape=jax.ShapeDtypeStruct((t, d), F32),
        compiler_params=_cparams("arbitrary"),
        name="proj_residual",
    )(y, w, h)


def _ssd_in_kernel(h_ref, g_ref, wz_ref, wx_ref, wdt_ref, cw_ref, cb_ref,
                   z_ref, xbc_ref, dt_ref, pad_ref, *, tiles_per_seq):
    i = pl.program_id(0)
    tm = h_ref.shape[0]
    u32 = _rms(h_ref[...], g_ref[...])
    u = u32.astype(BF16)
    z_ref[...] = _dot(u, wz_ref[...]).astype(z_ref.dtype)
    dt_ref[...] = _dot(u32, wdt_ref[...], precision=HIGHEST)

    @pl.when(i % tiles_per_seq == 0)
    def _():
        pad_ref[0:CONV_HALO, :] = jnp.zeros((CONV_HALO, pad_ref.shape[1]), F32)

    pad_ref[CONV_HALO:CONV_HALO + tm, :] = _dot(u, wx_ref[...])
    acc = cb_ref[...] + cw_ref[SSD_CONV - 1:SSD_CONV, :] * pad_ref[CONV_HALO:CONV_HALO + tm, :]
    for k in range(SSD_CONV - 1):
        off = CONV_HALO - (SSD_CONV - 1) + k
        acc = acc + cw_ref[k:k + 1, :] * pad_ref[off:off + tm, :]
    xbc_ref[...] = (acc * _sigmoid(acc)).astype(xbc_ref.dtype)
    pad_ref[0:CONV_HALO, :] = pad_ref[tm:tm + CONV_HALO, :]


def _ssd_scan_kernel(xs_ref, b_ref, c_ref, z_ref, dt_ref, dtt_ref,
                     bias_ref, biast_ref, alog_ref, alogt_ref, dfull_ref, nw_ref,
                     o_ref, state_ref):
    q = xs_ref.shape[0]
    n_heads = dt_ref.shape[1]
    d_inner = xs_ref.shape[1]
    heads_per_group = n_heads // SSD_GROUPS
    gw = d_inner // SSD_GROUPS
    p = SSD_HEAD_DIM

    @pl.when(pl.program_id(1) == 0)
    def _():
        state_ref[...] = jnp.zeros(state_ref.shape, F32)

    dt = _softplus(dt_ref[...] + bias_ref[...])
    a = -jnp.exp(alog_ref[...]) * dt
    dtt = _softplus(dtt_ref[...] + biast_ref[...])
    at = -jnp.exp(alogt_ref[...]) * dtt

    row = lax.broadcasted_iota(jnp.int32, (q, q), 0)
    col = lax.broadcasted_iota(jnp.int32, (q, q), 1)
    lower = row >= col
    a_cum = _dot(lower.astype(F32), a, precision=HIGHEST)
    a_cumt = _dot(at, (row <= col).astype(F32), precision=HIGHEST)
    a_tot = a_cum[q - 1:q, :]

    hh = lax.broadcasted_iota(jnp.int32, (n_heads, d_inner), 0)
    cc = lax.broadcasted_iota(jnp.int32, (n_heads, d_inner), 1)
    expand = (cc // p == hh).astype(F32)
    dt_full = _dot(dt, expand, precision=HIGHEST)
    dec_out_full = _dot(jnp.exp(a_cum), expand, precision=HIGHEST)
    dec_st_full = _dot(jnp.exp(a_tot - a_cum), expand, precision=HIGHEST)
    chunk_dec_full = _dot(jnp.broadcast_to(jnp.exp(a_tot), (V7X_SUBLANES, n_heads)), expand,
                          precision=HIGHEST)[0:1, :]

    lane = lax.broadcasted_iota(jnp.int32, (1, gw), 1)
    for g in range(SSD_GROUPS):
        sl = slice(g * gw, (g + 1) * gw)
        bg = b_ref[:, g * SSD_STATE:(g + 1) * SSD_STATE]
        cg = c_ref[:, g * SSD_STATE:(g + 1) * SSD_STATE]
        xs_g = xs_ref[:, sl].astype(F32)
        xdt = xs_g * dt_full[:, sl]
        cb = _dot_nt(cg, bg)
        s_prev = state_ref[g]
        y = _dot(cg, s_prev.astype(BF16)) * dec_out_full[:, sl]
        for r in range(heads_per_group):
            hd = g * heads_per_group + r
            seg = a_cum[:, hd:hd + 1] - a_cumt[hd:hd + 1, :]
            m = jnp.where(lower, cb * jnp.exp(seg), 0.0).astype(BF16)
            xr = jnp.where((lane >= r * p) & (lane < (r + 1) * p), xdt, 0.0).astype(BF16)
            y = y + _dot(m, xr)
        xw = (xdt * dec_st_full[:, sl]).astype(BF16)
        state_ref[g] = s_prev * chunk_dec_full[:, sl] + _dot_tn(bg, xw)
        y = y + xs_g * dfull_ref[:, sl]
        zg = z_ref[:, sl].astype(F32)
        yz = y * (zg * _sigmoid(zg))
        o_ref[:, sl] = _rms(yz, nw_ref[:, sl]).astype(o_ref.dtype)


def _ssd_mixer(h, bsz, seqlen, g_norm, w_in, conv_w, conv_b, dt_bias, a_log, d_skip, norm_w, w_out):
    t, d = h.shape
    n_heads = dt_bias.shape[0]
    d_inner = n_heads * SSD_HEAD_DIM
    gn = SSD_GROUPS * SSD_STATE
    conv_dim = d_inner + 2 * gn
    wz = w_in[:, :d_inner].astype(BF16)
    wx = w_in[:, d_inner:d_inner + conv_dim].astype(BF16)
    wdt = w_in[:, d_inner + conv_dim:]
    tm = SSD_IN_TILE
    z, xbc, dt_raw = pl.pallas_call(
        functools.partial(_ssd_in_kernel, tiles_per_seq=seqlen // tm),
        grid=(t // tm,),
        in_specs=[pl.BlockSpec((tm, d), lambda i: (i, 0)),
                  _const_spec((1, d)),
                  _const_spec((d, d_inner)),
                  _const_spec((d, conv_dim)),
                  _const_spec((d, n_heads)),
                  _const_spec((SSD_CONV, conv_dim)),
                  _const_spec((1, conv_dim))],
        out_specs=[pl.BlockSpec((tm, d_inner), lambda i: (i, 0)),
                   pl.BlockSpec((tm, conv_dim), lambda i: (i, 0)),
                   pl.BlockSpec((tm, n_heads), lambda i: (i, 0))],
        out_shape=[jax.ShapeDtypeStruct((t, d_inner), BF16),
                   jax.ShapeDtypeStruct((t, conv_dim), BF16),
                   jax.ShapeDtypeStruct((t, n_heads), F32)],
        scratch_shapes=[pltpu.VMEM((tm + CONV_HALO, conv_dim), F32)],
        compiler_params=_cparams("arbitrary"),
        name="ssd_in",
    )(h, g_norm.reshape(1, d), wz, wx, wdt, conv_w, conv_b.reshape(1, conv_dim))

    q = SSD_CHUNK
    nc = seqlen // q
    xbc3 = xbc.reshape(bsz, seqlen, conv_dim)
    dt3 = dt_raw.reshape(bsz, seqlen, n_heads)
    dtt3 = jnp.swapaxes(dt3, 1, 2)
    nb_x = d_inner // gn
    yg = pl.pallas_call(
        _ssd_scan_kernel,
        grid=(bsz, nc),
        in_specs=[pl.BlockSpec((None, q, d_inner), lambda b, c: (b, c, 0)),
                  pl.BlockSpec((None, q, gn), lambda b, c: (b, c, nb_x)),
                  pl.BlockSpec((None, q, gn), lambda b, c: (b, c, nb_x + 1)),
                  pl.BlockSpec((None, q, d_inner), lambda b, c: (b, c, 0)),
                  pl.BlockSpec((None, q, n_heads), lambda b, c: (b, c, 0)),
                  pl.BlockSpec((None, n_heads, q), lambda b, c: (b, 0, c)),
                  _const_spec((1, n_heads)), _const_spec((n_heads, 1)),
                  _const_spec((1, n_heads)), _const_spec((n_heads, 1)),
                  _const_spec((1, d_inner)), _const_spec((1, d_inner))],
        out_specs=pl.BlockSpec((None, q, d_inner), lambda b, c: (b, c, 0)),
        out_shape=jax.ShapeDtypeStruct((bsz, seqlen, d_inner), BF16),
        scratch_shapes=[pltpu.VMEM((SSD_GROUPS, SSD_STATE, d_inner // SSD_GROUPS), F32)],
        compiler_params=_cparams("arbitrary", "arbitrary"),
        name="ssd_scan",
    )(xbc3, xbc3, xbc3, z.reshape(bsz, seqlen, d_inner), dt3, dtt3,
      dt_bias.reshape(1, n_heads), dt_bias.reshape(n_heads, 1),
      a_log.reshape(1, n_heads), a_log.reshape(n_heads, 1),
      jnp.repeat(d_skip, SSD_HEAD_DIM).reshape(1, d_inner), norm_w.reshape(1, d_inner))
    return _proj_residual(yg.reshape(t, d_inner), w_out.astype(BF16), h)


def _sb_in_kernel(h_ref, g_ref, w_ref, o_ref, *, q_cols, q_scale):
    u = _rms(h_ref[...], g_ref[...]).astype(BF16)
    o_ref[:, :q_cols] = (_dot(u, w_ref[:, :q_cols]) * q_scale).astype(o_ref.dtype)
    o_ref[:, q_cols:] = _dot(u, w_ref[:, q_cols:]).astype(o_ref.dtype)


def _sb_attn_kernel(q_ref, k_ref, v_ref, o_ref, *, tile):
    seqlen = q_ref.shape[0]
    nq = seqlen // tile
    lane = lax.broadcasted_iota(jnp.int32, (1, 2 * SB_HEAD_DIM), 1)
    first = lane < SB_HEAD_DIM
    row = lax.broadcasted_iota(jnp.int32, (tile, tile), 0)
    col = lax.broadcasted_iota(jnp.int32, (tile, tile), 1)
    later = (row > col).astype(BF16)
    past = col < row

    def k_tile(q_pair, kj, carry, diagonal):
        k = k_ref[pl.ds(pl.multiple_of(kj * tile, tile), tile), :]
        v = v_ref[pl.ds(pl.multiple_of(kj * tile, tile), tile), :]
        out = []
        for hh in range(2):
            acc, csum = carry[hh]
            z = _dot_nt(q_pair[hh], k)
            sp = _softplus(z)
            log_keep = jnp.where(past, -sp, 0.0) if diagonal else -sp
            suffix = _dot(log_keep.astype(BF16), later) + csum
            w = jnp.exp((z - sp) + suffix)
            if diagonal:
                w = jnp.where(past, w, 0.0)
            acc = acc + _dot(w.astype(BF16), v)
            csum = csum + jnp.sum(log_keep, axis=1, keepdims=True)
            out.append((acc, csum))
        return tuple(out)

    def q_body(qi, _):
        q = q_ref[pl.ds(pl.multiple_of(qi * tile, tile), tile), :]
        zero = jnp.zeros_like(q)
        q_pair = (jnp.where(first, q, zero), jnp.where(first, zero, q))
        init = tuple((jnp.zeros((tile, 2 * SB_HEAD_DIM), F32), jnp.zeros((tile, 1), F32))
                     for _ in range(2))
        carry = k_tile(q_pair, qi, init, True)
        carry = lax.fori_loop(0, qi, lambda s, c: k_tile(q_pair, qi - 1 - s, c, False), carry)
        o = jnp.where(first, carry[0][0], carry[1][0])
        o_ref[pl.ds(pl.multiple_of(qi * tile, tile), tile), :] = o.astype(o_ref.dtype)
        return 0

    lax.fori_loop(0, nq, q_body, 0)


def _sb_mixer(h, bsz, seqlen, g_norm, w_in, w_out):
    t, d = h.shape
    hd = w_in.shape[1] // 3
    tm = ROW_TILE
    qkv = pl.pallas_call(
        functools.partial(_sb_in_kernel, q_cols=hd, q_scale=SB_HEAD_DIM ** -0.5),
        grid=(t // tm,),
        in_specs=[pl.BlockSpec((tm, d), lambda i: (i, 0)),
                  _const_spec((1, d)),
                  _const_spec((d, 3 * hd))],
        out_specs=pl.BlockSpec((tm, 3 * hd), lambda i: (i, 0)),
        out_shape=jax.ShapeDtypeStruct((t, 3 * hd), BF16),
        compiler_params=_cparams("arbitrary"),
        name="sb_in",
    )(h, g_norm.reshape(1, d), w_in.astype(BF16))
    qkv3 = qkv.reshape(bsz, seqlen, 3 * hd)
    pw = 2 * SB_HEAD_DIM
    n_pairs = hd // pw
    o = pl.pallas_call(
        functools.partial(_sb_attn_kernel, tile=min(ATTN_TILE, seqlen)),
        grid=(bsz, n_pairs),
        in_specs=[pl.BlockSpec((None, seqlen, pw), lambda b, j: (b, 0, j)),
                  pl.BlockSpec((None, seqlen, pw), lambda b, j: (b, 0, n_pairs + j)),
                  pl.BlockSpec((None, seqlen, pw), lambda b, j: (b, 0, 2 * n_pairs + j))],
        out_specs=pl.BlockSpec((None, seqlen, pw), lambda b, j: (b, 0, j)),
        out_shape=jax.ShapeDtypeStruct((bsz, seqlen, hd), BF16),
        compiler_params=_cparams("arbitrary", "arbitrary"),
        name="sb_attn",
    )(qkv3, qkv3, qkv3)
    return _proj_residual(o.reshape(t, hd), w_out.astype(BF16), h)


def _lru_kernel(h_ref, g_ref, win_ref, cw_ref, cb_ref, wr_ref, br_ref, wi_ref, bi_ref, lam_ref,
                y_ref, pad_ref, xc_ref, a_ref, b_ref, hs_ref, carry_ref):
    bsz, tt, d = h_ref.shape
    width = y_ref.shape[2]

    @pl.when(pl.program_id(0) == 0)
    def _():
        carry_ref[...] = jnp.zeros(carry_ref.shape, F32)
        for b in range(bsz):
            pad_ref[b, 0:CONV_HALO, :] = jnp.zeros((CONV_HALO, width), F32)

    u = _rms(h_ref[...].reshape(bsz * tt, d), g_ref[...]).astype(BF16)
    proj = _dot(u, win_ref[...])
    gate_branch = proj[:, :width]
    for b in range(bsz):
        pad_ref[b, CONV_HALO:CONV_HALO + tt, :] = proj[b * tt:(b + 1) * tt, width:]
        acc = cb_ref[...] + cw_ref[LRU_CONV - 1:LRU_CONV, :] * pad_ref[b, CONV_HALO:CONV_HALO + tt, :]
        for k in range(LRU_CONV - 1):
            off = CONV_HALO - (LRU_CONV - 1) + k
            acc = acc + cw_ref[k:k + 1, :] * pad_ref[b, off:off + tt, :]
        xc_ref[b * tt:(b + 1) * tt, :] = acc
        pad_ref[b, 0:CONV_HALO, :] = pad_ref[b, tt:tt + CONV_HALO, :]

    xc = xc_ref[...]
    xcb = xc.astype(BF16)
    r = _sigmoid(_dot(xcb, wr_ref[...]) + br_ref[...])
    ig = _sigmoid(_dot(xcb, wi_ref[...]) + bi_ref[...])
    log_a = (-LRU_C * _softplus(-lam_ref[...])) * r
    a = jnp.exp(log_a)
    bterm = jnp.sqrt(1.0 - a * a) * (ig * xc)
    n_lt = width // V7X_LANES
    for c in range(n_lt):
        a_ref[c] = a[:, c * V7X_LANES:(c + 1) * V7X_LANES]
        b_ref[c] = bterm[:, c * V7X_LANES:(c + 1) * V7X_LANES]

    def step(j, hprev):
        hnew = []
        for c in range(n_lt):
            rows_j = pl.ds(j, bsz, stride=tt)
            hc = a_ref[c, rows_j, :] * hprev[c] + b_ref[c, rows_j, :]
            hs_ref[c, rows_j, :] = hc
            hnew.append(hc)
        return tuple(hnew)

    hlast = lax.fori_loop(0, tt, step, tuple(carry_ref[c] for c in range(n_lt)))
    for c in range(n_lt):
        carry_ref[c] = hlast[c]
    c0 = math.sqrt(2.0 / math.pi)
    gelu = 0.5 * gate_branch * (1.0 + jnp.tanh(c0 * (gate_branch + 0.044715 * gate_branch ** 3)))
    y = jnp.concatenate([hs_ref[c] for c in range(n_lt)], axis=1) * gelu
    for b in range(bsz):
        y_ref[b] = y[b * tt:(b + 1) * tt, :].astype(y_ref.dtype)


def _block_diag(w):
    nb, bd, _ = w.shape
    eye = jnp.eye(nb, dtype=w.dtype)
    return (eye[:, None, :, None] * w[:, :, None, :]).reshape(nb * bd, nb * bd)


def _lru_mixer(h, bsz, seqlen, g_norm, w_in, conv_w, conv_b, w_rg, b_rg, w_ig, b_ig, lam, w_out):
    t, d = h.shape
    width = lam.shape[0]
    tt = min(LRU_TIME_TILE, seqlen)
    rows = bsz * tt
    y = pl.pallas_call(
        _lru_kernel,
        grid=(seqlen // tt,),
        in_specs=[pl.BlockSpec((bsz, tt, d), lambda i: (0, i, 0)),
                  _const_spec((1, d)),
                  _const_spec((d, 2 * width)),
                  _const_spec((LRU_CONV, width)), _const_spec((1, width)),
                  _const_spec((width, width)), _const_spec((1, width)),
                  _const_spec((width, width)), _const_spec((1, width)),
                  _const_spec((1, width))],
        out_specs=pl.BlockSpec((bsz, tt, width), lambda i: (0, i, 0)),
        out_shape=jax.ShapeDtypeStruct((bsz, seqlen, width), BF16),
        scratch_shapes=[pltpu.VMEM((bsz, tt + CONV_HALO, width), F32),
                        pltpu.VMEM((rows, width), F32),
                        pltpu.VMEM((width // V7X_LANES, rows, V7X_LANES), F32),
                        pltpu.VMEM((width // V7X_LANES, rows, V7X_LANES), F32),
                        pltpu.VMEM((width // V7X_LANES, rows, V7X_LANES), F32),
                        pltpu.VMEM((width // V7X_LANES, bsz, V7X_LANES), F32)],
        compiler_params=_cparams("arbitrary"),
        name="lru",
    )(h.reshape(bsz, seqlen, d), g_norm.reshape(1, d), w_in.astype(BF16),
      conv_w, conv_b.reshape(1, width),
      _block_diag(w_rg).astype(BF16), b_rg.reshape(1, width),
      _block_diag(w_ig).astype(BF16), b_ig.reshape(1, width), lam.reshape(1, width))
    return _proj_residual(y.reshape(t, width), w_out.astype(BF16), h)


def _diff_in_kernel(h_ref, g_ref, w_ref, cos_ref, sin_ref, qn_ref, kn_ref, q_ref, k_ref, v_ref,
                    *, q_scale):
    d_q = q_ref.shape[1]
    hdim = DIFF_HEAD_DIM
    n_grp = d_q // hdim
    u = _rms(h_ref[...], g_ref[...]).astype(BF16)
    v_ref[...] = _dot(u, w_ref[:, 2 * d_q:]).astype(v_ref.dtype)

    ci = lax.broadcasted_iota(jnp.int32, (d_q, n_grp), 0)
    gi = lax.broadcasted_iota(jnp.int32, (d_q, n_grp), 1)
    gsum = (ci // hdim == gi).astype(BF16)
    gj = lax.broadcasted_iota(jnp.int32, (2 * n_grp, d_q), 0)
    cj = lax.broadcasted_iota(jnp.int32, (2 * n_grp, d_q), 1)
    gexp = (cj // hdim == gj % n_grp).astype(BF16)
    lane = lax.broadcasted_iota(jnp.int32, (1, d_q), 1)
    first_half = (lane % hdim) < (hdim // 2)
    reps = d_q // cos_ref.shape[1]
    cos_t = jnp.tile(cos_ref[...], (1, reps))
    sin_t = jnp.tile(sin_ref[...], (1, reps))

    def norm_rope(x, nw):
        sq = x * x
        sq_hi = sq.astype(BF16)
        sq_lo = (sq - sq_hi.astype(F32)).astype(BF16)
        ms = (_dot(sq_hi, gsum) + _dot(sq_lo, gsum)) * (1.0 / hdim)
        inv = lax.rsqrt(ms + RMS_EPS)
        inv_hi = inv.astype(BF16)
        inv_lo = (inv - inv_hi.astype(F32)).astype(BF16)
        inv_full = _dot(jnp.concatenate([inv_hi, inv_lo], axis=1), gexp)
        xn = x * inv_full * nw
        rot = jnp.where(first_half, pltpu.roll(xn, d_q - hdim // 2, 1), pltpu.roll(xn, hdim // 2, 1))
        return xn * cos_t + rot * sin_t

    q_ref[...] = (norm_rope(_dot(u, w_ref[:, :d_q]), qn_ref[...]) * q_scale).astype(q_ref.dtype)
    k_ref[...] = norm_rope(_dot(u, w_ref[:, d_q:2 * d_q]), kn_ref[...]).astype(k_ref.dtype)


def _diff_attn_kernel(q_ref, k_ref, v_ref, lq1_ref, lk1_ref, lq2_ref, lk2_ref, sn_ref, o_ref,
                      *, tile, lambda_init):
    seqlen = q_ref.shape[0]
    nq = seqlen // tile
    hdim = DIFF_HEAD_DIM
    lane = lax.broadcasted_iota(jnp.int32, (1, 2 * hdim), 1)
    first = lane < hdim
    row = lax.broadcasted_iota(jnp.int32, (tile, tile), 0)
    col = lax.broadcasted_iota(jnp.int32, (tile, tile), 1)
    causal = col <= row
    lam = (jnp.exp(jnp.sum(lq1_ref[...] * lk1_ref[...], axis=1, keepdims=True))
           - jnp.exp(jnp.sum(lq2_ref[...] * lk2_ref[...], axis=1, keepdims=True)) + lambda_init)

    def k_tile(q_pair, kj, carry, diagonal):
        k = k_ref[pl.ds(pl.multiple_of(kj * tile, tile), tile), :]
        v = v_ref[pl.ds(pl.multiple_of(kj * tile, tile), tile), :]
        out = []
        for m in range(2):
            mx, den, acc = carry[m]
            s = _dot_nt(q_pair[m], k)
            if diagonal:
                s = jnp.where(causal, s, -jnp.inf)
            mx_new = jnp.maximum(mx, jnp.max(s, axis=1, keepdims=True))
            alpha = jnp.exp(mx - mx_new)
            pr = jnp.exp(s - mx_new)
            den = den * alpha + jnp.sum(pr, axis=1, keepdims=True)
            acc = acc * alpha + _dot(pr.astype(BF16), v)
            out.append((mx_new, den, acc))
        return tuple(out)

    def q_body(qi, _):
        q = q_ref[pl.ds(pl.multiple_of(qi * tile, tile), tile), :]
        zero = jnp.zeros_like(q)
        q_pair = (jnp.where(first, q, zero), jnp.where(first, zero, q))
        init = tuple((jnp.full((tile, 1), -jnp.inf, F32), jnp.zeros((tile, 1), F32),
                      jnp.zeros((tile, 2 * hdim), F32)) for _ in range(2))
        carry = k_tile(q_pair, qi, init, True)
        carry = lax.fori_loop(0, qi, lambda s, c: k_tile(q_pair, qi - 1 - s, c, False), carry)
        o = carry[0][2] / carry[0][1] - lam * (carry[1][2] / carry[1][1])
        o = _rms(o, sn_ref[...]) * (1.0 - lambda_init)
        o_ref[pl.ds(pl.multiple_of(qi * tile, tile), tile), :] = o.astype(o_ref.dtype)
        return 0

    lax.fori_loop(0, nq, q_body, 0)


def _diff_mixer(h, bsz, seqlen, cos_t, sin_t, g_norm, w_in, q_norm, k_norm, lq1, lk1, lq2, lk2,
                sub_norm, w_out, lambda_init):
    t, d = h.shape
    d_q = w_in.shape[1] // 3
    hdim = DIFF_HEAD_DIM
    tm = ROW_TILE
    tw = cos_t.shape[1]
    q, k, v = pl.pallas_call(
        functools.partial(_diff_in_kernel, q_scale=hdim ** -0.5),
        grid=(t // tm,),
        in_specs=[pl.BlockSpec((tm, d), lambda i: (i, 0)),
                  _const_spec((1, d)),
                  _const_spec((d, 3 * d_q)),
                  pl.BlockSpec((tm, tw), lambda i: (i, 0)),
                  pl.BlockSpec((tm, tw), lambda i: (i, 0)),
                  _const_spec((1, d_q)), _const_spec((1, d_q))],
        out_specs=[pl.BlockSpec((tm, d_q), lambda i: (i, 0))] * 3,
        out_shape=[jax.ShapeDtypeStruct((t, d_q), BF16)] * 3,
        compiler_params=_cparams("arbitrary"),
        name="diff_in",
    )(h, g_norm.reshape(1, d), w_in.astype(BF16), cos_t, sin_t,
      jnp.tile(q_norm, d_q // hdim).reshape(1, d_q), jnp.tile(k_norm, d_q // hdim).reshape(1, d_q))
    hw = 2 * hdim
    n_heads = d_q // hw
    spec = pl.BlockSpec((None, seqlen, hw), lambda b, j: (b, 0, j))
    vec = _const_spec((1, hdim))
    o = pl.pallas_call(
        functools.partial(_diff_attn_kernel, tile=min(ATTN_TILE, seqlen), lambda_init=lambda_init),
        grid=(bsz, n_heads),
        in_specs=[spec, spec, spec, vec, vec, vec, vec, _const_spec((1, hw))],
        out_specs=spec,
        out_shape=jax.ShapeDtypeStruct((bsz, seqlen, d_q), BF16),
        compiler_params=_cparams("arbitrary", "arbitrary"),
        name="diff_attn",
    )(q.reshape(bsz, seqlen, d_q), k.reshape(bsz, seqlen, d_q), v.reshape(bsz, seqlen, d_q),
      lq1.reshape(1, hdim), lk1.reshape(1, hdim), lq2.reshape(1, hdim), lk2.reshape(1, hdim),
      sub_norm.reshape(1, hw))
    return _proj_residual(o.reshape(t, d_q), w_out.astype(BF16), h)


_R_E1, _R_E2, _R_W1, _R_W2, _R_RANK1, _R_RANK2 = range(6)


def _router_kernel(h_ref, g_ref, w_ref, b_ref, rec_ref, cnt_ref, run_ref):
    tm = h_ref.shape[0]
    ne, ng, epg = MOE_EXPERTS, MOE_GROUPS, MOE_EXPERTS_PER_GROUP

    @pl.when(pl.program_id(0) == 0)
    def _():
        run_ref[...] = jnp.zeros(run_ref.shape, F32)

    u = _rms(h_ref[...], g_ref[...])
    logits = _dot(u, w_ref[...], precision=HIGHEST) + b_ref[...]
    lane = lax.broadcasted_iota(jnp.int32, logits.shape, 1)
    lane_f = lane.astype(F32)
    lane_group = (lane // epg).astype(F32)
    neg = -jnp.inf
    big = float(V7X_LANES)

    gl = jnp.where((lane >= ne) & (lane < ne + ng), logits, neg)
    gmax = jnp.max(gl, axis=1, keepdims=True)
    gsel = jnp.min(jnp.where(gl == gmax, lane_f - ne, big), axis=1, keepdims=True)
    g_gate = 1.0 / jnp.sum(jnp.exp(gl - gmax), axis=1, keepdims=True)

    el = jnp.where((lane < ne) & (lane_group == gsel), logits, neg)
    m1 = jnp.max(el, axis=1, keepdims=True)
    i1 = jnp.min(jnp.where(el == m1, lane_f, big), axis=1, keepdims=True)
    el2 = jnp.where(lane_f == i1, neg, el)
    m2 = jnp.max(el2, axis=1, keepdims=True)
    i2 = jnp.min(jnp.where(el2 == m2, lane_f, big), axis=1, keepdims=True)
    e21 = jnp.exp(m2 - m1)
    w1 = g_gate / (1.0 + e21)
    w2 = g_gate * e21 / (1.0 + e21)

    oh1 = lane_f == i1
    oh2 = lane_f == i2
    oh = jnp.where(oh1 | oh2, 1.0, 0.0)
    row = lax.broadcasted_iota(jnp.int32, (tm, tm), 0)
    col = lax.broadcasted_iota(jnp.int32, (tm, tm), 1)
    before = _dot((col < row).astype(BF16), oh.astype(BF16))
    pos = run_ref[0:1, :] + before
    rank1 = jnp.sum(jnp.where(oh1, pos, 0.0), axis=1, keepdims=True)
    rank2 = jnp.sum(jnp.where(oh2, pos, 0.0), axis=1, keepdims=True)
    run_ref[...] = run_ref[...] + jnp.sum(oh, axis=0, keepdims=True)
    cnt_ref[...] = run_ref[...]

    rec = jnp.zeros(logits.shape, F32)
    for idx, val in ((_R_E1, i1), (_R_E2, i2), (_R_W1, w1), (_R_W2, w2),
                     (_R_RANK1, rank1), (_R_RANK2, rank2)):
        rec = jnp.where(lane == idx, val, rec)
    rec_ref[...] = rec


def _dispatch_kernel(dest_ref, h_ref, g_ref, rows_in_ref, rows_ref, u_ref, sem):
    del rows_in_ref
    tm = h_ref.shape[0]
    u_ref[...] = _rms(h_ref[...], g_ref[...])

    def row_copy(tok, slot):
        return pltpu.make_async_copy(u_ref.at[tok], rows_ref.at[dest_ref[2 * tok + slot]], sem)

    def issue(tok, _):
        row_copy(tok, 0).start()
        row_copy(tok, 1).start()
        return 0

    def drain(tok, _):
        row_copy(tok, 0).wait()
        row_copy(tok, 1).wait()
        return 0

    lax.fori_loop(0, tm, issue, 0)
    lax.fori_loop(0, tm, drain, 0)


def _expert_kernel(be_ref, nu_ref, x_ref, wg_ref, wu_ref, wd_ref, y_ref, wgb_ref, wub_ref, wdb_ref):
    i = pl.program_id(0)

    @pl.when(i < nu_ref[0])
    def _():
        changed = jnp.logical_or(i == 0, be_ref[i] != be_ref[jnp.maximum(i - 1, 0)])

        @pl.when(changed)
        def _():
            wgb_ref[...] = wg_ref[...].astype(BF16)
            wub_ref[...] = wu_ref[...].astype(BF16)
            wdb_ref[...] = wd_ref[...].astype(BF16)

        x = x_ref[...].astype(BF16)
        gate = _dot(x, wgb_ref[...])
        up = _dot(x, wub_ref[...])
        hid = (gate * _sigmoid(gate) * up).astype(BF16)
        y_ref[...] = _dot(hid, wdb_ref[...])

    @pl.when(i >= nu_ref[0])
    def _():
        y_ref[...] = jnp.zeros(y_ref.shape, y_ref.dtype)


def _combine_kernel(dest_ref, h_ref, rec_ref, rows_ref, o_ref, y0_ref, y1_ref, sem):
    tm = h_ref.shape[0]

    def row_copy(tok, slot, buf):
        return pltpu.make_async_copy(rows_ref.at[dest_ref[2 * tok + slot]], buf.at[tok], sem)

    def issue(tok, _):
        row_copy(tok, 0, y0_ref).start()
        row_copy(tok, 1, y1_ref).start()
        return 0

    def drain(tok, _):
        row_copy(tok, 0, y0_ref).wait()
        row_copy(tok, 1, y1_ref).wait()
        return 0

    lax.fori_loop(0, tm, issue, 0)
    lax.fori_loop(0, tm, drain, 0)
    rec = rec_ref[...]
    w1 = rec[:, _R_W1:_R_W1 + 1]
    w2 = rec[:, _R_W2:_R_W2 + 1]
    o_ref[...] = h_ref[...] + (w1 * y0_ref[...] + w2 * y1_ref[...])


def _hier_moe(h, g_norm, w_group, b_group, w_expert, b_expert, w_gate, w_up, w_down):
    t, d = h.shape
    ne, ng = MOE_EXPERTS, MOE_GROUPS
    ff = w_gate.shape[2]
    tm = min(MOE_TOKEN_TILE, t)
    g2 = g_norm.reshape(1, d)
    w_router = jnp.zeros((d, V7X_LANES), F32).at[:, :ne].set(w_expert).at[:, ne:ne + ng].set(w_group)
    b_router = jnp.zeros((1, V7X_LANES), F32).at[0, :ne].set(b_expert).at[0, ne:ne + ng].set(b_group)
    rec, cnt = pl.pallas_call(
        _router_kernel,
        grid=(t // tm,),
        in_specs=[pl.BlockSpec((tm, d), lambda i: (i, 0)),
                  _const_spec((1, d)),
                  _const_spec((d, V7X_LANES)),
                  _const_spec((1, V7X_LANES))],
        out_specs=[pl.BlockSpec((tm, V7X_LANES), lambda i: (i, 0)),
                   _const_spec((V7X_SUBLANES, V7X_LANES))],
        out_shape=[jax.ShapeDtypeStruct((t, V7X_LANES), F32),
                   jax.ShapeDtypeStruct((V7X_SUBLANES, V7X_LANES), F32)],
        scratch_shapes=[pltpu.VMEM((V7X_SUBLANES, V7X_LANES), F32)],
        compiler_params=_cparams("arbitrary"),
        name="moe_router",
    )(h, g2, w_router, b_router)

    br = MOE_ROW_BLOCK
    n_blocks = -(-(2 * t + ne * (br - 1)) // br)
    n_rows = n_blocks * br
    counts = cnt[0, :ne].astype(jnp.int32)
    padded = (counts + br - 1) // br * br
    pad_end = jnp.cumsum(padded)
    pad_start = pad_end - padded
    eid = rec[:, _R_E1:_R_E2 + 1].astype(jnp.int32)
    rank = rec[:, _R_RANK1:_R_RANK2 + 1].astype(jnp.int32)
    dest = (pad_start[eid] + rank).reshape(-1)
    block_expert = jnp.minimum(
        jnp.searchsorted(pad_end, jnp.arange(n_blocks, dtype=jnp.int32) * br, side='right'),
        ne - 1).astype(jnp.int32)
    n_used = (pad_end[-1:] // br).astype(jnp.int32)

    x_rows = pl.pallas_call(
        _dispatch_kernel,
        grid_spec=pltpu.PrefetchScalarGridSpec(
            num_scalar_prefetch=0,
            grid=(t // tm,),
            in_specs=[pl.BlockSpec((2 * tm,), lambda i: (i,), memory_space=pltpu.SMEM),
                      pl.BlockSpec((tm, d), lambda i: (i, 0)),
                      _const_spec((1, d)),
                      pl.BlockSpec(memory_space=pl.ANY)],
            out_specs=pl.BlockSpec(memory_space=pl.ANY),
            scratch_shapes=[pltpu.VMEM((tm, d), F32), pltpu.SemaphoreType.DMA(())]),
        out_shape=jax.ShapeDtypeStruct((n_rows, d), F32),
        input_output_aliases={3: 0},
        compiler_params=_cparams("arbitrary"),
        name="moe_dispatch",
    )(dest, h, g2, jnp.zeros((n_rows, d), F32))

    def blk(i, be, nu):
        return jnp.minimum(i, nu[0] - 1)

    y_rows = pl.pallas_call(
        _expert_kernel,
        grid_spec=pltpu.PrefetchScalarGridSpec(
            num_scalar_prefetch=2,
            grid=(n_blocks,),
            in_specs=[pl.BlockSpec((br, d), lambda i, be, nu: (blk(i, be, nu), 0)),
                      pl.BlockSpec((None, d, ff), lambda i, be, nu: (be[blk(i, be, nu)], 0, 0)),
                      pl.BlockSpec((None, d, ff), lambda i, be, nu: (be[blk(i, be, nu)], 0, 0)),
                      pl.BlockSpec((None, ff, d), lambda i, be, nu: (be[blk(i, be, nu)], 0, 0))],
            out_specs=pl.BlockSpec((br, d), lambda i, be, nu: (i, 0)),
            scratch_shapes=[pltpu.VMEM((d, ff), BF16), pltpu.VMEM((d, ff), BF16),
                            pltpu.VMEM((ff, d), BF16)]),
        out_shape=jax.ShapeDtypeStruct((n_rows, d), F32),
        compiler_params=_cparams("arbitrary"),
        name="moe_experts",
    )(block_expert, n_used, x_rows, w_gate, w_up, w_down)

    return pl.pallas_call(
        _combine_kernel,
        grid_spec=pltpu.PrefetchScalarGridSpec(
            num_scalar_prefetch=0,
            grid=(t // tm,),
            in_specs=[pl.BlockSpec((2 * tm,), lambda i: (i,), memory_space=pltpu.SMEM),
                      pl.BlockSpec((tm, d), lambda i: (i, 0)),
                      pl.BlockSpec((tm, V7X_LANES), lambda i: (i, 0)),
                      pl.BlockSpec(memory_space=pl.ANY)],
            out_specs=pl.BlockSpec((tm, d), lambda i: (i, 0)),
            scratch_shapes=[pltpu.VMEM((tm, d), F32), pltpu.VMEM((tm, d), F32),
                            pltpu.SemaphoreType.DMA(())]),
        out_shape=jax.ShapeDtypeStruct((t, d), F32),
        compiler_params=_cparams("arbitrary"),
        name="moe_combine",
    )(dest, h, rec, y_rows)


def _rope_tables(positions):
    half = DIFF_HEAD_DIM // 2
    inv = ROPE_THETA ** (-jnp.arange(0, DIFF_HEAD_DIM, 2, dtype=F32) / DIFF_HEAD_DIM)
    ang = positions.astype(F32).reshape(-1, 1) * inv
    cos, sin = jnp.cos(ang), jnp.sin(ang)
    cos_t = jnp.concatenate([cos] * 4, axis=1)
    sin_t = jnp.concatenate([-sin, sin, -sin, sin], axis=1)
    assert cos_t.shape[1] == 4 * half
    return cos_t, sin_t


def kernel(x, positions, norm_mix, norm_ffn, ssd_w_in, ssd_conv_w, ssd_conv_b, ssd_dt_bias, ssd_a_log, ssd_d, ssd_norm, ssd_w_out, sb_w_in, sb_w_out, lru_w_in, lru_conv_w, lru_conv_b, lru_w_rg, lru_b_rg, lru_w_ig, lru_b_ig, lru_lambda, lru_w_out, diff_w_in, diff_q_norm, diff_k_norm, diff_lam_q1, diff_lam_k1, diff_lam_q2, diff_lam_k2, diff_sub_norm, diff_w_out, moe_w_group, moe_b_group, moe_w_expert, moe_b_expert, moe_w_gate, moe_w_up, moe_w_down):
    bsz, seqlen, d = x.shape
    depth = norm_mix.shape[0]
    h = x.reshape(bsz * seqlen, d)
    cos_t, sin_t = _rope_tables(positions)
    for i in range(depth):
        m, j = i % N_MIXERS, i // N_MIXERS
        if m == 0:
            h = _ssd_mixer(h, bsz, seqlen, norm_mix[i], ssd_w_in[j], ssd_conv_w[j], ssd_conv_b[j],
                           ssd_dt_bias[j], ssd_a_log[j], ssd_d[j], ssd_norm[j], ssd_w_out[j])
        elif m == 1:
            h = _sb_mixer(h, bsz, seqlen, norm_mix[i], sb_w_in[j], sb_w_out[j])
        elif m == 2:
            h = _lru_mixer(h, bsz, seqlen, norm_mix[i], lru_w_in[j], lru_conv_w[j], lru_conv_b[j],
                           lru_w_rg[j], lru_b_rg[j], lru_w_ig[j], lru_b_ig[j], lru_lambda[j],
                           lru_w_out[j])
        else:
            lambda_init = 0.8 - 0.6 * math.exp(-0.3 * i)
            h = _diff_mixer(h, bsz, seqlen, cos_t, sin_t, norm_mix[i], diff_w_in[j], diff_q_norm[j],
                            diff_k_norm[j], diff_lam_q1[j], diff_lam_k1[j], diff_lam_q2[j],
                            diff_lam_k2[j], diff_sub_norm[j], diff_w_out[j], lambda_init)
        h = _hier_moe(h, norm_ffn[i], moe_w_group[i], moe_b_group[i], moe_w_expert[i],
                      moe_b_expert[i], moe_w_gate[i], moe_w_up[i], moe_w_down[i])
    return h.reshape(bsz, seqlen, d)
```

```python
import functools
import math

import jax
import jax.numpy as jnp
from jax import lax
from jax.experimental import pallas as pl
from jax.experimental.pallas import tpu as pltpu

F32 = jnp.float32
BF16 = jnp.bfloat16
HIGHEST = lax.Precision.HIGHEST

RMS_EPS = 1e-6
_LOG2_E = math.log2(math.e)
N_MIXERS = 4

V7X_VMEM_BYTES = 64 * 1024 * 1024
V7X_LANES = 128
V7X_SUBLANES = 8
VMEM_LIMIT = (V7X_VMEM_BYTES * 7) // 8

SSD_HEAD_DIM = 64
SSD_GROUPS = 8
SSD_STATE = 128
SSD_CONV = 4
SSD_CHUNK = 128
SB_HEAD_DIM = 64
LRU_BLOCKS = 16
LRU_CONV = 4
LRU_C = 8.0
DIFF_HEAD_DIM = 64
ROPE_THETA = 10000.0
MOE_GROUPS = 4
MOE_EXPERTS_PER_GROUP = 8
MOE_EXPERTS = MOE_GROUPS * MOE_EXPERTS_PER_GROUP

ROW_TILE = 512
SSD_IN_TILE = 256
ATTN_TILE = 256
SB_PAIRS_PER_STEP = 4
DIFF_HEADS_PER_STEP = 4
LRU_TIME_TILE = 64
MOE_TOKEN_TILE = 512
MOE_ROW_BLOCK = 256
CONV_HALO = V7X_SUBLANES
DMA_LOOP_UNROLL = 8


def _cparams(*sem):
    return pltpu.CompilerParams(dimension_semantics=sem, vmem_limit_bytes=VMEM_LIMIT)


def _const_spec(shape):
    nd = len(shape)
    return pl.BlockSpec(shape, lambda *_: (0,) * nd)


def _rms(x, w):
    ms = jnp.mean(x * x, axis=-1, keepdims=True)
    return x * lax.rsqrt(ms + RMS_EPS) * w


def _sigmoid(x):
    return 1.0 / (1.0 + jnp.exp(-x))


def _softplus(x):
    return jnp.maximum(x, 0.0) + jnp.log(1.0 + jnp.exp(-jnp.abs(x)))


def _dot(a, b, **kw):
    return jnp.dot(a, b, preferred_element_type=F32, **kw)


def _dot_nt(a, b):
    return lax.dot_general(a, b, (((1,), (1,)), ((), ())), preferred_element_type=F32)


def _dot_tn(a, b):
    return lax.dot_general(a, b, (((0,), (0,)), ((), ())), preferred_element_type=F32)


def _proj_residual_kernel(y_ref, w_ref, h_ref, o_ref):
    o_ref[...] = h_ref[...] + _dot(y_ref[...], w_ref[...])


def _proj_residual(y, w, h):
    t, k = y.shape
    d = w.shape[1]
    tm = ROW_TILE
    return pl.pallas_call(
        _proj_residual_kernel,
        grid=(t // tm,),
        in_specs=[pl.BlockSpec((tm, k), lambda i: (i, 0)),
                  _const_spec((k, d)),
                  pl.BlockSpec((tm, d), lambda i: (i, 0))],
        out_specs=pl.BlockSpec((tm, d), lambda i: (i, 0)),
        out_shape=jax.ShapeDtypeStruct((t, d), F32),
        compiler_params=_cparams("arbitrary"),
        name="proj_residual",
    )(y, w, h)


def _ssd_in_kernel(h_ref, g_ref, wz_ref, wx_ref, wdt_ref, cw_ref, cb_ref,
                   z_ref, xbc_ref, dt_ref, pad_ref, *, tiles_per_seq):
    i = pl.program_id(0)
    tm = h_ref.shape[0]
    u32 = _rms(h_ref[...], g_ref[...])
    u = u32.astype(BF16)
    z_ref[...] = _dot(u, wz_ref[...]).astype(z_ref.dtype)
    dt_ref[...] = _dot(u32, wdt_ref[...], precision=HIGHEST)

    @pl.when(i % tiles_per_seq == 0)
    def _():
        pad_ref[0:CONV_HALO, :] = jnp.zeros((CONV_HALO, pad_ref.shape[1]), F32)

    pad_ref[CONV_HALO:CONV_HALO + tm, :] = _dot(u, wx_ref[...])
    acc = cb_ref[...] + cw_ref[SSD_CONV - 1:SSD_CONV, :] * pad_ref[CONV_HALO:CONV_HALO + tm, :]
    for k in range(SSD_CONV - 1):
        off = CONV_HALO - (SSD_CONV - 1) + k
        acc = acc + cw_ref[k:k + 1, :] * pad_ref[off:off + tm, :]
    xbc_ref[...] = (acc * _sigmoid(acc)).astype(xbc_ref.dtype)
    pad_ref[0:CONV_HALO, :] = pad_ref[tm:tm + CONV_HALO, :]


def _ssd_scan_kernel(xs_ref, b_ref, c_ref, z_ref, dt_ref, dtt_ref,
                     bias_ref, biast_ref, alog_ref, alogt_ref, dfull_ref, nw_ref,
                     o_ref, state_ref):
    q = xs_ref.shape[0]
    n_heads = dt_ref.shape[1]
    d_inner = xs_ref.shape[1]
    heads_per_group = n_heads // SSD_GROUPS
    gw = d_inner // SSD_GROUPS
    p = SSD_HEAD_DIM

    @pl.when(pl.program_id(1) == 0)
    def _():
        state_ref[...] = jnp.zeros(state_ref.shape, F32)

    dt = _softplus(dt_ref[...] + bias_ref[...])
    a = -jnp.exp(alog_ref[...]) * dt
    dtt = _softplus(dtt_ref[...] + biast_ref[...])
    at = -jnp.exp(alogt_ref[...]) * dtt

    row = lax.broadcasted_iota(jnp.int32, (q, q), 0)
    col = lax.broadcasted_iota(jnp.int32, (q, q), 1)
    lower = row >= col
    a_cum = _dot(lower.astype(F32), a, precision=HIGHEST)
    a_cumt = _dot(at, (row <= col).astype(F32), precision=HIGHEST)
    a_tot = a_cum[q - 1:q, :]

    hh = lax.broadcasted_iota(jnp.int32, (n_heads, d_inner), 0)
    cc = lax.broadcasted_iota(jnp.int32, (n_heads, d_inner), 1)
    expand = (cc // p == hh).astype(F32)
    dt_full = _dot(dt, expand, precision=HIGHEST)
    dec_out_full = _dot(jnp.exp(a_cum), expand, precision=HIGHEST)
    dec_st_full = _dot(jnp.exp(a_tot - a_cum), expand, precision=HIGHEST)
    chunk_dec_full = _dot(jnp.broadcast_to(jnp.exp(a_tot), (V7X_SUBLANES, n_heads)), expand,
                          precision=HIGHEST)[0:1, :]

    lane = lax.broadcasted_iota(jnp.int32, (1, gw), 1)
    for g in range(SSD_GROUPS):
        sl = slice(g * gw, (g + 1) * gw)
        bg = b_ref[:, g * SSD_STATE:(g + 1) * SSD_STATE]
        cg = c_ref[:, g * SSD_STATE:(g + 1) * SSD_STATE]
        xs_g = xs_ref[:, sl].astype(F32)
        xdt = xs_g * dt_full[:, sl]
        cb = _dot_nt(cg, bg)
        s_prev = state_ref[g]
        y = _dot(cg, s_prev.astype(BF16)) * dec_out_full[:, sl]
        for r in range(heads_per_group):
            hd = g * heads_per_group + r
            seg = a_cum[:, hd:hd + 1] - a_cumt[hd:hd + 1, :]
            m = jnp.where(lower, cb * jnp.exp(seg), 0.0).astype(BF16)
            xr = jnp.where((lane >= r * p) & (lane < (r + 1) * p), xdt, 0.0).astype(BF16)
            y = y + _dot(m, xr)
        xw = (xdt * dec_st_full[:, sl]).astype(BF16)
        state_ref[g] = s_prev * chunk_dec_full[:, sl] + _dot_tn(bg, xw)
        y = y + xs_g * dfull_ref[:, sl]
        zg = z_ref[:, sl].astype(F32)
        yz = y * (zg * _sigmoid(zg))
        o_ref[:, sl] = _rms(yz, nw_ref[:, sl]).astype(o_ref.dtype)


def _ssd_mixer(h, bsz, seqlen, g_norm, w_in, conv_w, conv_b, dt_bias, a_log, d_skip, norm_w, w_out):
    t, d = h.shape
    n_heads = dt_bias.shape[0]
    d_inner = n_heads * SSD_HEAD_DIM
    gn = SSD_GROUPS * SSD_STATE
    conv_dim = d_inner + 2 * gn
    wz = w_in[:, :d_inner].astype(BF16)
    wx = w_in[:, d_inner:d_inner + conv_dim].astype(BF16)
    wdt = w_in[:, d_inner + conv_dim:]
    tm = SSD_IN_TILE
    z, xbc, dt_raw = pl.pallas_call(
        functools.partial(_ssd_in_kernel, tiles_per_seq=seqlen // tm),
        grid=(t // tm,),
        in_specs=[pl.BlockSpec((tm, d), lambda i: (i, 0)),
                  _const_spec((1, d)),
                  _const_spec((d, d_inner)),
                  _const_spec((d, conv_dim)),
                  _const_spec((d, n_heads)),
                  _const_spec((SSD_CONV, conv_dim)),
                  _const_spec((1, conv_dim))],
        out_specs=[pl.BlockSpec((tm, d_inner), lambda i: (i, 0)),
                   pl.BlockSpec((tm, conv_dim), lambda i: (i, 0)),
                   pl.BlockSpec((tm, n_heads), lambda i: (i, 0))],
        out_shape=[jax.ShapeDtypeStruct((t, d_inner), BF16),
                   jax.ShapeDtypeStruct((t, conv_dim), BF16),
                   jax.ShapeDtypeStruct((t, n_heads), F32)],
        scratch_shapes=[pltpu.VMEM((tm + CONV_HALO, conv_dim), F32)],
        compiler_params=_cparams("arbitrary"),
        name="ssd_in",
    )(h, g_norm.reshape(1, d), wz, wx, wdt, conv_w, conv_b.reshape(1, conv_dim))

    q = SSD_CHUNK
    nc = seqlen // q
    xbc3 = xbc.reshape(bsz, seqlen, conv_dim)
    dt3 = dt_raw.reshape(bsz, seqlen, n_heads)
    dtt3 = jnp.swapaxes(dt3, 1, 2)
    nb_x = d_inner // gn
    yg = pl.pallas_call(
        _ssd_scan_kernel,
        grid=(bsz, nc),
        in_specs=[pl.BlockSpec((None, q, d_inner), lambda b, c: (b, c, 0)),
                  pl.BlockSpec((None, q, gn), lambda b, c: (b, c, nb_x)),
                  pl.BlockSpec((None, q, gn), lambda b, c: (b, c, nb_x + 1)),
                  pl.BlockSpec((None, q, d_inner), lambda b, c: (b, c, 0)),
                  pl.BlockSpec((None, q, n_heads), lambda b, c: (b, c, 0)),
                  pl.BlockSpec((None, n_heads, q), lambda b, c: (b, 0, c)),
                  _const_spec((1, n_heads)), _const_spec((n_heads, 1)),
                  _const_spec((1, n_heads)), _const_spec((n_heads, 1)),
                  _const_spec((1, d_inner)), _const_spec((1, d_inner))],
        out_specs=pl.BlockSpec((None, q, d_inner), lambda b, c: (b, c, 0)),
        out_shape=jax.ShapeDtypeStruct((bsz, seqlen, d_inner), BF16),
        scratch_shapes=[pltpu.VMEM((SSD_GROUPS, SSD_STATE, d_inner // SSD_GROUPS), F32)],
        compiler_params=_cparams("arbitrary", "arbitrary"),
        name="ssd_scan",
    )(xbc3, xbc3, xbc3, z.reshape(bsz, seqlen, d_inner), dt3, dtt3,
      dt_bias.reshape(1, n_heads), dt_bias.reshape(n_heads, 1),
      a_log.reshape(1, n_heads), a_log.reshape(n_heads, 1),
      jnp.repeat(d_skip, SSD_HEAD_DIM).reshape(1, d_inner), norm_w.reshape(1, d_inner))
    return _proj_residual(yg.reshape(t, d_inner), w_out.astype(BF16), h)


def _sb_in_kernel(h_ref, g_ref, w_ref, o_ref, *, q_cols, q_scale):
    u = _rms(h_ref[...], g_ref[...]).astype(BF16)
    o_ref[:, :q_cols] = (_dot(u, w_ref[:, :q_cols]) * q_scale).astype(o_ref.dtype)
    o_ref[:, q_cols:] = _dot(u, w_ref[:, q_cols:]).astype(o_ref.dtype)


def _sb_attn_kernel(q_ref, k_ref, v_ref, o_ref, acc_ref, *, tile):
    seqlen = q_ref.shape[0]
    nq = seqlen // tile
    pw = 2 * SB_HEAD_DIM
    n_pairs = q_ref.shape[1] // pw
    lane = lax.broadcasted_iota(jnp.int32, (1, pw), 1)
    first = lane < SB_HEAD_DIM
    row = lax.broadcasted_iota(jnp.int32, (tile, tile), 0)
    col = lax.broadcasted_iota(jnp.int32, (tile, tile), 1)
    neg_from = jnp.where(row >= col, -1.0, 0.0).astype(BF16)
    past = col < row
    past2 = jnp.concatenate([past, past], axis=0)

    def k_tile(qs, kj, csums, diagonal):
        rows = pl.ds(pl.multiple_of(kj * tile, tile), tile)
        out = []
        for p in range(n_pairs):
            k = k_ref[rows, p * pw:(p + 1) * pw]
            v = v_ref[rows, p * pw:(p + 1) * pw]
            t = _dot_nt(qs[p], k)
            neg_abs = pltpu.bitcast(pltpu.bitcast(t, jnp.uint32) | jnp.uint32(0x80000000), F32)
            sp = jnp.maximum(t, 0.0) + jnp.log(1.0 + jnp.exp2(neg_abs)) * _LOG2_E
            if diagonal:
                sp = jnp.where(past2, sp, 0.0)
            w = jnp.exp2(t + _dot(sp.astype(BF16), neg_from) + csums[p])
            if diagonal:
                w = jnp.where(past2, w, 0.0)
            acc_ref[p] += _dot(w.astype(BF16), v)
            out.append(csums[p] - jnp.sum(sp, axis=1, keepdims=True))
        return tuple(out)

    def q_body(qi, _):
        qrows = pl.ds(pl.multiple_of(qi * tile, tile), tile)
        qs = []
        for p in range(n_pairs):
            q = q_ref[qrows, p * pw:(p + 1) * pw]
            zero = jnp.zeros_like(q)
            qs.append(jnp.concatenate([jnp.where(first, q, zero), jnp.where(first, zero, q)], axis=0))
            acc_ref[p] = jnp.zeros(acc_ref.shape[1:], F32)
        init = tuple(jnp.zeros((2 * tile, 1), F32) for _ in range(n_pairs))
        csums = k_tile(qs, qi, init, True)
        lax.fori_loop(0, qi, lambda s, c: k_tile(qs, qi - 1 - s, c, False), csums)
        for p in range(n_pairs):
            o = jnp.where(first, acc_ref[p, 0:tile, :], acc_ref[p, tile:2 * tile, :])
            o_ref[qrows, p * pw:(p + 1) * pw] = o.astype(o_ref.dtype)
        return 0

    lax.fori_loop(0, nq, q_body, 0)


def _sb_mixer(h, bsz, seqlen, g_norm, w_in, w_out):
    t, d = h.shape
    hd = w_in.shape[1] // 3
    tm = ROW_TILE
    qkv = pl.pallas_call(
        functools.partial(_sb_in_kernel, q_cols=hd, q_scale=SB_HEAD_DIM ** -0.5 * _LOG2_E),
        grid=(t // tm,),
        in_specs=[pl.BlockSpec((tm, d), lambda i: (i, 0)),
                  _const_spec((1, d)),
                  _const_spec((d, 3 * hd))],
        out_specs=pl.BlockSpec((tm, 3 * hd), lambda i: (i, 0)),
        out_shape=jax.ShapeDtypeStruct((t, 3 * hd), BF16),
        compiler_params=_cparams("arbitrary"),
        name="sb_in",
    )(h, g_norm.reshape(1, d), w_in.astype(BF16))
    qkv3 = qkv.reshape(bsz, seqlen, 3 * hd)
    bw = SB_PAIRS_PER_STEP * 2 * SB_HEAD_DIM
    nb = hd // bw
    tile = min(ATTN_TILE, seqlen)
    o = pl.pallas_call(
        functools.partial(_sb_attn_kernel, tile=tile),
        grid=(bsz, nb),
        in_specs=[pl.BlockSpec((None, seqlen, bw), lambda b, j: (b, 0, j)),
                  pl.BlockSpec((None, seqlen, bw), lambda b, j: (b, 0, nb + j)),
                  pl.BlockSpec((None, seqlen, bw), lambda b, j: (b, 0, 2 * nb + j))],
        out_specs=pl.BlockSpec((None, seqlen, bw), lambda b, j: (b, 0, j)),
        out_shape=jax.ShapeDtypeStruct((bsz, seqlen, hd), BF16),
        scratch_shapes=[pltpu.VMEM((SB_PAIRS_PER_STEP, 2 * tile, 2 * SB_HEAD_DIM), F32)],
        compiler_params=_cparams("arbitrary", "arbitrary"),
        name="sb_attn",
    )(qkv3, qkv3, qkv3)
    return _proj_residual(o.reshape(t, hd), w_out.astype(BF16), h)


def _lru_kernel(h_ref, g_ref, win_ref, cw_ref, cb_ref, wr_ref, br_ref, wi_ref, bi_ref, lam_ref,
                y_ref, pad_ref, xc_ref, a_ref, b_ref, hs_ref, carry_ref):
    bsz, tt, d = h_ref.shape
    width = y_ref.shape[2]

    @pl.when(pl.program_id(0) == 0)
    def _():
        carry_ref[...] = jnp.zeros(carry_ref.shape, F32)
        for b in range(bsz):
            pad_ref[b, 0:CONV_HALO, :] = jnp.zeros((CONV_HALO, width), F32)

    u = _rms(h_ref[...].reshape(bsz * tt, d), g_ref[...]).astype(BF16)
    proj = _dot(u, win_ref[...])
    gate_branch = proj[:, :width]
    for b in range(bsz):
        pad_ref[b, CONV_HALO:CONV_HALO + tt, :] = proj[b * tt:(b + 1) * tt, width:]
        acc = cb_ref[...] + cw_ref[LRU_CONV - 1:LRU_CONV, :] * pad_ref[b, CONV_HALO:CONV_HALO + tt, :]
        for k in range(LRU_CONV - 1):
            off = CONV_HALO - (LRU_CONV - 1) + k
            acc = acc + cw_ref[k:k + 1, :] * pad_ref[b, off:off + tt, :]
        xc_ref[b * tt:(b + 1) * tt, :] = acc
        pad_ref[b, 0:CONV_HALO, :] = pad_ref[b, tt:tt + CONV_HALO, :]

    xc = xc_ref[...]
    xcb = xc.astype(BF16)
    r = _sigmoid(_dot(xcb, wr_ref[...]) + br_ref[...])
    ig = _sigmoid(_dot(xcb, wi_ref[...]) + bi_ref[...])
    log_a = (-LRU_C * _softplus(-lam_ref[...])) * r
    a = jnp.exp(log_a)
    bterm = jnp.sqrt(1.0 - a * a) * (ig * xc)
    n_lt = width // V7X_LANES
    for c in range(n_lt):
        a_ref[c] = a[:, c * V7X_LANES:(c + 1) * V7X_LANES]
        b_ref[c] = bterm[:, c * V7X_LANES:(c + 1) * V7X_LANES]

    def step(j, hprev):
        hnew = []
        for c in range(n_lt):
            rows_j = pl.ds(j, bsz, stride=tt)
            hc = a_ref[c, rows_j, :] * hprev[c] + b_ref[c, rows_j, :]
            hs_ref[c, rows_j, :] = hc
            hnew.append(hc)
        return tuple(hnew)

    hlast = lax.fori_loop(0, tt, step, tuple(carry_ref[c] for c in range(n_lt)))
    for c in range(n_lt):
        carry_ref[c] = hlast[c]
    c0 = math.sqrt(2.0 / math.pi)
    gelu = 0.5 * gate_branch * (1.0 + jnp.tanh(c0 * (gate_branch + 0.044715 * gate_branch ** 3)))
    y = jnp.concatenate([hs_ref[c] for c in range(n_lt)], axis=1) * gelu
    for b in range(bsz):
        y_ref[b] = y[b * tt:(b + 1) * tt, :].astype(y_ref.dtype)


def _block_diag(w):
    nb, bd, _ = w.shape
    eye = jnp.eye(nb, dtype=w.dtype)
    return (eye[:, None, :, None] * w[:, :, None, :]).reshape(nb * bd, nb * bd)


def _lru_mixer(h, bsz, seqlen, g_norm, w_in, conv_w, conv_b, w_rg, b_rg, w_ig, b_ig, lam, w_out):
    t, d = h.shape
    width = lam.shape[0]
    tt = min(LRU_TIME_TILE, seqlen)
    rows = bsz * tt
    y = pl.pallas_call(
        _lru_kernel,
        grid=(seqlen // tt,),
        in_specs=[pl.BlockSpec((bsz, tt, d), lambda i: (0, i, 0)),
                  _const_spec((1, d)),
                  _const_spec((d, 2 * width)),
                  _const_spec((LRU_CONV, width)), _const_spec((1, width)),
                  _const_spec((width, width)), _const_spec((1, width)),
                  _const_spec((width, width)), _const_spec((1, width)),
                  _const_spec((1, width))],
        out_specs=pl.BlockSpec((bsz, tt, width), lambda i: (0, i, 0)),
        out_shape=jax.ShapeDtypeStruct((bsz, seqlen, width), BF16),
        scratch_shapes=[pltpu.VMEM((bsz, tt + CONV_HALO, width), F32),
                        pltpu.VMEM((rows, width), F32),
                        pltpu.VMEM((width // V7X_LANES, rows, V7X_LANES), F32),
                        pltpu.VMEM((width // V7X_LANES, rows, V7X_LANES), F32),
                        pltpu.VMEM((width // V7X_LANES, rows, V7X_LANES), F32),
                        pltpu.VMEM((width // V7X_LANES, bsz, V7X_LANES), F32)],
        compiler_params=_cparams("arbitrary"),
        name="lru",
    )(h.reshape(bsz, seqlen, d), g_norm.reshape(1, d), w_in.astype(BF16),
      conv_w, conv_b.reshape(1, width),
      _block_diag(w_rg).astype(BF16), b_rg.reshape(1, width),
      _block_diag(w_ig).astype(BF16), b_ig.reshape(1, width), lam.reshape(1, width))
    return _proj_residual(y.reshape(t, width), w_out.astype(BF16), h)


def _diff_in_kernel(h_ref, g_ref, w_ref, cos_ref, sin_ref, qn_ref, kn_ref, q_ref, k_ref, v_ref,
                    *, q_scale):
    d_q = q_ref.shape[1]
    hdim = DIFF_HEAD_DIM
    n_grp = d_q // hdim
    u = _rms(h_ref[...], g_ref[...]).astype(BF16)
    v_ref[...] = _dot(u, w_ref[:, 2 * d_q:]).astype(v_ref.dtype)

    ci = lax.broadcasted_iota(jnp.int32, (d_q, n_grp), 0)
    gi = lax.broadcasted_iota(jnp.int32, (d_q, n_grp), 1)
    gsum = (ci // hdim == gi).astype(BF16)
    gj = lax.broadcasted_iota(jnp.int32, (2 * n_grp, d_q), 0)
    cj = lax.broadcasted_iota(jnp.int32, (2 * n_grp, d_q), 1)
    gexp = (cj // hdim == gj % n_grp).astype(BF16)
    lane = lax.broadcasted_iota(jnp.int32, (1, d_q), 1)
    first_half = (lane % hdim) < (hdim // 2)
    reps = d_q // cos_ref.shape[1]
    cos_t = jnp.tile(cos_ref[...], (1, reps))
    sin_t = jnp.tile(sin_ref[...], (1, reps))

    def norm_rope(x, nw):
        sq = x * x
        sq_hi = sq.astype(BF16)
        sq_lo = (sq - sq_hi.astype(F32)).astype(BF16)
        ms = (_dot(sq_hi, gsum) + _dot(sq_lo, gsum)) * (1.0 / hdim)
        inv = lax.rsqrt(ms + RMS_EPS)
        inv_hi = inv.astype(BF16)
        inv_lo = (inv - inv_hi.astype(F32)).astype(BF16)
        inv_full = _dot(jnp.concatenate([inv_hi, inv_lo], axis=1), gexp)
        xn = x * inv_full * nw
        rot = jnp.where(first_half, pltpu.roll(xn, d_q - hdim // 2, 1), pltpu.roll(xn, hdim // 2, 1))
        return xn * cos_t + rot * sin_t

    q_ref[...] = (norm_rope(_dot(u, w_ref[:, :d_q]), qn_ref[...]) * q_scale).astype(q_ref.dtype)
    k_ref[...] = norm_rope(_dot(u, w_ref[:, d_q:2 * d_q]), kn_ref[...]).astype(k_ref.dtype)


def _diff_attn_kernel(q_ref, k_ref, v_ref, lq1_ref, lk1_ref, lq2_ref, lk2_ref, sn_ref, o_ref,
                      acc_ref, *, tile, lambda_init):
    seqlen = q_ref.shape[0]
    nq = seqlen // tile
    hdim = DIFF_HEAD_DIM
    hw = 2 * hdim
    n_heads = q_ref.shape[1] // hw
    lane = lax.broadcasted_iota(jnp.int32, (1, hw), 1)
    first = lane < hdim
    row = lax.broadcasted_iota(jnp.int32, (tile, tile), 0)
    col = lax.broadcasted_iota(jnp.int32, (tile, tile), 1)
    causal = col <= row
    causal2 = jnp.concatenate([causal, causal], axis=0)
    ones = jnp.ones((tile, hw), BF16)
    lam = (jnp.exp(jnp.sum(lq1_ref[...] * lk1_ref[...], axis=1, keepdims=True))
           - jnp.exp(jnp.sum(lq2_ref[...] * lk2_ref[...], axis=1, keepdims=True)) + lambda_init)

    def k_tile(qs, kj, maxes, diagonal):
        rows = pl.ds(pl.multiple_of(kj * tile, tile), tile)
        out = []
        for hd in range(n_heads):
            k = k_ref[rows, hd * hw:(hd + 1) * hw]
            v1 = jnp.concatenate([v_ref[rows, hd * hw:(hd + 1) * hw], ones], axis=1)
            s = _dot_nt(qs[hd], k)
            if diagonal:
                s = jnp.where(causal2, s, -jnp.inf)
            mx_new = jnp.maximum(maxes[hd], jnp.max(s, axis=1, keepdims=True))
            alpha = jnp.exp2(maxes[hd] - mx_new)
            pr = jnp.exp2(s - mx_new)
            acc_ref[hd] = acc_ref[hd] * alpha + _dot(pr.astype(BF16), v1)
            out.append(mx_new)
        return tuple(out)

    def q_body(qi, _):
        qrows = pl.ds(pl.multiple_of(qi * tile, tile), tile)
        qs = []
        for hd in range(n_heads):
            q = q_ref[qrows, hd * hw:(hd + 1) * hw]
            zero = jnp.zeros_like(q)
            qs.append(jnp.concatenate([jnp.where(first, q, zero), jnp.where(first, zero, q)], axis=0))
            acc_ref[hd] = jnp.zeros(acc_ref.shape[1:], F32)
        init = tuple(jnp.full((2 * tile, 1), -jnp.inf, F32) for _ in range(n_heads))
        maxes = k_tile(qs, qi, init, True)
        lax.fori_loop(0, qi, lambda s, c: k_tile(qs, qi - 1 - s, c, False), maxes)
        for hd in range(n_heads):
            p = acc_ref[hd, :, 0:hw] / acc_ref[hd, :, hw:2 * hw]
            o = p[0:tile] - lam * p[tile:2 * tile]
            o = _rms(o, sn_ref[...]) * (1.0 - lambda_init)
            o_ref[qrows, hd * hw:(hd + 1) * hw] = o.astype(o_ref.dtype)
        return 0

    lax.fori_loop(0, nq, q_body, 0)


def _diff_mixer(h, bsz, seqlen, cos_t, sin_t, g_norm, w_in, q_norm, k_norm, lq1, lk1, lq2, lk2,
                sub_norm, w_out, lambda_init):
    t, d = h.shape
    d_q = w_in.shape[1] // 3
    hdim = DIFF_HEAD_DIM
    tm = ROW_TILE
    tw = cos_t.shape[1]
    q, k, v = pl.pallas_call(
        functools.partial(_diff_in_kernel, q_scale=hdim ** -0.5 * _LOG2_E),
        grid=(t // tm,),
        in_specs=[pl.BlockSpec((tm, d), lambda i: (i, 0)),
                  _const_spec((1, d)),
                  _const_spec((d, 3 * d_q)),
                  pl.BlockSpec((tm, tw), lambda i: (i, 0)),
                  pl.BlockSpec((tm, tw), lambda i: (i, 0)),
                  _const_spec((1, d_q)), _const_spec((1, d_q))],
        out_specs=[pl.BlockSpec((tm, d_q), lambda i: (i, 0))] * 3,
        out_shape=[jax.ShapeDtypeStruct((t, d_q), BF16)] * 3,
        compiler_params=_cparams("arbitrary"),
        name="diff_in",
    )(h, g_norm.reshape(1, d), w_in.astype(BF16), cos_t, sin_t,
      jnp.tile(q_norm, d_q // hdim).reshape(1, d_q), jnp.tile(k_norm, d_q // hdim).reshape(1, d_q))
    hw = 2 * hdim
    bw = DIFF_HEADS_PER_STEP * hw
    tile = min(ATTN_TILE, seqlen)
    spec = pl.BlockSpec((None, seqlen, bw), lambda b, j: (b, 0, j))
    vec = _const_spec((1, hdim))
    o = pl.pallas_call(
        functools.partial(_diff_attn_kernel, tile=tile, lambda_init=lambda_init),
        grid=(bsz, d_q // bw),
        in_specs=[spec, spec, spec, vec, vec, vec, vec, _const_spec((1, hw))],
        out_specs=spec,
        out_shape=jax.ShapeDtypeStruct((bsz, seqlen, d_q), BF16),
        scratch_shapes=[pltpu.VMEM((DIFF_HEADS_PER_STEP, 2 * tile, 2 * hw), F32)],
        compiler_params=_cparams("arbitrary", "arbitrary"),
        name="diff_attn",
    )(q.reshape(bsz, seqlen, d_q), k.reshape(bsz, seqlen, d_q), v.reshape(bsz, seqlen, d_q),
      lq1.reshape(1, hdim), lk1.reshape(1, hdim), lq2.reshape(1, hdim), lk2.reshape(1, hdim),
      sub_norm.reshape(1, hw))
    return _proj_residual(o.reshape(t, d_q), w_out.astype(BF16), h)


_R_E1, _R_E2, _R_W1, _R_W2, _R_RANK1, _R_RANK2 = range(6)


def _router_kernel(h_ref, g_ref, w_ref, b_ref, rec_ref, cnt_ref, run_ref):
    tm = h_ref.shape[0]
    ne, ng, epg = MOE_EXPERTS, MOE_GROUPS, MOE_EXPERTS_PER_GROUP

    @pl.when(pl.program_id(0) == 0)
    def _():
        run_ref[...] = jnp.zeros(run_ref.shape, F32)

    u = _rms(h_ref[...], g_ref[...])
    logits = _dot(u, w_ref[...], precision=HIGHEST) + b_ref[...]
    lane = lax.broadcasted_iota(jnp.int32, logits.shape, 1)
    lane_f = lane.astype(F32)
    lane_group = (lane // epg).astype(F32)
    neg = -jnp.inf
    big = float(V7X_LANES)

    gl = jnp.where((lane >= ne) & (lane < ne + ng), logits, neg)
    gmax = jnp.max(gl, axis=1, keepdims=True)
    gsel = jnp.min(jnp.where(gl == gmax, lane_f - ne, big), axis=1, keepdims=True)
    g_gate = 1.0 / jnp.sum(jnp.exp(gl - gmax), axis=1, keepdims=True)

    el = jnp.where((lane < ne) & (lane_group == gsel), logits, neg)
    m1 = jnp.max(el, axis=1, keepdims=True)
    i1 = jnp.min(jnp.where(el == m1, lane_f, big), axis=1, keepdims=True)
    el2 = jnp.where(lane_f == i1, neg, el)
    m2 = jnp.max(el2, axis=1, keepdims=True)
    i2 = jnp.min(jnp.where(el2 == m2, lane_f, big), axis=1, keepdims=True)
    e21 = jnp.exp(m2 - m1)
    w1 = g_gate / (1.0 + e21)
    w2 = g_gate * e21 / (1.0 + e21)

    oh1 = lane_f == i1
    oh2 = lane_f == i2
    oh = jnp.where(oh1 | oh2, 1.0, 0.0)
    row = lax.broadcasted_iota(jnp.int32, (tm, tm), 0)
    col = lax.broadcasted_iota(jnp.int32, (tm, tm), 1)
    before = _dot((col < row).astype(BF16), oh.astype(BF16))
    pos = run_ref[0:1, :] + before
    rank1 = jnp.sum(jnp.where(oh1, pos, 0.0), axis=1, keepdims=True)
    rank2 = jnp.sum(jnp.where(oh2, pos, 0.0), axis=1, keepdims=True)
    run_ref[...] = run_ref[...] + jnp.sum(oh, axis=0, keepdims=True)
    cnt_ref[...] = run_ref[...]

    rec = jnp.zeros(logits.shape, F32)
    for idx, val in ((_R_E1, i1), (_R_E2, i2), (_R_W1, w1), (_R_W2, w2),
                     (_R_RANK1, rank1), (_R_RANK2, rank2)):
        rec = jnp.where(lane == idx, val, rec)
    rec_ref[...] = rec


def _dispatch_kernel(dest_ref, h_ref, g_ref, rows_in_ref, rows_ref, u_ref, sem):
    del rows_in_ref
    tm = h_ref.shape[0]
    u_ref[...] = _rms(h_ref[...], g_ref[...])

    def row_copy(tok, slot):
        return pltpu.make_async_copy(u_ref.at[tok], rows_ref.at[dest_ref[2 * tok + slot]], sem)

    def issue(tok, _):
        row_copy(tok, 0).start()
        row_copy(tok, 1).start()
        return 0

    def drain(tok, _):
        row_copy(tok, 0).wait()
        row_copy(tok, 1).wait()
        return 0

    lax.fori_loop(0, tm, issue, 0, unroll=DMA_LOOP_UNROLL)
    lax.fori_loop(0, tm, drain, 0, unroll=DMA_LOOP_UNROLL)


def _expert_kernel(be_ref, nu_ref, x_ref, wg_ref, wu_ref, wd_ref, y_ref, wgb_ref, wub_ref, wdb_ref):
    i = pl.program_id(0)

    @pl.when(i < nu_ref[0])
    def _():
        changed = jnp.logical_or(i == 0, be_ref[i] != be_ref[jnp.maximum(i - 1, 0)])

        @pl.when(changed)
        def _():
            wgb_ref[...] = wg_ref[...].astype(BF16)
            wub_ref[...] = wu_ref[...].astype(BF16)
            wdb_ref[...] = wd_ref[...].astype(BF16)

        x = x_ref[...].astype(BF16)
        gate = _dot(x, wgb_ref[...])
        up = _dot(x, wub_ref[...])
        hid = (gate * _sigmoid(gate) * up).astype(BF16)
        y_ref[...] = _dot(hid, wdb_ref[...])

    @pl.when(i >= nu_ref[0])
    def _():
        y_ref[...] = jnp.zeros(y_ref.shape, y_ref.dtype)


def _combine_kernel(dest_ref, h_ref, rec_ref, rows_ref, o_ref, y0_ref, y1_ref, sem):
    tm = h_ref.shape[0]

    def row_copy(tok, slot, buf):
        return pltpu.make_async_copy(rows_ref.at[dest_ref[2 * tok + slot]], buf.at[tok], sem)

    def issue(tok, _):
        row_copy(tok, 0, y0_ref).start()
        row_copy(tok, 1, y1_ref).start()
        return 0

    def drain(tok, _):
        row_copy(tok, 0, y0_ref).wait()
        row_copy(tok, 1, y1_ref).wait()
        return 0

    lax.fori_loop(0, tm, issue, 0, unroll=DMA_LOOP_UNROLL)
    lax.fori_loop(0, tm, drain, 0, unroll=DMA_LOOP_UNROLL)
    rec = rec_ref[...]
    w1 = rec[:, _R_W1:_R_W1 + 1]
    w2 = rec[:, _R_W2:_R_W2 + 1]
    o_ref[...] = h_ref[...] + (w1 * y0_ref[...] + w2 * y1_ref[...])


def _hier_moe(h, g_norm, w_group, b_group, w_expert, b_expert, layer, w_gate, w_up, w_down):
    t, d = h.shape
    ne, ng = MOE_EXPERTS, MOE_GROUPS
    ff = w_gate.shape[3]
    tm = min(MOE_TOKEN_TILE, t)
    g2 = g_norm.reshape(1, d)
    lane_pad = V7X_LANES - ne - ng
    w_router = jnp.concatenate([w_expert, w_group, jnp.zeros((d, lane_pad), F32)], axis=1)
    b_router = jnp.concatenate([b_expert, b_group, jnp.zeros((lane_pad,), F32)]).reshape(1, V7X_LANES)
    rec, cnt = pl.pallas_call(
        _router_kernel,
        grid=(t // tm,),
        in_specs=[pl.BlockSpec((tm, d), lambda i: (i, 0)),
                  _const_spec((1, d)),
                  _const_spec((d, V7X_LANES)),
                  _const_spec((1, V7X_LANES))],
        out_specs=[pl.BlockSpec((tm, V7X_LANES), lambda i: (i, 0)),
                   _const_spec((V7X_SUBLANES, V7X_LANES))],
        out_shape=[jax.ShapeDtypeStruct((t, V7X_LANES), F32),
                   jax.ShapeDtypeStruct((V7X_SUBLANES, V7X_LANES), F32)],
        scratch_shapes=[pltpu.VMEM((V7X_SUBLANES, V7X_LANES), F32)],
        compiler_params=_cparams("arbitrary"),
        name="moe_router",
    )(h, g2, w_router, b_router)

    br = MOE_ROW_BLOCK
    n_blocks = -(-(2 * t + ne * (br - 1)) // br)
    n_rows = n_blocks * br
    counts = cnt[0, :ne].astype(jnp.int32)
    padded = (counts + br - 1) // br * br
    pad_end = jnp.cumsum(padded)
    pad_start = pad_end - padded
    eid = rec[:, _R_E1:_R_E2 + 1].astype(jnp.int32)
    rank = rec[:, _R_RANK1:_R_RANK2 + 1].astype(jnp.int32)
    experts = jnp.arange(ne, dtype=jnp.int32)
    start_of = jnp.sum(jnp.where(eid[:, :, None] == experts, pad_start, 0), axis=-1)
    dest = (start_of + rank).reshape(-1)
    block_row0 = jnp.arange(n_blocks, dtype=jnp.int32) * br
    block_expert = jnp.minimum(
        jnp.sum((pad_end[None, :] <= block_row0[:, None]).astype(jnp.int32), axis=1), ne - 1)
    n_used = (pad_end[-1:] // br).astype(jnp.int32)

    x_rows = pl.pallas_call(
        _dispatch_kernel,
        grid_spec=pltpu.PrefetchScalarGridSpec(
            num_scalar_prefetch=0,
            grid=(t // tm,),
            in_specs=[pl.BlockSpec((2 * tm,), lambda i: (i,), memory_space=pltpu.SMEM),
                      pl.BlockSpec((tm, d), lambda i: (i, 0)),
                      _const_spec((1, d)),
                      pl.BlockSpec(memory_space=pl.ANY)],
            out_specs=pl.BlockSpec(memory_space=pl.ANY),
            scratch_shapes=[pltpu.VMEM((tm, d), F32), pltpu.SemaphoreType.DMA(())]),
        out_shape=jax.ShapeDtypeStruct((n_rows, d), F32),
        input_output_aliases={3: 0},
        compiler_params=_cparams("arbitrary"),
        name="moe_dispatch",
    )(dest, h, g2, jnp.zeros((n_rows, d), F32))

    def blk(i, be, nu):
        return jnp.minimum(i, nu[0] - 1)

    y_rows = pl.pallas_call(
        _expert_kernel,
        grid_spec=pltpu.PrefetchScalarGridSpec(
            num_scalar_prefetch=2,
            grid=(n_blocks,),
            in_specs=[pl.BlockSpec((br, d), lambda i, be, nu: (blk(i, be, nu), 0)),
                      pl.BlockSpec((None, None, d, ff),
                                   lambda i, be, nu: (layer, be[blk(i, be, nu)], 0, 0)),
                      pl.BlockSpec((None, None, d, ff),
                                   lambda i, be, nu: (layer, be[blk(i, be, nu)], 0, 0)),
                      pl.BlockSpec((None, None, ff, d),
                                   lambda i, be, nu: (layer, be[blk(i, be, nu)], 0, 0))],
            out_specs=pl.BlockSpec((br, d), lambda i, be, nu: (i, 0)),
            scratch_shapes=[pltpu.VMEM((d, ff), BF16), pltpu.VMEM((d, ff), BF16),
                            pltpu.VMEM((ff, d), BF16)]),
        out_shape=jax.ShapeDtypeStruct((n_rows, d), F32),
        compiler_params=_cparams("arbitrary"),
        name="moe_experts",
    )(block_expert, n_used, x_rows, w_gate, w_up, w_down)

    return pl.pallas_call(
        _combine_kernel,
        grid_spec=pltpu.PrefetchScalarGridSpec(
            num_scalar_prefetch=0,
            grid=(t // tm,),
            in_specs=[pl.BlockSpec((2 * tm,), lambda i: (i,), memory_space=pltpu.SMEM),
                      pl.BlockSpec((tm, d), lambda i: (i, 0)),
                      pl.BlockSpec((tm, V7X_LANES), lambda i: (i, 0)),
                      pl.BlockSpec(memory_space=pl.ANY)],
            out_specs=pl.BlockSpec((tm, d), lambda i: (i, 0)),
            scratch_shapes=[pltpu.VMEM((tm, d), F32), pltpu.VMEM((tm, d), F32),
                            pltpu.SemaphoreType.DMA(())]),
        out_shape=jax.ShapeDtypeStruct((t, d), F32),
        compiler_params=_cparams("arbitrary"),
        name="moe_combine",
    )(dest, h, rec, y_rows)


def _rope_tables(positions):
    half = DIFF_HEAD_DIM // 2
    inv = ROPE_THETA ** (-jnp.arange(0, DIFF_HEAD_DIM, 2, dtype=F32) / DIFF_HEAD_DIM)
    ang = positions.astype(F32).reshape(-1, 1) * inv
    cos, sin = jnp.cos(ang), jnp.sin(ang)
    cos_t = jnp.concatenate([cos] * 4, axis=1)
    sin_t = jnp.concatenate([-sin, sin, -sin, sin], axis=1)
    assert cos_t.shape[1] == 4 * half
    return cos_t, sin_t


def kernel(x, positions, norm_mix, norm_ffn, ssd_w_in, ssd_conv_w, ssd_conv_b, ssd_dt_bias, ssd_a_log, ssd_d, ssd_norm, ssd_w_out, sb_w_in, sb_w_out, lru_w_in, lru_conv_w, lru_conv_b, lru_w_rg, lru_b_rg, lru_w_ig, lru_b_ig, lru_lambda, lru_w_out, diff_w_in, diff_q_norm, diff_k_norm, diff_lam_q1, diff_lam_k1, diff_lam_q2, diff_lam_k2, diff_sub_norm, diff_w_out, moe_w_group, moe_b_group, moe_w_expert, moe_b_expert, moe_w_gate, moe_w_up, moe_w_down):
    bsz, seqlen, d = x.shape
    depth = norm_mix.shape[0]
    h = x.reshape(bsz * seqlen, d)
    cos_t, sin_t = _rope_tables(positions)
    for i in range(depth):
        m, j = i % N_MIXERS, i // N_MIXERS
        if m == 0:
            h = _ssd_mixer(h, bsz, seqlen, norm_mix[i], ssd_w_in[j], ssd_conv_w[j], ssd_conv_b[j],
                           ssd_dt_bias[j], ssd_a_log[j], ssd_d[j], ssd_norm[j], ssd_w_out[j])
        elif m == 1:
            h = _sb_mixer(h, bsz, seqlen, norm_mix[i], sb_w_in[j], sb_w_out[j])
        elif m == 2:
            h = _lru_mixer(h, bsz, seqlen, norm_mix[i], lru_w_in[j], lru_conv_w[j], lru_conv_b[j],
                           lru_w_rg[j], lru_b_rg[j], lru_w_ig[j], lru_b_ig[j], lru_lambda[j],
                           lru_w_out[j])
        else:
            lambda_init = 0.8 - 0.6 * math.exp(-0.3 * i)
            h = _diff_mixer(h, bsz, seqlen, cos_t, sin_t, norm_mix[i], diff_w_in[j], diff_q_norm[j],
                            diff_k_norm[j], diff_lam_q1[j], diff_lam_k1[j], diff_lam_q2[j],
                            diff_lam_k2[j], diff_sub_norm[j], diff_w_out[j], lambda_init)
        h = _hier_moe(h, norm_ffn[i], moe_w_group[i], moe_b_group[i], moe_w_expert[i],
                      moe_b_expert[i], i, moe_w_gate, moe_w_up, moe_w_down)
    return h.reshape(bsz, seqlen, d)
```

```python
import functools
import math

import jax
import jax.numpy as jnp
from jax import lax
from jax.experimental import pallas as pl
from jax.experimental.pallas import tpu as pltpu

F32 = jnp.float32
BF16 = jnp.bfloat16
HIGHEST = lax.Precision.HIGHEST

RMS_EPS = 1e-6
_LOG2_E = math.log2(math.e)
N_MIXERS = 4

V7X_VMEM_BYTES = 64 * 1024 * 1024
V7X_LANES = 128
V7X_SUBLANES = 8
VMEM_LIMIT = (V7X_VMEM_BYTES * 7) // 8

SSD_HEAD_DIM = 64
SSD_GROUPS = 8
SSD_STATE = 128
SSD_CONV = 4
SSD_CHUNK = 128
SB_HEAD_DIM = 64
LRU_BLOCKS = 16
LRU_CONV = 4
LRU_C = 8.0
DIFF_HEAD_DIM = 64
ROPE_THETA = 10000.0
MOE_GROUPS = 4
MOE_EXPERTS_PER_GROUP = 8
MOE_EXPERTS = MOE_GROUPS * MOE_EXPERTS_PER_GROUP

ROW_TILE = 512
SSD_IN_TILE = 256
SSD_CONV_CHUNK = 512
ATTN_TILE = 256
SB_PAIRS_PER_STEP = 4
DIFF_HEADS_PER_STEP = 4
LRU_TIME_TILE = 64
LRU_PITCH_PAD = 4
LRU_BAND = 3 * V7X_LANES
LRU_SCAN_UNROLL = 4
MOE_TOKEN_TILE = 512
MOE_ROW_BLOCK = 256
CONV_HALO = V7X_SUBLANES
DMA_LOOP_UNROLL = 8


def _cparams(*sem):
    return pltpu.CompilerParams(dimension_semantics=sem, vmem_limit_bytes=VMEM_LIMIT)


def _const_spec(shape):
    nd = len(shape)
    return pl.BlockSpec(shape, lambda *_: (0,) * nd)


def _rms(x, w):
    ms = jnp.mean(x * x, axis=-1, keepdims=True)
    return x * lax.rsqrt(ms + RMS_EPS) * w


def _sigmoid(x):
    return 1.0 / (1.0 + jnp.exp(-x))


def _softplus(x):
    return jnp.maximum(x, 0.0) + jnp.log(1.0 + jnp.exp(-jnp.abs(x)))


def _dot(a, b, **kw):
    return jnp.dot(a, b, preferred_element_type=F32, **kw)


def _dot_nt(a, b):
    return lax.dot_general(a, b, (((1,), (1,)), ((), ())), preferred_element_type=F32)


def _dot_tn(a, b):
    return lax.dot_general(a, b, (((0,), (0,)), ((), ())), preferred_element_type=F32)


def _split_bf16(x):
    hi = x.astype(BF16)
    return hi, (x - hi.astype(F32)).astype(BF16)


def _split_weight(w):
    return jnp.concatenate(_split_bf16(w), axis=1)


def _dot_split(x, x_hi, w2_ref):
    n = w2_ref.shape[1] // 2
    x_lo = (x - x_hi.astype(F32)).astype(BF16)
    t = _dot(x_hi, w2_ref[...])
    return t[:, :n] + t[:, n:] + _dot(x_lo, w2_ref[:, :n])


def _proj_residual_kernel(y_ref, w_ref, h_ref, o_ref):
    o_ref[...] = h_ref[...] + _dot(y_ref[...], w_ref[...])


def _proj_residual(y, w, h):
    t, k = y.shape
    d = w.shape[1]
    tm = ROW_TILE
    return pl.pallas_call(
        _proj_residual_kernel,
        grid=(t // tm,),
        in_specs=[pl.BlockSpec((tm, k), lambda i: (i, 0)),
                  _const_spec((k, d)),
                  pl.BlockSpec((tm, d), lambda i: (i, 0))],
        out_specs=pl.BlockSpec((tm, d), lambda i: (i, 0)),
        out_shape=jax.ShapeDtypeStruct((t, d), F32),
        compiler_params=_cparams("arbitrary"),
        name="proj_residual",
    )(y, w, h)


def _ssd_in_kernel(h_ref, g_ref, wz_ref, wx_ref, wdt_ref, cw_ref, cb_ref,
                   z_ref, xbc_ref, dt_ref, pad_ref, *, tiles_per_seq):
    i = pl.program_id(0)
    tm = h_ref.shape[0]
    u32 = _rms(h_ref[...], g_ref[...])
    u = u32.astype(BF16)
    z_ref[...] = _dot(u, wz_ref[...]).astype(z_ref.dtype)
    dt_ref[...] = _dot_split(u32, u, wdt_ref)

    @pl.when(i % tiles_per_seq == 0)
    def _():
        pad_ref[0:CONV_HALO, :] = jnp.zeros((CONV_HALO, pad_ref.shape[1]), F32)

    conv_dim = pad_ref.shape[1]
    for c0 in range(0, conv_dim, SSD_CONV_CHUNK):
        cs = slice(c0, c0 + SSD_CONV_CHUNK)
        pad_ref[CONV_HALO:CONV_HALO + tm, cs] = _dot(u, wx_ref[:, cs])
        acc = cb_ref[:, cs] + cw_ref[SSD_CONV - 1:SSD_CONV, cs] * pad_ref[CONV_HALO:CONV_HALO + tm, cs]
        for k in range(SSD_CONV - 1):
            off = CONV_HALO - (SSD_CONV - 1) + k
            acc = acc + cw_ref[k:k + 1, cs] * pad_ref[off:off + tm, cs]
        xbc_ref[:, cs] = (acc * _sigmoid(acc)).astype(xbc_ref.dtype)
        pad_ref[0:CONV_HALO, cs] = pad_ref[tm:tm + CONV_HALO, cs]


def _ssd_scan_kernel(xs_ref, b_ref, c_ref, z_ref, dt_ref, dtt_ref,
                     bias_ref, biast_ref, alog_ref, alogt_ref, dfull_ref, nw_ref,
                     o_ref, state_ref):
    q = xs_ref.shape[0]
    n_heads = dt_ref.shape[1]
    d_inner = xs_ref.shape[1]
    heads_per_group = n_heads // SSD_GROUPS
    gw = d_inner // SSD_GROUPS
    p = SSD_HEAD_DIM

    @pl.when(pl.program_id(1) == 0)
    def _():
        state_ref[...] = jnp.zeros(state_ref.shape, F32)

    dt = _softplus(dt_ref[...] + bias_ref[...])
    a = -jnp.exp(alog_ref[...]) * dt
    dtt = _softplus(dtt_ref[...] + biast_ref[...])
    at = -jnp.exp(alogt_ref[...]) * dtt

    row = lax.broadcasted_iota(jnp.int32, (q, q), 0)
    col = lax.broadcasted_iota(jnp.int32, (q, q), 1)
    lower = row >= col
    a_cum = _dot(lower.astype(F32), a, precision=HIGHEST)
    a_cumt = _dot(at, (row <= col).astype(F32), precision=HIGHEST)
    a_tot = a_cum[q - 1:q, :]

    hh = lax.broadcasted_iota(jnp.int32, (2 * n_heads, d_inner), 0)
    cc = lax.broadcasted_iota(jnp.int32, (2 * n_heads, d_inner), 1)
    expand = (cc // p == hh % n_heads).astype(BF16)
    per_head = jnp.concatenate(
        [dt, jnp.exp(a_cum), jnp.exp(a_tot - a_cum),
         jnp.broadcast_to(jnp.exp(a_tot), (V7X_SUBLANES, n_heads))], axis=0)
    per_channel = _dot(jnp.concatenate(_split_bf16(per_head), axis=1), expand)
    dt_full = per_channel[0:q]
    dec_out_full = per_channel[q:2 * q]
    dec_st_full = per_channel[2 * q:3 * q]
    chunk_dec_full = per_channel[3 * q:3 * q + 1]

    lane = lax.broadcasted_iota(jnp.int32, (1, gw), 1)
    for g in range(SSD_GROUPS):
        sl = slice(g * gw, (g + 1) * gw)
        bg = b_ref[:, g * SSD_STATE:(g + 1) * SSD_STATE]
        cg = c_ref[:, g * SSD_STATE:(g + 1) * SSD_STATE]
        xs_g = xs_ref[:, sl].astype(F32)
        xdt = xs_g * dt_full[:, sl]
        cb = _dot_nt(cg, bg)
        s_prev = state_ref[g]
        y = _dot(cg, s_prev.astype(BF16)) * dec_out_full[:, sl]
        for r in range(heads_per_group):
            hd = g * heads_per_group + r
            seg = a_cum[:, hd:hd + 1] - a_cumt[hd:hd + 1, :]
            m = jnp.where(lower, cb * jnp.exp(seg), 0.0).astype(BF16)
            xr = jnp.where((lane >= r * p) & (lane < (r + 1) * p), xdt, 0.0).astype(BF16)
            y = y + _dot(m, xr)
        xw = (xdt * dec_st_full[:, sl]).astype(BF16)
        state_ref[g] = s_prev * chunk_dec_full[:, sl] + _dot_tn(bg, xw)
        y = y + xs_g * dfull_ref[:, sl]
        zg = z_ref[:, sl].astype(F32)
        yz = y * (zg * _sigmoid(zg))
        o_ref[:, sl] = _rms(yz, nw_ref[:, sl]).astype(o_ref.dtype)


def _ssd_mixer(h, bsz, seqlen, g_norm, w_in, conv_w, conv_b, dt_bias, a_log, d_skip, norm_w, w_out):
    t, d = h.shape
    n_heads = dt_bias.shape[0]
    d_inner = n_heads * SSD_HEAD_DIM
    gn = SSD_GROUPS * SSD_STATE
    conv_dim = d_inner + 2 * gn
    wz = w_in[:, :d_inner].astype(BF16)
    wx = w_in[:, d_inner:d_inner + conv_dim].astype(BF16)
    wdt = _split_weight(w_in[:, d_inner + conv_dim:])
    tm = SSD_IN_TILE
    z, xbc, dt_raw = pl.pallas_call(
        functools.partial(_ssd_in_kernel, tiles_per_seq=seqlen // tm),
        grid=(t // tm,),
        in_specs=[pl.BlockSpec((tm, d), lambda i: (i, 0)),
                  _const_spec((1, d)),
                  _const_spec((d, d_inner)),
                  _const_spec((d, conv_dim)),
                  _const_spec((d, 2 * n_heads)),
                  _const_spec((SSD_CONV, conv_dim)),
                  _const_spec((1, conv_dim))],
        out_specs=[pl.BlockSpec((tm, d_inner), lambda i: (i, 0)),
                   pl.BlockSpec((tm, conv_dim), lambda i: (i, 0)),
                   pl.BlockSpec((tm, n_heads), lambda i: (i, 0))],
        out_shape=[jax.ShapeDtypeStruct((t, d_inner), BF16),
                   jax.ShapeDtypeStruct((t, conv_dim), BF16),
                   jax.ShapeDtypeStruct((t, n_heads), F32)],
        scratch_shapes=[pltpu.VMEM((tm + CONV_HALO, conv_dim), F32)],
        compiler_params=_cparams("arbitrary"),
        name="ssd_in",
    )(h, g_norm.reshape(1, d), wz, wx, wdt, conv_w, conv_b.reshape(1, conv_dim))

    q = SSD_CHUNK
    nc = seqlen // q
    xbc3 = xbc.reshape(bsz, seqlen, conv_dim)
    dt3 = dt_raw.reshape(bsz, seqlen, n_heads)
    dtt3 = jnp.swapaxes(dt3, 1, 2)
    nb_x = d_inner // gn
    yg = pl.pallas_call(
        _ssd_scan_kernel,
        grid=(bsz, nc),
        in_specs=[pl.BlockSpec((None, q, d_inner), lambda b, c: (b, c, 0)),
                  pl.BlockSpec((None, q, gn), lambda b, c: (b, c, nb_x)),
                  pl.BlockSpec((None, q, gn), lambda b, c: (b, c, nb_x + 1)),
                  pl.BlockSpec((None, q, d_inner), lambda b, c: (b, c, 0)),
                  pl.BlockSpec((None, q, n_heads), lambda b, c: (b, c, 0)),
                  pl.BlockSpec((None, n_heads, q), lambda b, c: (b, 0, c)),
                  _const_spec((1, n_heads)), _const_spec((n_heads, 1)),
                  _const_spec((1, n_heads)), _const_spec((n_heads, 1)),
                  _const_spec((1, d_inner)), _const_spec((1, d_inner))],
        out_specs=pl.BlockSpec((None, q, d_inner), lambda b, c: (b, c, 0)),
        out_shape=jax.ShapeDtypeStruct((bsz, seqlen, d_inner), BF16),
        scratch_shapes=[pltpu.VMEM((SSD_GROUPS, SSD_STATE, d_inner // SSD_GROUPS), F32)],
        compiler_params=_cparams("arbitrary", "arbitrary"),
        name="ssd_scan",
    )(xbc3, xbc3, xbc3, z.reshape(bsz, seqlen, d_inner), dt3, dtt3,
      dt_bias.reshape(1, n_heads), dt_bias.reshape(n_heads, 1),
      a_log.reshape(1, n_heads), a_log.reshape(n_heads, 1),
      jnp.repeat(d_skip, SSD_HEAD_DIM).reshape(1, d_inner), norm_w.reshape(1, d_inner))
    return _proj_residual(yg.reshape(t, d_inner), w_out.astype(BF16), h)


def _sb_in_kernel(h_ref, g_ref, w_ref, o_ref, *, q_cols, q_scale):
    u = _rms(h_ref[...], g_ref[...]).astype(BF16)
    o_ref[:, :q_cols] = (_dot(u, w_ref[:, :q_cols]) * q_scale).astype(o_ref.dtype)
    o_ref[:, q_cols:] = _dot(u, w_ref[:, q_cols:]).astype(o_ref.dtype)


def _sb_attn_kernel(q_ref, k_ref, v_ref, o_ref, acc_ref, *, tile):
    seqlen = q_ref.shape[0]
    nq = seqlen // tile
    pw = 2 * SB_HEAD_DIM
    n_pairs = q_ref.shape[1] // pw
    lane = lax.broadcasted_iota(jnp.int32, (1, pw), 1)
    first = lane < SB_HEAD_DIM
    row = lax.broadcasted_iota(jnp.int32, (tile, tile), 0)
    col = lax.broadcasted_iota(jnp.int32, (tile, tile), 1)
    neg_from = jnp.where(row >= col, -1.0, 0.0).astype(BF16)
    past = col < row
    past2 = jnp.concatenate([past, past], axis=0)

    def k_tile(qs, kj, csums, diagonal):
        rows = pl.ds(pl.multiple_of(kj * tile, tile), tile)
        out = []
        for p in range(n_pairs):
            k = k_ref[rows, p * pw:(p + 1) * pw]
            v = v_ref[rows, p * pw:(p + 1) * pw]
            t = _dot_nt(qs[p], k)
            neg_abs = pltpu.bitcast(pltpu.bitcast(t, jnp.uint32) | jnp.uint32(0x80000000), F32)
            sp = jnp.maximum(t, 0.0) + jnp.log(1.0 + jnp.exp2(neg_abs)) * _LOG2_E
            if diagonal:
                sp = jnp.where(past2, sp, 0.0)
            w = jnp.exp2(t + _dot(sp.astype(BF16), neg_from) + csums[p])
            if diagonal:
                w = jnp.where(past2, w, 0.0)
            acc_ref[p] += _dot(w.astype(BF16), v)
            out.append(csums[p] - jnp.sum(sp, axis=1, keepdims=True))
        return tuple(out)

    def q_body(qi, _):
        qrows = pl.ds(pl.multiple_of(qi * tile, tile), tile)
        qs = []
        for p in range(n_pairs):
            q = q_ref[qrows, p * pw:(p + 1) * pw]
            zero = jnp.zeros_like(q)
            qs.append(jnp.concatenate([jnp.where(first, q, zero), jnp.where(first, zero, q)], axis=0))
            acc_ref[p] = jnp.zeros(acc_ref.shape[1:], F32)
        init = tuple(jnp.zeros((2 * tile, 1), F32) for _ in range(n_pairs))
        csums = k_tile(qs, qi, init, True)
        lax.fori_loop(0, qi, lambda s, c: k_tile(qs, qi - 1 - s, c, False), csums)
        for p in range(n_pairs):
            o = jnp.where(first, acc_ref[p, 0:tile, :], acc_ref[p, tile:2 * tile, :])
            o_ref[qrows, p * pw:(p + 1) * pw] = o.astype(o_ref.dtype)
        return 0

    lax.fori_loop(0, nq, q_body, 0)


def _sb_mixer(h, bsz, seqlen, g_norm, w_in, w_out):
    t, d = h.shape
    hd = w_in.shape[1] // 3
    tm = ROW_TILE
    qkv = pl.pallas_call(
        functools.partial(_sb_in_kernel, q_cols=hd, q_scale=SB_HEAD_DIM ** -0.5 * _LOG2_E),
        grid=(t // tm,),
        in_specs=[pl.BlockSpec((tm, d), lambda i: (i, 0)),
                  _const_spec((1, d)),
                  _const_spec((d, 3 * hd))],
        out_specs=pl.BlockSpec((tm, 3 * hd), lambda i: (i, 0)),
        out_shape=jax.ShapeDtypeStruct((t, 3 * hd), BF16),
        compiler_params=_cparams("arbitrary"),
        name="sb_in",
    )(h, g_norm.reshape(1, d), w_in.astype(BF16))
    qkv3 = qkv.reshape(bsz, seqlen, 3 * hd)
    bw = SB_PAIRS_PER_STEP * 2 * SB_HEAD_DIM
    nb = hd // bw
    tile = min(ATTN_TILE, seqlen)
    o = pl.pallas_call(
        functools.partial(_sb_attn_kernel, tile=tile),
        grid=(bsz, nb),
        in_specs=[pl.BlockSpec((None, seqlen, bw), lambda b, j: (b, 0, j)),
                  pl.BlockSpec((None, seqlen, bw), lambda b, j: (b, 0, nb + j)),
                  pl.BlockSpec((None, seqlen, bw), lambda b, j: (b, 0, 2 * nb + j))],
        out_specs=pl.BlockSpec((None, seqlen, bw), lambda b, j: (b, 0, j)),
        out_shape=jax.ShapeDtypeStruct((bsz, seqlen, hd), BF16),
        scratch_shapes=[pltpu.VMEM((SB_PAIRS_PER_STEP, 2 * tile, 2 * SB_HEAD_DIM), F32)],
        compiler_params=_cparams("arbitrary", "arbitrary"),
        name="sb_attn",
    )(qkv3, qkv3, qkv3)
    return _proj_residual(o.reshape(t, hd), w_out.astype(BF16), h)


def _lru_kernel(h_ref, g_ref, win_ref, cw_ref, cb_ref, wband_ref, br_ref, bi_ref, lam_ref,
                y_ref, pad_ref, xc_ref, a_ref, b_ref, hs_ref, carry_ref, *, band_starts):
    bsz, tt, d = h_ref.shape
    width = y_ref.shape[2]

    @pl.when(pl.program_id(0) == 0)
    def _():
        carry_ref[...] = jnp.zeros(carry_ref.shape, F32)
        for b in range(bsz):
            pad_ref[b, 0:CONV_HALO, :] = jnp.zeros((CONV_HALO, width), F32)

    u = _rms(h_ref[...].reshape(bsz * tt, d), g_ref[...]).astype(BF16)
    proj = _dot(u, win_ref[...])
    gate_branch = proj[:, :width]
    for b in range(bsz):
        pad_ref[b, CONV_HALO:CONV_HALO + tt, :] = proj[b * tt:(b + 1) * tt, width:]
        acc = cb_ref[...] + cw_ref[LRU_CONV - 1:LRU_CONV, :] * pad_ref[b, CONV_HALO:CONV_HALO + tt, :]
        for k in range(LRU_CONV - 1):
            off = CONV_HALO - (LRU_CONV - 1) + k
            acc = acc + cw_ref[k:k + 1, :] * pad_ref[b, off:off + tt, :]
        xc_ref[b * tt:(b + 1) * tt, :] = acc
        pad_ref[b, 0:CONV_HALO, :] = pad_ref[b, tt:tt + CONV_HALO, :]

    xcb = xc_ref[...].astype(BF16)
    n_lt = width // V7X_LANES
    pitch = a_ref.shape[1] // bsz
    neg_c_softplus = -LRU_C * _softplus(-lam_ref[...])
    for c in range(n_lt):
        cs = slice(c * V7X_LANES, (c + 1) * V7X_LANES)
        pre = _dot(xcb[:, band_starts[c]:band_starts[c] + LRU_BAND], wband_ref[c])
        r = _sigmoid(pre[:, :V7X_LANES] + br_ref[:, cs])
        ig = _sigmoid(pre[:, V7X_LANES:] + bi_ref[:, cs])
        a = jnp.exp(neg_c_softplus[:, cs] * r)
        bterm = jnp.sqrt(1.0 - a * a) * (ig * xc_ref[:, cs])
        for b in range(bsz):
            a_ref[c, b * pitch:b * pitch + tt, :] = a[b * tt:(b + 1) * tt, :]
            b_ref[c, b * pitch:b * pitch + tt, :] = bterm[b * tt:(b + 1) * tt, :]

    def step(j, hprev):
        hnew = []
        for c in range(n_lt):
            rows_j = pl.ds(j, bsz, stride=pitch)
            hc = a_ref[c, rows_j, :] * hprev[c] + b_ref[c, rows_j, :]
            hs_ref[c, rows_j, :] = hc
            hnew.append(hc)
        return tuple(hnew)

    hlast = lax.fori_loop(0, tt, step, tuple(carry_ref[c] for c in range(n_lt)), unroll=LRU_SCAN_UNROLL)
    for c in range(n_lt):
        carry_ref[c] = hlast[c]
    c0 = math.sqrt(2.0 / math.pi)
    gelu = 0.5 * gate_branch * (1.0 + jnp.tanh(c0 * (gate_branch + 0.044715 * gate_branch ** 3)))
    for b in range(bsz):
        hs_b = jnp.concatenate([hs_ref[c, b * pitch:b * pitch + tt, :] for c in range(n_lt)], axis=1)
        y_ref[b] = (hs_b * gelu[b * tt:(b + 1) * tt, :]).astype(y_ref.dtype)


def _block_diag(w):
    nb, bd, _ = w.shape
    eye = jnp.eye(nb, dtype=w.dtype)
    return (eye[:, None, :, None] * w[:, :, None, :]).reshape(nb * bd, nb * bd)


def _lru_band_starts(width):
    bd = width // LRU_BLOCKS
    starts = []
    for j in range(width // V7X_LANES):
        b_lo = (j * V7X_LANES) // bd
        b_hi = ((j + 1) * V7X_LANES - 1) // bd
        lo = (b_lo * bd) // V7X_LANES * V7X_LANES
        assert (b_hi + 1) * bd - lo <= LRU_BAND
        starts.append(min(lo, width - LRU_BAND))
    return tuple(starts)


def _lru_band_weights(w_rg, w_ig, starts):
    dense_r, dense_i = _block_diag(w_rg), _block_diag(w_ig)
    tiles = []
    for j, s in enumerate(starts):
        cs = slice(j * V7X_LANES, (j + 1) * V7X_LANES)
        tiles.append(jnp.concatenate([dense_r[s:s + LRU_BAND, cs], dense_i[s:s + LRU_BAND, cs]], axis=1))
    return jnp.stack(tiles).astype(BF16)


def _lru_mixer(h, bsz, seqlen, g_norm, w_in, conv_w, conv_b, w_rg, b_rg, w_ig, b_ig, lam, w_out):
    t, d = h.shape
    width = lam.shape[0]
    tt = min(LRU_TIME_TILE, seqlen)
    rows = bsz * tt
    n_lt = width // V7X_LANES
    scan_rows = bsz * (tt + LRU_PITCH_PAD)
    starts = _lru_band_starts(width)
    y = pl.pallas_call(
        functools.partial(_lru_kernel, band_starts=starts),
        grid=(seqlen // tt,),
        in_specs=[pl.BlockSpec((bsz, tt, d), lambda i: (0, i, 0)),
                  _const_spec((1, d)),
                  _const_spec((d, 2 * width)),
                  _const_spec((LRU_CONV, width)), _const_spec((1, width)),
                  _const_spec((n_lt, LRU_BAND, 2 * V7X_LANES)),
                  _const_spec((1, width)), _const_spec((1, width)),
                  _const_spec((1, width))],
        out_specs=pl.BlockSpec((bsz, tt, width), lambda i: (0, i, 0)),
        out_shape=jax.ShapeDtypeStruct((bsz, seqlen, width), BF16),
        scratch_shapes=[pltpu.VMEM((bsz, tt + CONV_HALO, width), F32),
                        pltpu.VMEM((rows, width), F32),
                        pltpu.VMEM((n_lt, scan_rows, V7X_LANES), F32),
                        pltpu.VMEM((n_lt, scan_rows, V7X_LANES), F32),
                        pltpu.VMEM((n_lt, scan_rows, V7X_LANES), F32),
                        pltpu.VMEM((n_lt, bsz, V7X_LANES), F32)],
        compiler_params=_cparams("arbitrary"),
        name="lru",
    )(h.reshape(bsz, seqlen, d), g_norm.reshape(1, d), w_in.astype(BF16),
      conv_w, conv_b.reshape(1, width), _lru_band_weights(w_rg, w_ig, starts),
      b_rg.reshape(1, width), b_ig.reshape(1, width), lam.reshape(1, width))
    return _proj_residual(y.reshape(t, width), w_out.astype(BF16), h)


def _diff_in_kernel(h_ref, g_ref, w_ref, cos_ref, sin_ref, qn_ref, kn_ref, q_ref, k_ref, v_ref,
                    *, q_scale):
    d_q = q_ref.shape[1]
    hdim = DIFF_HEAD_DIM
    n_grp = d_q // hdim
    u = _rms(h_ref[...], g_ref[...]).astype(BF16)
    v_ref[...] = _dot(u, w_ref[:, 2 * d_q:]).astype(v_ref.dtype)

    ci = lax.broadcasted_iota(jnp.int32, (d_q, n_grp), 0)
    gi = lax.broadcasted_iota(jnp.int32, (d_q, n_grp), 1)
    gsum = (ci // hdim == gi).astype(BF16)
    gj = lax.broadcasted_iota(jnp.int32, (2 * n_grp, d_q), 0)
    cj = lax.broadcasted_iota(jnp.int32, (2 * n_grp, d_q), 1)
    gexp = (cj // hdim == gj % n_grp).astype(BF16)
    lane = lax.broadcasted_iota(jnp.int32, (1, d_q), 1)
    first_half = (lane % hdim) < (hdim // 2)
    reps = d_q // cos_ref.shape[1]
    cos_t = jnp.tile(cos_ref[...], (1, reps))
    sin_t = jnp.tile(sin_ref[...], (1, reps))

    def norm_rope(x, nw):
        sq = x * x
        sq_hi = sq.astype(BF16)
        sq_lo = (sq - sq_hi.astype(F32)).astype(BF16)
        ms = (_dot(sq_hi, gsum) + _dot(sq_lo, gsum)) * (1.0 / hdim)
        inv = lax.rsqrt(ms + RMS_EPS)
        inv_hi = inv.astype(BF16)
        inv_lo = (inv - inv_hi.astype(F32)).astype(BF16)
        inv_full = _dot(jnp.concatenate([inv_hi, inv_lo], axis=1), gexp)
        xn = x * inv_full * nw
        rot = jnp.where(first_half, pltpu.roll(xn, d_q - hdim // 2, 1), pltpu.roll(xn, hdim // 2, 1))
        return xn * cos_t + rot * sin_t

    q_ref[...] = (norm_rope(_dot(u, w_ref[:, :d_q]), qn_ref[...]) * q_scale).astype(q_ref.dtype)
    k_ref[...] = norm_rope(_dot(u, w_ref[:, d_q:2 * d_q]), kn_ref[...]).astype(k_ref.dtype)


def _diff_attn_kernel(q_ref, k_ref, v_ref, lq1_ref, lk1_ref, lq2_ref, lk2_ref, sn_ref, o_ref,
                      acc_ref, *, tile, lambda_init):
    seqlen = q_ref.shape[0]
    nq = seqlen // tile
    hdim = DIFF_HEAD_DIM
    hw = 2 * hdim
    n_heads = q_ref.shape[1] // hw
    lane = lax.broadcasted_iota(jnp.int32, (1, hw), 1)
    first = lane < hdim
    row = lax.broadcasted_iota(jnp.int32, (tile, tile), 0)
    col = lax.broadcasted_iota(jnp.int32, (tile, tile), 1)
    causal = col <= row
    causal2 = jnp.concatenate([causal, causal], axis=0)
    ones = jnp.ones((tile, hw), BF16)
    lam = (jnp.exp(jnp.sum(lq1_ref[...] * lk1_ref[...], axis=1, keepdims=True))
           - jnp.exp(jnp.sum(lq2_ref[...] * lk2_ref[...], axis=1, keepdims=True)) + lambda_init)

    def k_tile(qs, kj, maxes, diagonal):
        rows = pl.ds(pl.multiple_of(kj * tile, tile), tile)
        out = []
        for hd in range(n_heads):
            k = k_ref[rows, hd * hw:(hd + 1) * hw]
            v1 = jnp.concatenate([v_ref[rows, hd * hw:(hd + 1) * hw], ones], axis=1)
            s = _dot_nt(qs[hd], k)
            if diagonal:
                s = jnp.where(causal2, s, -jnp.inf)
            mx_new = jnp.maximum(maxes[hd], jnp.max(s, axis=1, keepdims=True))
            alpha = jnp.exp2(maxes[hd] - mx_new)
            pr = jnp.exp2(s - mx_new)
            acc_ref[hd] = acc_ref[hd] * alpha + _dot(pr.astype(BF16), v1)
            out.append(mx_new)
        return tuple(out)

    def q_body(qi, _):
        qrows = pl.ds(pl.multiple_of(qi * tile, tile), tile)
        qs = []
        for hd in range(n_heads):
            q = q_ref[qrows, hd * hw:(hd + 1) * hw]
            zero = jnp.zeros_like(q)
            qs.append(jnp.concatenate([jnp.where(first, q, zero), jnp.where(first, zero, q)], axis=0))
            acc_ref[hd] = jnp.zeros(acc_ref.shape[1:], F32)
        init = tuple(jnp.full((2 * tile, 1), -jnp.inf, F32) for _ in range(n_heads))
        maxes = k_tile(qs, qi, init, True)
        lax.fori_loop(0, qi, lambda s, c: k_tile(qs, qi - 1 - s, c, False), maxes)
        for hd in range(n_heads):
            p = acc_ref[hd, :, 0:hw] / acc_ref[hd, :, hw:2 * hw]
            o = p[0:tile] - lam * p[tile:2 * tile]
            o = _rms(o, sn_ref[...]) * (1.0 - lambda_init)
            o_ref[qrows, hd * hw:(hd + 1) * hw] = o.astype(o_ref.dtype)
        return 0

    lax.fori_loop(0, nq, q_body, 0)


def _diff_mixer(h, bsz, seqlen, cos_t, sin_t, g_norm, w_in, q_norm, k_norm, lq1, lk1, lq2, lk2,
                sub_norm, w_out, lambda_init):
    t, d = h.shape
    d_q = w_in.shape[1] // 3
    hdim = DIFF_HEAD_DIM
    tm = ROW_TILE
    tw = cos_t.shape[1]
    q, k, v = pl.pallas_call(
        functools.partial(_diff_in_kernel, q_scale=hdim ** -0.5 * _LOG2_E),
        grid=(t // tm,),
        in_specs=[pl.BlockSpec((tm, d), lambda i: (i, 0)),
                  _const_spec((1, d)),
                  _const_spec((d, 3 * d_q)),
                  pl.BlockSpec((tm, tw), lambda i: (i, 0)),
                  pl.BlockSpec((tm, tw), lambda i: (i, 0)),
                  _const_spec((1, d_q)), _const_spec((1, d_q))],
        out_specs=[pl.BlockSpec((tm, d_q), lambda i: (i, 0))] * 3,
        out_shape=[jax.ShapeDtypeStruct((t, d_q), BF16)] * 3,
        compiler_params=_cparams("arbitrary"),
        name="diff_in",
    )(h, g_norm.reshape(1, d), w_in.astype(BF16), cos_t, sin_t,
      jnp.tile(q_norm, d_q // hdim).reshape(1, d_q), jnp.tile(k_norm, d_q // hdim).reshape(1, d_q))
    hw = 2 * hdim
    bw = DIFF_HEADS_PER_STEP * hw
    tile = min(ATTN_TILE, seqlen)
    spec = pl.BlockSpec((None, seqlen, bw), lambda b, j: (b, 0, j))
    vec = _const_spec((1, hdim))
    o = pl.pallas_call(
        functools.partial(_diff_attn_kernel, tile=tile, lambda_init=lambda_init),
        grid=(bsz, d_q // bw),
        in_specs=[spec, spec, spec, vec, vec, vec, vec, _const_spec((1, hw))],
        out_specs=spec,
        out_shape=jax.ShapeDtypeStruct((bsz, seqlen, d_q), BF16),
        scratch_shapes=[pltpu.VMEM((DIFF_HEADS_PER_STEP, 2 * tile, 2 * hw), F32)],
        compiler_params=_cparams("arbitrary", "arbitrary"),
        name="diff_attn",
    )(q.reshape(bsz, seqlen, d_q), k.reshape(bsz, seqlen, d_q), v.reshape(bsz, seqlen, d_q),
      lq1.reshape(1, hdim), lk1.reshape(1, hdim), lq2.reshape(1, hdim), lk2.reshape(1, hdim),
      sub_norm.reshape(1, hw))
    return _proj_residual(o.reshape(t, d_q), w_out.astype(BF16), h)


_R_E1, _R_E2, _R_W1, _R_W2, _R_RANK1, _R_RANK2 = range(6)


def _router_kernel(h_ref, g_ref, w_ref, b_ref, rec_ref, cnt_ref, run_ref):
    tm = h_ref.shape[0]
    ne, ng, epg = MOE_EXPERTS, MOE_GROUPS, MOE_EXPERTS_PER_GROUP

    @pl.when(pl.program_id(0) == 0)
    def _():
        run_ref[...] = jnp.zeros(run_ref.shape, F32)

    u = _rms(h_ref[...], g_ref[...])
    logits = _dot_split(u, u.astype(BF16), w_ref) + b_ref[...]
    lane = lax.broadcasted_iota(jnp.int32, logits.shape, 1)
    lane_f = lane.astype(F32)
    lane_group = (lane // epg).astype(F32)
    neg = -jnp.inf
    big = float(V7X_LANES)

    gl = jnp.where((lane >= ne) & (lane < ne + ng), logits, neg)
    gmax = jnp.max(gl, axis=1, keepdims=True)
    gsel = jnp.min(jnp.where(gl == gmax, lane_f - ne, big), axis=1, keepdims=True)
    g_gate = 1.0 / jnp.sum(jnp.exp(gl - gmax), axis=1, keepdims=True)

    el = jnp.where((lane < ne) & (lane_group == gsel), logits, neg)
    m1 = jnp.max(el, axis=1, keepdims=True)
    i1 = jnp.min(jnp.where(el == m1, lane_f, big), axis=1, keepdims=True)
    el2 = jnp.where(lane_f == i1, neg, el)
    m2 = jnp.max(el2, axis=1, keepdims=True)
    i2 = jnp.min(jnp.where(el2 == m2, lane_f, big), axis=1, keepdims=True)
    e21 = jnp.exp(m2 - m1)
    w1 = g_gate / (1.0 + e21)
    w2 = g_gate * e21 / (1.0 + e21)

    oh1 = lane_f == i1
    oh2 = lane_f == i2
    oh = jnp.where(oh1 | oh2, 1.0, 0.0)
    row = lax.broadcasted_iota(jnp.int32, (tm, tm), 0)
    col = lax.broadcasted_iota(jnp.int32, (tm, tm), 1)
    before = _dot((col < row).astype(BF16), oh.astype(BF16))
    pos = run_ref[0:1, :] + before
    rank1 = jnp.sum(jnp.where(oh1, pos, 0.0), axis=1, keepdims=True)
    rank2 = jnp.sum(jnp.where(oh2, pos, 0.0), axis=1, keepdims=True)
    run_ref[...] = run_ref[...] + jnp.sum(oh, axis=0, keepdims=True)
    cnt_ref[...] = run_ref[...]

    rec = jnp.zeros(logits.shape, F32)
    for idx, val in ((_R_E1, i1), (_R_E2, i2), (_R_W1, w1), (_R_W2, w2),
                     (_R_RANK1, rank1), (_R_RANK2, rank2)):
        rec = jnp.where(lane == idx, val, rec)
    rec_ref[...] = rec


def _pack_bf16_pair(x):
    n = x.shape[1] // 2
    xr = x.astype(BF16).astype(F32)
    return pltpu.bitcast(xr[:, :n], jnp.uint32) | (pltpu.bitcast(xr[:, n:], jnp.uint32) >> 16)


def _unpack_bf16_pair(w):
    hi = pltpu.bitcast(w & jnp.uint32(0xFFFF0000), F32)
    lo = pltpu.bitcast(w << 16, F32)
    return jnp.concatenate([hi, lo], axis=1)


def _dispatch_kernel(dest_ref, h_ref, g_ref, rows_in_ref, rows_ref, u_ref, sem):
    del rows_in_ref
    tm = h_ref.shape[0]
    u_ref[...] = _pack_bf16_pair(_rms(h_ref[...], g_ref[...]))

    def row_copy(tok, slot):
        return pltpu.make_async_copy(u_ref.at[tok], rows_ref.at[dest_ref[2 * tok + slot]], sem)

    def issue(tok, _):
        row_copy(tok, 0).start()
        row_copy(tok, 1).start()
        return 0

    def drain(tok, _):
        row_copy(tok, 0).wait()
        row_copy(tok, 1).wait()
        return 0

    lax.fori_loop(0, tm, issue, 0, unroll=DMA_LOOP_UNROLL)
    lax.fori_loop(0, tm, drain, 0, unroll=DMA_LOOP_UNROLL)


def _expert_kernel(be_ref, nu_ref, x_ref, wg_ref, wu_ref, wd_ref, y_ref, wgb_ref, wub_ref, wdb_ref):
    i = pl.program_id(0)

    @pl.when(i < nu_ref[0])
    def _():
        changed = jnp.logical_or(i == 0, be_ref[i] != be_ref[jnp.maximum(i - 1, 0)])

        @pl.when(changed)
        def _():
            wgb_ref[...] = wg_ref[...].astype(BF16)
            wub_ref[...] = wu_ref[...].astype(BF16)
            wdb_ref[...] = wd_ref[...].astype(BF16)

        x = _unpack_bf16_pair(x_ref[...]).astype(BF16)
        gate = _dot(x, wgb_ref[...])
        up = _dot(x, wub_ref[...])
        hid = (gate * _sigmoid(gate) * up).astype(BF16)
        y_ref[...] = _pack_bf16_pair(_dot(hid, wdb_ref[...]))

    @pl.when(i >= nu_ref[0])
    def _():
        y_ref[...] = jnp.zeros(y_ref.shape, y_ref.dtype)


def _combine_kernel(dest_ref, h_ref, rec_ref, rows_ref, o_ref, y0_ref, y1_ref, sem):
    tm = h_ref.shape[0]

    def row_copy(tok, slot, buf):
        return pltpu.make_async_copy(rows_ref.at[dest_ref[2 * tok + slot]], buf.at[tok], sem)

    def issue(tok, _):
        row_copy(tok, 0, y0_ref).start()
        row_copy(tok, 1, y1_ref).start()
        return 0

    def drain(tok, _):
        row_copy(tok, 0, y0_ref).wait()
        row_copy(tok, 1, y1_ref).wait()
        return 0

    lax.fori_loop(0, tm, issue, 0, unroll=DMA_LOOP_UNROLL)
    lax.fori_loop(0, tm, drain, 0, unroll=DMA_LOOP_UNROLL)
    rec = rec_ref[...]
    w1 = rec[:, _R_W1:_R_W1 + 1]
    w2 = rec[:, _R_W2:_R_W2 + 1]
    o_ref[...] = h_ref[...] + (w1 * _unpack_bf16_pair(y0_ref[...]) + w2 * _unpack_bf16_pair(y1_ref[...]))


def _hier_moe(h, g_norm, w_group, b_group, w_expert, b_expert, layer, w_gate, w_up, w_down):
    t, d = h.shape
    ne, ng = MOE_EXPERTS, MOE_GROUPS
    ff = w_gate.shape[3]
    dp = d // 2
    tm = min(MOE_TOKEN_TILE, t)
    g2 = g_norm.reshape(1, d)
    lane_pad = V7X_LANES - ne - ng
    w_router = _split_weight(
        jnp.concatenate([w_expert, w_group, jnp.zeros((d, lane_pad), F32)], axis=1))
    b_router = jnp.concatenate([b_expert, b_group, jnp.zeros((lane_pad,), F32)]).reshape(1, V7X_LANES)
    rec, cnt = pl.pallas_call(
        _router_kernel,
        grid=(t // tm,),
        in_specs=[pl.BlockSpec((tm, d), lambda i: (i, 0)),
                  _const_spec((1, d)),
                  _const_spec((d, 2 * V7X_LANES)),
                  _const_spec((1, V7X_LANES))],
        out_specs=[pl.BlockSpec((tm, V7X_LANES), lambda i: (i, 0)),
                   _const_spec((V7X_SUBLANES, V7X_LANES))],
        out_shape=[jax.ShapeDtypeStruct((t, V7X_LANES), F32),
                   jax.ShapeDtypeStruct((V7X_SUBLANES, V7X_LANES), F32)],
        scratch_shapes=[pltpu.VMEM((V7X_SUBLANES, V7X_LANES), F32)],
        compiler_params=_cparams("arbitrary"),
        name="moe_router",
    )(h, g2, w_router, b_router)

    br = MOE_ROW_BLOCK
    n_blocks = -(-(2 * t + ne * (br - 1)) // br)
    n_rows = n_blocks * br
    counts = cnt[0, :ne].astype(jnp.int32)
    padded = (counts + br - 1) // br * br
    pad_end = jnp.cumsum(padded)
    pad_start = pad_end - padded
    eid = rec[:, _R_E1:_R_E2 + 1].astype(jnp.int32)
    rank = rec[:, _R_RANK1:_R_RANK2 + 1].astype(jnp.int32)
    experts = jnp.arange(ne, dtype=jnp.int32)
    start_of = jnp.sum(jnp.where(eid[:, :, None] == experts, pad_start, 0), axis=-1)
    dest = (start_of + rank).reshape(-1)
    block_row0 = jnp.arange(n_blocks, dtype=jnp.int32) * br
    block_expert = jnp.minimum(
        jnp.sum((pad_end[None, :] <= block_row0[:, None]).astype(jnp.int32), axis=1), ne - 1)
    n_used = (pad_end[-1:] // br).astype(jnp.int32)

    x_rows = pl.pallas_call(
        _dispatch_kernel,
        grid_spec=pltpu.PrefetchScalarGridSpec(
            num_scalar_prefetch=0,
            grid=(t // tm,),
            in_specs=[pl.BlockSpec((2 * tm,), lambda i: (i,), memory_space=pltpu.SMEM),
                      pl.BlockSpec((tm, d), lambda i: (i, 0)),
                      _const_spec((1, d)),
                      pl.BlockSpec(memory_space=pl.ANY)],
            out_specs=pl.BlockSpec(memory_space=pl.ANY),
            scratch_shapes=[pltpu.VMEM((tm, dp), jnp.uint32), pltpu.SemaphoreType.DMA(())]),
        out_shape=jax.ShapeDtypeStruct((n_rows, dp), jnp.uint32),
        input_output_aliases={3: 0},
        compiler_params=_cparams("arbitrary"),
        name="moe_dispatch",
    )(dest, h, g2, jnp.zeros((n_rows, dp), jnp.uint32))

    def blk(i, be, nu):
        return jnp.maximum(jnp.minimum(i, nu[0] - 1), 0)

    y_rows = pl.pallas_call(
        _expert_kernel,
        grid_spec=pltpu.PrefetchScalarGridSpec(
            num_scalar_prefetch=2,
            grid=(n_blocks,),
            in_specs=[pl.BlockSpec((br, dp), lambda i, be, nu: (blk(i, be, nu), 0)),
                      pl.BlockSpec((None, None, d, ff),
                                   lambda i, be, nu: (layer, be[blk(i, be, nu)], 0, 0)),
                      pl.BlockSpec((None, None, d, ff),
                                   lambda i, be, nu: (layer, be[blk(i, be, nu)], 0, 0)),
                      pl.BlockSpec((None, None, ff, d),
                                   lambda i, be, nu: (layer, be[blk(i, be, nu)], 0, 0))],
            out_specs=pl.BlockSpec((br, dp), lambda i, be, nu: (i, 0)),
            scratch_shapes=[pltpu.VMEM((d, ff), BF16), pltpu.VMEM((d, ff), BF16),
                            pltpu.VMEM((ff, d), BF16)]),
        out_shape=jax.ShapeDtypeStruct((n_rows, dp), jnp.uint32),
        compiler_params=_cparams("arbitrary"),
        name="moe_experts",
    )(block_expert, n_used, x_rows, w_gate, w_up, w_down)

    return pl.pallas_call(
        _combine_kernel,
        grid_spec=pltpu.PrefetchScalarGridSpec(
            num_scalar_prefetch=0,
            grid=(t // tm,),
            in_specs=[pl.BlockSpec((2 * tm,), lambda i: (i,), memory_space=pltpu.SMEM),
                      pl.BlockSpec((tm, d), lambda i: (i, 0)),
                      pl.BlockSpec((tm, V7X_LANES), lambda i: (i, 0)),
                      pl.BlockSpec(memory_space=pl.ANY)],
            out_specs=pl.BlockSpec((tm, d), lambda i: (i, 0)),
            scratch_shapes=[pltpu.VMEM((tm, dp), jnp.uint32), pltpu.VMEM((tm, dp), jnp.uint32),
                            pltpu.SemaphoreType.DMA(())]),
        out_shape=jax.ShapeDtypeStruct((t, d), F32),
        compiler_params=_cparams("arbitrary"),
        name="moe_combine",
    )(dest, h, rec, y_rows)


def _rope_tables(positions):
    half = DIFF_HEAD_DIM // 2
    inv = ROPE_THETA ** (-jnp.arange(0, DIFF_HEAD_DIM, 2, dtype=F32) / DIFF_HEAD_DIM)
    ang = positions.astype(F32).reshape(-1, 1) * inv
    cos, sin = jnp.cos(ang), jnp.sin(ang)
    cos_t = jnp.concatenate([cos] * 4, axis=1)
    sin_t = jnp.concatenate([-sin, sin, -sin, sin], axis=1)
    assert cos_t.shape[1] == 4 * half
    return cos_t, sin_t


def kernel(x, positions, norm_mix, norm_ffn, ssd_w_in, ssd_conv_w, ssd_conv_b, ssd_dt_bias, ssd_a_log, ssd_d, ssd_norm, ssd_w_out, sb_w_in, sb_w_out, lru_w_in, lru_conv_w, lru_conv_b, lru_w_rg, lru_b_rg, lru_w_ig, lru_b_ig, lru_lambda, lru_w_out, diff_w_in, diff_q_norm, diff_k_norm, diff_lam_q1, diff_lam_k1, diff_lam_q2, diff_lam_k2, diff_sub_norm, diff_w_out, moe_w_group, moe_b_group, moe_w_expert, moe_b_expert, moe_w_gate, moe_w_up, moe_w_down):
    bsz, seqlen, d = x.shape
    depth = norm_mix.shape[0]
    h = x.reshape(bsz * seqlen, d)
    cos_t, sin_t = _rope_tables(positions)
    for i in range(depth):
        m, j = i % N_MIXERS, i // N_MIXERS
        if m == 0:
            h = _ssd_mixer(h, bsz, seqlen, norm_mix[i], ssd_w_in[j], ssd_conv_w[j], ssd_conv_b[j],
                           ssd_dt_bias[j], ssd_a_log[j], ssd_d[j], ssd_norm[j], ssd_w_out[j])
        elif m == 1:
            h = _sb_mixer(h, bsz, seqlen, norm_mix[i], sb_w_in[j], sb_w_out[j])
        elif m == 2:
            h = _lru_mixer(h, bsz, seqlen, norm_mix[i], lru_w_in[j], lru_conv_w[j], lru_conv_b[j],
                           lru_w_rg[j], lru_b_rg[j], lru_w_ig[j], lru_b_ig[j], lru_lambda[j],
                           lru_w_out[j])
        else:
            lambda_init = 0.8 - 0.6 * math.exp(-0.3 * i)
            h = _diff_mixer(h, bsz, seqlen, cos_t, sin_t, norm_mix[i], diff_w_in[j], diff_q_norm[j],
                            diff_k_norm[j], diff_lam_q1[j], diff_lam_k1[j], diff_lam_q2[j],
                            diff_lam_k2[j], diff_sub_norm[j], diff_w_out[j], lambda_init)
        h = _hier_moe(h, norm_ffn[i], moe_w_group[i], moe_b_group[i], moe_w_expert[i],
                      moe_b_expert[i], i, moe_w_gate, moe_w_up, moe_w_down)
    return h.reshape(bsz, seqlen, d)
```

```python
import functools
import math

import jax
import jax.numpy as jnp
from jax import lax
from jax.experimental import pallas as pl
from jax.experimental.pallas import tpu as pltpu

F32 = jnp.float32
BF16 = jnp.bfloat16
HIGHEST = lax.Precision.HIGHEST

RMS_EPS = 1e-6
_LOG2_E = math.log2(math.e)
N_MIXERS = 4

V7X_VMEM_BYTES = 64 * 1024 * 1024
V7X_LANES = 128
V7X_SUBLANES = 8
VMEM_LIMIT = (V7X_VMEM_BYTES * 7) // 8

SSD_HEAD_DIM = 64
SSD_GROUPS = 8
SSD_STATE = 128
SSD_CONV = 4
SSD_CHUNK = 128
SB_HEAD_DIM = 64
LRU_BLOCKS = 16
LRU_CONV = 4
LRU_C = 8.0
DIFF_HEAD_DIM = 64
ROPE_THETA = 10000.0
MOE_GROUPS = 4
MOE_EXPERTS_PER_GROUP = 8
MOE_EXPERTS = MOE_GROUPS * MOE_EXPERTS_PER_GROUP

ROW_TILE = 512
SSD_IN_TILE = 256
SSD_CONV_CHUNK = 512
ATTN_TILE = 256
SB_PAIRS_PER_STEP = 4
DIFF_HEADS_PER_STEP = 4
LRU_TIME_TILE = 64
LRU_PITCH_PAD = 4
LRU_BAND = 3 * V7X_LANES
LRU_SCAN_UNROLL = 4
MOE_TOKEN_TILE = 512
MOE_ROW_BLOCK = 256
CONV_HALO = V7X_SUBLANES
DMA_LOOP_UNROLL = 8


def _cparams(*sem):
    return pltpu.CompilerParams(dimension_semantics=sem, vmem_limit_bytes=VMEM_LIMIT)


def _const_spec(shape):
    nd = len(shape)
    return pl.BlockSpec(shape, lambda *_: (0,) * nd)


def _rms(x, w):
    ms = jnp.mean(x * x, axis=-1, keepdims=True)
    return x * lax.rsqrt(ms + RMS_EPS) * w


def _sigmoid(x):
    return 1.0 / (1.0 + jnp.exp(-x))


def _softplus(x):
    return jnp.maximum(x, 0.0) + jnp.log(1.0 + jnp.exp(-jnp.abs(x)))


def _dot(a, b, **kw):
    return jnp.dot(a, b, preferred_element_type=F32, **kw)


def _dot_nt(a, b):
    return lax.dot_general(a, b, (((1,), (1,)), ((), ())), preferred_element_type=F32)


def _dot_tn(a, b):
    return lax.dot_general(a, b, (((0,), (0,)), ((), ())), preferred_element_type=F32)


def _split_bf16(x):
    hi = x.astype(BF16)
    return hi, (x - hi.astype(F32)).astype(BF16)


def _split_weight(w):
    return jnp.concatenate(_split_bf16(w), axis=1)


def _dot_split(x, x_hi, w2_ref):
    n = w2_ref.shape[1] // 2
    x_lo = (x - x_hi.astype(F32)).astype(BF16)
    t = _dot(x_hi, w2_ref[...])
    return t[:, :n] + t[:, n:] + _dot(x_lo, w2_ref[:, :n])


def _proj_residual_kernel(y_ref, w_ref, h_ref, o_ref):
    o_ref[...] = h_ref[...] + _dot(y_ref[...], w_ref[...])


def _proj_residual(y, w, h):
    t, k = y.shape
    d = w.shape[1]
    tm = ROW_TILE
    return pl.pallas_call(
        _proj_residual_kernel,
        grid=(t // tm,),
        in_specs=[pl.BlockSpec((tm, k), lambda i: (i, 0)),
                  _const_spec((k, d)),
                  pl.BlockSpec((tm, d), lambda i: (i, 0))],
        out_specs=pl.BlockSpec((tm, d), lambda i: (i, 0)),
        out_shape=jax.ShapeDtypeStruct((t, d), F32),
        compiler_params=_cparams("arbitrary"),
        name="proj_residual",
    )(y, w, h)


def _ssd_in_kernel(h_ref, g_ref, wz_ref, wx_ref, wdt_ref, cw_ref, cb_ref,
                   z_ref, xbc_ref, dt_ref, pad_ref, *, tiles_per_seq):
    i = pl.program_id(0)
    tm = h_ref.shape[0]
    u32 = _rms(h_ref[...], g_ref[...])
    u = u32.astype(BF16)
    z_ref[...] = _dot(u, wz_ref[...]).astype(z_ref.dtype)
    dt_ref[...] = _dot_split(u32, u, wdt_ref)

    @pl.when(i % tiles_per_seq == 0)
    def _():
        pad_ref[0:CONV_HALO, :] = jnp.zeros((CONV_HALO, pad_ref.shape[1]), F32)

    conv_dim = pad_ref.shape[1]
    for c0 in range(0, conv_dim, SSD_CONV_CHUNK):
        cs = slice(c0, c0 + SSD_CONV_CHUNK)
        pad_ref[CONV_HALO:CONV_HALO + tm, cs] = _dot(u, wx_ref[:, cs])
        acc = cb_ref[:, cs] + cw_ref[SSD_CONV - 1:SSD_CONV, cs] * pad_ref[CONV_HALO:CONV_HALO + tm, cs]
        for k in range(SSD_CONV - 1):
            off = CONV_HALO - (SSD_CONV - 1) + k
            acc = acc + cw_ref[k:k + 1, cs] * pad_ref[off:off + tm, cs]
        xbc_ref[:, cs] = (acc * _sigmoid(acc)).astype(xbc_ref.dtype)
        pad_ref[0:CONV_HALO, cs] = pad_ref[tm:tm + CONV_HALO, cs]


def _ssd_scan_kernel(xs_ref, b_ref, c_ref, z_ref, dt_ref, dtt_ref,
                     bias_ref, biast_ref, alog_ref, alogt_ref, dfull_ref, nw_ref,
                     o_ref, state_ref):
    q = xs_ref.shape[0]
    n_heads = dt_ref.shape[1]
    d_inner = xs_ref.shape[1]
    heads_per_group = n_heads // SSD_GROUPS
    gw = d_inner // SSD_GROUPS
    p = SSD_HEAD_DIM

    @pl.when(pl.program_id(1) == 0)
    def _():
        state_ref[...] = jnp.zeros(state_ref.shape, F32)

    dt = _softplus(dt_ref[...] + bias_ref[...])
    a = -jnp.exp(alog_ref[...]) * dt
    dtt = _softplus(dtt_ref[...] + biast_ref[...])
    at = -jnp.exp(alogt_ref[...]) * dtt

    row = lax.broadcasted_iota(jnp.int32, (q, q), 0)
    col = lax.broadcasted_iota(jnp.int32, (q, q), 1)
    lower = row >= col
    a_cum = _dot(lower.astype(F32), a, precision=HIGHEST)
    a_cumt = _dot(at, (row <= col).astype(F32), precision=HIGHEST)
    a_tot = a_cum[q - 1:q, :]

    hh = lax.broadcasted_iota(jnp.int32, (2 * n_heads, d_inner), 0)
    cc = lax.broadcasted_iota(jnp.int32, (2 * n_heads, d_inner), 1)
    expand = (cc // p == hh % n_heads).astype(BF16)
    per_head = jnp.concatenate(
        [dt, jnp.exp(a_cum), jnp.exp(a_tot - a_cum),
         jnp.broadcast_to(jnp.exp(a_tot), (V7X_SUBLANES, n_heads))], axis=0)
    per_channel = _dot(jnp.concatenate(_split_bf16(per_head), axis=1), expand)
    dt_full = per_channel[0:q]
    dec_out_full = per_channel[q:2 * q]
    dec_st_full = per_channel[2 * q:3 * q]
    chunk_dec_full = per_channel[3 * q:3 * q + 1]

    lane = lax.broadcasted_iota(jnp.int32, (1, gw), 1)
    for g in range(SSD_GROUPS):
        sl = slice(g * gw, (g + 1) * gw)
        bg = b_ref[:, g * SSD_STATE:(g + 1) * SSD_STATE]
        cg = c_ref[:, g * SSD_STATE:(g + 1) * SSD_STATE]
        xs_g = xs_ref[:, sl].astype(F32)
        xdt = xs_g * dt_full[:, sl]
        cb = _dot_nt(cg, bg)
        s_prev = state_ref[g]
        y = _dot(cg, s_prev.astype(BF16)) * dec_out_full[:, sl]
        for r in range(heads_per_group):
            hd = g * heads_per_group + r
            seg = a_cum[:, hd:hd + 1] - a_cumt[hd:hd + 1, :]
            m = jnp.where(lower, cb * jnp.exp(seg), 0.0).astype(BF16)
            xr = jnp.where((lane >= r * p) & (lane < (r + 1) * p), xdt, 0.0).astype(BF16)
            y = y + _dot(m, xr)
        xw = (xdt * dec_st_full[:, sl]).astype(BF16)
        state_ref[g] = s_prev * chunk_dec_full[:, sl] + _dot_tn(bg, xw)
        y = y + xs_g * dfull_ref[:, sl]
        zg = z_ref[:, sl].astype(F32)
        yz = y * (zg * _sigmoid(zg))
        o_ref[:, sl] = _rms(yz, nw_ref[:, sl]).astype(o_ref.dtype)


def _ssd_mixer(h, bsz, seqlen, g_norm, w_in, conv_w, conv_b, dt_bias, a_log, d_skip, norm_w, w_out):
    t, d = h.shape
    n_heads = dt_bias.shape[0]
    d_inner = n_heads * SSD_HEAD_DIM
    gn = SSD_GROUPS * SSD_STATE
    conv_dim = d_inner + 2 * gn
    wz = w_in[:, :d_inner].astype(BF16)
    wx = w_in[:, d_inner:d_inner + conv_dim].astype(BF16)
    wdt = _split_weight(w_in[:, d_inner + conv_dim:])
    tm = SSD_IN_TILE
    z, xbc, dt_raw = pl.pallas_call(
        functools.partial(_ssd_in_kernel, tiles_per_seq=seqlen // tm),
        grid=(t // tm,),
        in_specs=[pl.BlockSpec((tm, d), lambda i: (i, 0)),
                  _const_spec((1, d)),
                  _const_spec((d, d_inner)),
                  _const_spec((d, conv_dim)),
                  _const_spec((d, 2 * n_heads)),
                  _const_spec((SSD_CONV, conv_dim)),
                  _const_spec((1, conv_dim))],
        out_specs=[pl.BlockSpec((tm, d_inner), lambda i: (i, 0)),
                   pl.BlockSpec((tm, conv_dim), lambda i: (i, 0)),
                   pl.BlockSpec((tm, n_heads), lambda i: (i, 0))],
        out_shape=[jax.ShapeDtypeStruct((t, d_inner), BF16),
                   jax.ShapeDtypeStruct((t, conv_dim), BF16),
                   jax.ShapeDtypeStruct((t, n_heads), F32)],
        scratch_shapes=[pltpu.VMEM((tm + CONV_HALO, conv_dim), F32)],
        compiler_params=_cparams("arbitrary"),
        name="ssd_in",
    )(h, g_norm.reshape(1, d), wz, wx, wdt, conv_w, conv_b.reshape(1, conv_dim))

    q = SSD_CHUNK
    nc = seqlen // q
    xbc3 = xbc.reshape(bsz, seqlen, conv_dim)
    dt3 = dt_raw.reshape(bsz, seqlen, n_heads)
    dtt3 = jnp.swapaxes(dt3, 1, 2)
    nb_x = d_inner // gn
    yg = pl.pallas_call(
        _ssd_scan_kernel,
        grid=(bsz, nc),
        in_specs=[pl.BlockSpec((None, q, d_inner), lambda b, c: (b, c, 0)),
                  pl.BlockSpec((None, q, gn), lambda b, c: (b, c, nb_x)),
                  pl.BlockSpec((None, q, gn), lambda b, c: (b, c, nb_x + 1)),
                  pl.BlockSpec((None, q, d_inner), lambda b, c: (b, c, 0)),
                  pl.BlockSpec((None, q, n_heads), lambda b, c: (b, c, 0)),
                  pl.BlockSpec((None, n_heads, q), lambda b, c: (b, 0, c)),
                  _const_spec((1, n_heads)), _const_spec((n_heads, 1)),
                  _const_spec((1, n_heads)), _const_spec((n_heads, 1)),
                  _const_spec((1, d_inner)), _const_spec((1, d_inner))],
        out_specs=pl.BlockSpec((None, q, d_inner), lambda b, c: (b, c, 0)),
        out_shape=jax.ShapeDtypeStruct((bsz, seqlen, d_inner), BF16),
        scratch_shapes=[pltpu.VMEM((SSD_GROUPS, SSD_STATE, d_inner // SSD_GROUPS), F32)],
        compiler_params=_cparams("arbitrary", "arbitrary"),
        name="ssd_scan",
    )(xbc3, xbc3, xbc3, z.reshape(bsz, seqlen, d_inner), dt3, dtt3,
      dt_bias.reshape(1, n_heads), dt_bias.reshape(n_heads, 1),
      a_log.reshape(1, n_heads), a_log.reshape(n_heads, 1),
      jnp.repeat(d_skip, SSD_HEAD_DIM).reshape(1, d_inner), norm_w.reshape(1, d_inner))
    return _proj_residual(yg.reshape(t, d_inner), w_out.astype(BF16), h)


def _sb_in_kernel(h_ref, g_ref, w_ref, o_ref, *, q_cols, q_scale):
    u = _rms(h_ref[...], g_ref[...]).astype(BF16)
    o_ref[:, :q_cols] = (_dot(u, w_ref[:, :q_cols]) * q_scale).astype(o_ref.dtype)
    o_ref[:, q_cols:] = _dot(u, w_ref[:, q_cols:]).astype(o_ref.dtype)


def _sb_attn_kernel(q_ref, k_ref, v_ref, o_ref, acc_ref, *, tile):
    seqlen = q_ref.shape[0]
    nq = seqlen // tile
    pw = 2 * SB_HEAD_DIM
    n_pairs = q_ref.shape[1] // pw
    lane = lax.broadcasted_iota(jnp.int32, (1, pw), 1)
    first = lane < SB_HEAD_DIM
    row = lax.broadcasted_iota(jnp.int32, (tile, tile), 0)
    col = lax.broadcasted_iota(jnp.int32, (tile, tile), 1)
    neg_from = jnp.where(row >= col, -1.0, 0.0).astype(BF16)
    past = col < row
    past2 = jnp.concatenate([past, past], axis=0)

    def k_tile(qs, kj, csums, diagonal):
        rows = pl.ds(pl.multiple_of(kj * tile, tile), tile)
        out = []
        for p in range(n_pairs):
            k = k_ref[rows, p * pw:(p + 1) * pw]
            v = v_ref[rows, p * pw:(p + 1) * pw]
            t = _dot_nt(qs[p], k)
            neg_abs = pltpu.bitcast(pltpu.bitcast(t, jnp.uint32) | jnp.uint32(0x80000000), F32)
            sp = jnp.maximum(t, 0.0) + jnp.log(1.0 + jnp.exp2(neg_abs)) * _LOG2_E
            if diagonal:
                sp = jnp.where(past2, sp, 0.0)
            w = jnp.exp2(t + _dot(sp.astype(BF16), neg_from) + csums[p])
            if diagonal:
                w = jnp.where(past2, w, 0.0)
            acc_ref[p] += _dot(w.astype(BF16), v)
            out.append(csums[p] - jnp.sum(sp, axis=1, keepdims=True))
        return tuple(out)

    def q_body(qi, _):
        qrows = pl.ds(pl.multiple_of(qi * tile, tile), tile)
        qs = []
        for p in range(n_pairs):
            q = q_ref[qrows, p * pw:(p + 1) * pw]
            zero = jnp.zeros_like(q)
            qs.append(jnp.concatenate([jnp.where(first, q, zero), jnp.where(first, zero, q)], axis=0))
            acc_ref[p] = jnp.zeros(acc_ref.shape[1:], F32)
        init = tuple(jnp.zeros((2 * tile, 1), F32) for _ in range(n_pairs))
        csums = k_tile(qs, qi, init, True)
        lax.fori_loop(0, qi, lambda s, c: k_tile(qs, qi - 1 - s, c, False), csums)
        for p in range(n_pairs):
            o = jnp.where(first, acc_ref[p, 0:tile, :], acc_ref[p, tile:2 * tile, :])
            o_ref[qrows, p * pw:(p + 1) * pw] = o.astype(o_ref.dtype)
        return 0

    lax.fori_loop(0, nq, q_body, 0)


def _sb_mixer(h, bsz, seqlen, g_norm, w_in, w_out):
    t, d = h.shape
    hd = w_in.shape[1] // 3
    tm = ROW_TILE
    qkv = pl.pallas_call(
        functools.partial(_sb_in_kernel, q_cols=hd, q_scale=SB_HEAD_DIM ** -0.5 * _LOG2_E),
        grid=(t // tm,),
        in_specs=[pl.BlockSpec((tm, d), lambda i: (i, 0)),
                  _const_spec((1, d)),
                  _const_spec((d, 3 * hd))],
        out_specs=pl.BlockSpec((tm, 3 * hd), lambda i: (i, 0)),
        out_shape=jax.ShapeDtypeStruct((t, 3 * hd), BF16),
        compiler_params=_cparams("arbitrary"),
        name="sb_in",
    )(h, g_norm.reshape(1, d), w_in.astype(BF16))
    qkv3 = qkv.reshape(bsz, seqlen, 3 * hd)
    bw = SB_PAIRS_PER_STEP * 2 * SB_HEAD_DIM
    nb = hd // bw
    tile = min(ATTN_TILE, seqlen)
    o = pl.pallas_call(
        functools.partial(_sb_attn_kernel, tile=tile),
        grid=(bsz, nb),
        in_specs=[pl.BlockSpec((None, seqlen, bw), lambda b, j: (b, 0, j)),
                  pl.BlockSpec((None, seqlen, bw), lambda b, j: (b, 0, nb + j)),
                  pl.BlockSpec((None, seqlen, bw), lambda b, j: (b, 0, 2 * nb + j))],
        out_specs=pl.BlockSpec((None, seqlen, bw), lambda b, j: (b, 0, j)),
        out_shape=jax.ShapeDtypeStruct((bsz, seqlen, hd), BF16),
        scratch_shapes=[pltpu.VMEM((SB_PAIRS_PER_STEP, 2 * tile, 2 * SB_HEAD_DIM), F32)],
        compiler_params=_cparams("arbitrary", "arbitrary"),
        name="sb_attn",
    )(qkv3, qkv3, qkv3)
    return _proj_residual(o.reshape(t, hd), w_out.astype(BF16), h)


def _lru_kernel(h_ref, g_ref, win_ref, cw_ref, cb_ref, wband_ref, br_ref, bi_ref, lam_ref,
                y_ref, pad_ref, xc_ref, a_ref, b_ref, hs_ref, carry_ref, *, band_starts):
    bsz, tt, d = h_ref.shape
    width = y_ref.shape[2]

    @pl.when(pl.program_id(0) == 0)
    def _():
        carry_ref[...] = jnp.zeros(carry_ref.shape, F32)
        for b in range(bsz):
            pad_ref[b, 0:CONV_HALO, :] = jnp.zeros((CONV_HALO, width), F32)

    u = _rms(h_ref[...].reshape(bsz * tt, d), g_ref[...]).astype(BF16)
    proj = _dot(u, win_ref[...])
    gate_branch = proj[:, :width]
    for b in range(bsz):
        pad_ref[b, CONV_HALO:CONV_HALO + tt, :] = proj[b * tt:(b + 1) * tt, width:]
        acc = cb_ref[...] + cw_ref[LRU_CONV - 1:LRU_CONV, :] * pad_ref[b, CONV_HALO:CONV_HALO + tt, :]
        for k in range(LRU_CONV - 1):
            off = CONV_HALO - (LRU_CONV - 1) + k
            acc = acc + cw_ref[k:k + 1, :] * pad_ref[b, off:off + tt, :]
        xc_ref[b * tt:(b + 1) * tt, :] = acc
        pad_ref[b, 0:CONV_HALO, :] = pad_ref[b, tt:tt + CONV_HALO, :]

    xcb = xc_ref[...].astype(BF16)
    n_lt = width // V7X_LANES
    pitch = a_ref.shape[1] // bsz
    neg_c_softplus = -LRU_C * _softplus(-lam_ref[...])
    for c in range(n_lt):
        cs = slice(c * V7X_LANES, (c + 1) * V7X_LANES)
        pre = _dot(xcb[:, band_starts[c]:band_starts[c] + LRU_BAND], wband_ref[c])
        r = _sigmoid(pre[:, :V7X_LANES] + br_ref[:, cs])
        ig = _sigmoid(pre[:, V7X_LANES:] + bi_ref[:, cs])
        a = jnp.exp(neg_c_softplus[:, cs] * r)
        bterm = jnp.sqrt(1.0 - a * a) * (ig * xc_ref[:, cs])
        for b in range(bsz):
            a_ref[c, b * pitch:b * pitch + tt, :] = a[b * tt:(b + 1) * tt, :]
            b_ref[c, b * pitch:b * pitch + tt, :] = bterm[b * tt:(b + 1) * tt, :]

    def step(j, hprev):
        hnew = []
        for c in range(n_lt):
            rows_j = pl.ds(j, bsz, stride=pitch)
            hc = a_ref[c, rows_j, :] * hprev[c] + b_ref[c, rows_j, :]
            hs_ref[c, rows_j, :] = hc
            hnew.append(hc)
        return tuple(hnew)

    hlast = lax.fori_loop(0, tt, step, tuple(carry_ref[c] for c in range(n_lt)), unroll=LRU_SCAN_UNROLL)
    for c in range(n_lt):
        carry_ref[c] = hlast[c]
    c0 = math.sqrt(2.0 / math.pi)
    gelu = 0.5 * gate_branch * (1.0 + jnp.tanh(c0 * (gate_branch + 0.044715 * gate_branch ** 3)))
    for b in range(bsz):
        hs_b = jnp.concatenate([hs_ref[c, b * pitch:b * pitch + tt, :] for c in range(n_lt)], axis=1)
        y_ref[b] = (hs_b * gelu[b * tt:(b + 1) * tt, :]).astype(y_ref.dtype)


def _block_diag(w):
    nb, bd, _ = w.shape
    rows = [jnp.pad(w[b], ((0, 0), (b * bd, (nb - 1 - b) * bd))) for b in range(nb)]
    return jnp.concatenate(rows, axis=0)


def _lru_band_starts(width):
    bd = width // LRU_BLOCKS
    starts = []
    for j in range(width // V7X_LANES):
        b_lo = (j * V7X_LANES) // bd
        b_hi = ((j + 1) * V7X_LANES - 1) // bd
        lo = (b_lo * bd) // V7X_LANES * V7X_LANES
        assert (b_hi + 1) * bd - lo <= LRU_BAND
        starts.append(min(lo, width - LRU_BAND))
    return tuple(starts)


def _lru_band_weights(w_rg, w_ig, starts):
    dense_r, dense_i = _block_diag(w_rg), _block_diag(w_ig)
    tiles = []
    for j, s in enumerate(starts):
        cs = slice(j * V7X_LANES, (j + 1) * V7X_LANES)
        tiles.append(jnp.concatenate([dense_r[s:s + LRU_BAND, cs], dense_i[s:s + LRU_BAND, cs]], axis=1))
    return jnp.stack(tiles).astype(BF16)


def _lru_mixer(h, bsz, seqlen, g_norm, w_in, conv_w, conv_b, w_rg, b_rg, w_ig, b_ig, lam, w_out):
    t, d = h.shape
    width = lam.shape[0]
    tt = min(LRU_TIME_TILE, seqlen)
    rows = bsz * tt
    n_lt = width // V7X_LANES
    scan_rows = bsz * (tt + LRU_PITCH_PAD)
    starts = _lru_band_starts(width)
    y = pl.pallas_call(
        functools.partial(_lru_kernel, band_starts=starts),
        grid=(seqlen // tt,),
        in_specs=[pl.BlockSpec((bsz, tt, d), lambda i: (0, i, 0)),
                  _const_spec((1, d)),
                  _const_spec((d, 2 * width)),
                  _const_spec((LRU_CONV, width)), _const_spec((1, width)),
                  _const_spec((n_lt, LRU_BAND, 2 * V7X_LANES)),
                  _const_spec((1, width)), _const_spec((1, width)),
                  _const_spec((1, width))],
        out_specs=pl.BlockSpec((bsz, tt, width), lambda i: (0, i, 0)),
        out_shape=jax.ShapeDtypeStruct((bsz, seqlen, width), BF16),
        scratch_shapes=[pltpu.VMEM((bsz, tt + CONV_HALO, width), F32),
                        pltpu.VMEM((rows, width), F32),
                        pltpu.VMEM((n_lt, scan_rows, V7X_LANES), F32),
                        pltpu.VMEM((n_lt, scan_rows, V7X_LANES), F32),
                        pltpu.VMEM((n_lt, scan_rows, V7X_LANES), F32),
                        pltpu.VMEM((n_lt, bsz, V7X_LANES), F32)],
        compiler_params=_cparams("arbitrary"),
        name="lru",
    )(h.reshape(bsz, seqlen, d), g_norm.reshape(1, d), w_in.astype(BF16),
      conv_w, conv_b.reshape(1, width), _lru_band_weights(w_rg, w_ig, starts),
      b_rg.reshape(1, width), b_ig.reshape(1, width), lam.reshape(1, width))
    return _proj_residual(y.reshape(t, width), w_out.astype(BF16), h)


def _diff_in_kernel(h_ref, g_ref, w_ref, cos_ref, sin_ref, qn_ref, kn_ref, q_ref, k_ref, v_ref,
                    *, q_scale):
    d_q = q_ref.shape[1]
    hdim = DIFF_HEAD_DIM
    n_grp = d_q // hdim
    u = _rms(h_ref[...], g_ref[...]).astype(BF16)
    v_ref[...] = _dot(u, w_ref[:, 2 * d_q:]).astype(v_ref.dtype)

    ci = lax.broadcasted_iota(jnp.int32, (d_q, n_grp), 0)
    gi = lax.broadcasted_iota(jnp.int32, (d_q, n_grp), 1)
    gsum = (ci // hdim == gi).astype(BF16)
    gj = lax.broadcasted_iota(jnp.int32, (2 * n_grp, d_q), 0)
    cj = lax.broadcasted_iota(jnp.int32, (2 * n_grp, d_q), 1)
    gexp = (cj // hdim == gj % n_grp).astype(BF16)
    lane = lax.broadcasted_iota(jnp.int32, (1, d_q), 1)
    first_half = (lane % hdim) < (hdim // 2)
    reps = d_q // cos_ref.shape[1]
    cos_t = jnp.tile(cos_ref[...], (1, reps))
    sin_t = jnp.tile(sin_ref[...], (1, reps))

    def norm_rope(x, nw):
        sq = x * x
        sq_hi = sq.astype(BF16)
        sq_lo = (sq - sq_hi.astype(F32)).astype(BF16)
        ms = (_dot(sq_hi, gsum) + _dot(sq_lo, gsum)) * (1.0 / hdim)
        inv = lax.rsqrt(ms + RMS_EPS)
        inv_hi = inv.astype(BF16)
        inv_lo = (inv - inv_hi.astype(F32)).astype(BF16)
        inv_full = _dot(jnp.concatenate([inv_hi, inv_lo], axis=1), gexp)
        xn = x * inv_full * nw
        rot = jnp.where(first_half, pltpu.roll(xn, d_q - hdim // 2, 1), pltpu.roll(xn, hdim // 2, 1))
        return xn * cos_t + rot * sin_t

    q_ref[...] = (norm_rope(_dot(u, w_ref[:, :d_q]), qn_ref[...]) * q_scale).astype(q_ref.dtype)
    k_ref[...] = norm_rope(_dot(u, w_ref[:, d_q:2 * d_q]), kn_ref[...]).astype(k_ref.dtype)


def _diff_attn_kernel(q_ref, k_ref, v_ref, lq1_ref, lk1_ref, lq2_ref, lk2_ref, sn_ref, o_ref,
                      acc_ref, *, tile, lambda_init):
    seqlen = q_ref.shape[0]
    nq = seqlen // tile
    hdim = DIFF_HEAD_DIM
    hw = 2 * hdim
    n_heads = q_ref.shape[1] // hw
    lane = lax.broadcasted_iota(jnp.int32, (1, hw), 1)
    first = lane < hdim
    row = lax.broadcasted_iota(jnp.int32, (tile, tile), 0)
    col = lax.broadcasted_iota(jnp.int32, (tile, tile), 1)
    causal = col <= row
    causal2 = jnp.concatenate([causal, causal], axis=0)
    ones = jnp.ones((tile, hw), BF16)
    lam = (jnp.exp(jnp.sum(lq1_ref[...] * lk1_ref[...], axis=1, keepdims=True))
           - jnp.exp(jnp.sum(lq2_ref[...] * lk2_ref[...], axis=1, keepdims=True)) + lambda_init)

    def k_tile(qs, kj, maxes, diagonal):
        rows = pl.ds(pl.multiple_of(kj * tile, tile), tile)
        out = []
        for hd in range(n_heads):
            k = k_ref[rows, hd * hw:(hd + 1) * hw]
            v1 = jnp.concatenate([v_ref[rows, hd * hw:(hd + 1) * hw], ones], axis=1)
            s = _dot_nt(qs[hd], k)
            if diagonal:
                s = jnp.where(causal2, s, -jnp.inf)
            mx_new = jnp.maximum(maxes[hd], jnp.max(s, axis=1, keepdims=True))
            alpha = jnp.exp2(maxes[hd] - mx_new)
            pr = jnp.exp2(s - mx_new)
            acc_ref[hd] = acc_ref[hd] * alpha + _dot(pr.astype(BF16), v1)
            out.append(mx_new)
        return tuple(out)

    def q_body(qi, _):
        qrows = pl.ds(pl.multiple_of(qi * tile, tile), tile)
        qs = []
        for hd in range(n_heads):
            q = q_ref[qrows, hd * hw:(hd + 1) * hw]
            zero = jnp.zeros_like(q)
            qs.append(jnp.concatenate([jnp.where(first, q, zero), jnp.where(first, zero, q)], axis=0))
            acc_ref[hd] = jnp.zeros(acc_ref.shape[1:], F32)
        init = tuple(jnp.full((2 * tile, 1), -jnp.inf, F32) for _ in range(n_heads))
        maxes = k_tile(qs, qi, init, True)
        lax.fori_loop(0, qi, lambda s, c: k_tile(qs, qi - 1 - s, c, False), maxes)
        for hd in range(n_heads):
            p = acc_ref[hd, :, 0:hw] / acc_ref[hd, :, hw:2 * hw]
            o = p[0:tile] - lam * p[tile:2 * tile]
            o = _rms(o, sn_ref[...]) * (1.0 - lambda_init)
            o_ref[qrows, hd * hw:(hd + 1) * hw] = o.astype(o_ref.dtype)
        return 0

    lax.fori_loop(0, nq, q_body, 0)


def _diff_mixer(h, bsz, seqlen, cos_t, sin_t, g_norm, w_in, q_norm, k_norm, lq1, lk1, lq2, lk2,
                sub_norm, w_out, lambda_init):
    t, d = h.shape
    d_q = w_in.shape[1] // 3
    hdim = DIFF_HEAD_DIM
    tm = ROW_TILE
    tw = cos_t.shape[1]
    q, k, v = pl.pallas_call(
        functools.partial(_diff_in_kernel, q_scale=hdim ** -0.5 * _LOG2_E),
        grid=(t // tm,),
        in_specs=[pl.BlockSpec((tm, d), lambda i: (i, 0)),
                  _const_spec((1, d)),
                  _const_spec((d, 3 * d_q)),
                  pl.BlockSpec((tm, tw), lambda i: (i, 0)),
                  pl.BlockSpec((tm, tw), lambda i: (i, 0)),
                  _const_spec((1, d_q)), _const_spec((1, d_q))],
        out_specs=[pl.BlockSpec((tm, d_q), lambda i: (i, 0))] * 3,
        out_shape=[jax.ShapeDtypeStruct((t, d_q), BF16)] * 3,
        compiler_params=_cparams("arbitrary"),
        name="diff_in",
    )(h, g_norm.reshape(1, d), w_in.astype(BF16), cos_t, sin_t,
      jnp.tile(q_norm, d_q // hdim).reshape(1, d_q), jnp.tile(k_norm, d_q // hdim).reshape(1, d_q))
    hw = 2 * hdim
    bw = DIFF_HEADS_PER_STEP * hw
    tile = min(ATTN_TILE, seqlen)
    spec = pl.BlockSpec((None, seqlen, bw), lambda b, j: (b, 0, j))
    vec = _const_spec((1, hdim))
    o = pl.pallas_call(
        functools.partial(_diff_attn_kernel, tile=tile, lambda_init=lambda_init),
        grid=(bsz, d_q // bw),
        in_specs=[spec, spec, spec, vec, vec, vec, vec, _const_spec((1, hw))],
        out_specs=spec,
        out_shape=jax.ShapeDtypeStruct((bsz, seqlen, d_q), BF16),
        scratch_shapes=[pltpu.VMEM((DIFF_HEADS_PER_STEP, 2 * tile, 2 * hw), F32)],
        compiler_params=_cparams("arbitrary", "arbitrary"),
        name="diff_attn",
    )(q.reshape(bsz, seqlen, d_q), k.reshape(bsz, seqlen, d_q), v.reshape(bsz, seqlen, d_q),
      lq1.reshape(1, hdim), lk1.reshape(1, hdim), lq2.reshape(1, hdim), lk2.reshape(1, hdim),
      sub_norm.reshape(1, hw))
    return _proj_residual(o.reshape(t, d_q), w_out.astype(BF16), h)


_R_E1, _R_E2, _R_W1, _R_W2, _R_RANK1, _R_RANK2 = range(6)


def _router_kernel(h_ref, g_ref, w_ref, b_ref, rec_ref, cnt_ref, run_ref):
    tm = h_ref.shape[0]
    ne, ng, epg = MOE_EXPERTS, MOE_GROUPS, MOE_EXPERTS_PER_GROUP

    @pl.when(pl.program_id(0) == 0)
    def _():
        run_ref[...] = jnp.zeros(run_ref.shape, F32)

    u = _rms(h_ref[...], g_ref[...])
    logits = _dot_split(u, u.astype(BF16), w_ref) + b_ref[...]
    lane = lax.broadcasted_iota(jnp.int32, logits.shape, 1)
    lane_f = lane.astype(F32)
    lane_group = (lane // epg).astype(F32)
    neg = -jnp.inf
    big = float(V7X_LANES)

    gl = jnp.where((lane >= ne) & (lane < ne + ng), logits, neg)
    gmax = jnp.max(gl, axis=1, keepdims=True)
    gsel = jnp.min(jnp.where(gl == gmax, lane_f - ne, big), axis=1, keepdims=True)
    g_gate = 1.0 / jnp.sum(jnp.exp(gl - gmax), axis=1, keepdims=True)

    el = jnp.where((lane < ne) & (lane_group == gsel), logits, neg)
    m1 = jnp.max(el, axis=1, keepdims=True)
    i1 = jnp.min(jnp.where(el == m1, lane_f, big), axis=1, keepdims=True)
    el2 = jnp.where(lane_f == i1, neg, el)
    m2 = jnp.max(el2, axis=1, keepdims=True)
    i2 = jnp.min(jnp.where(el2 == m2, lane_f, big), axis=1, keepdims=True)
    e21 = jnp.exp(m2 - m1)
    w1 = g_gate / (1.0 + e21)
    w2 = g_gate * e21 / (1.0 + e21)

    oh1 = lane_f == i1
    oh2 = lane_f == i2
    oh = jnp.where(oh1 | oh2, 1.0, 0.0)
    row = lax.broadcasted_iota(jnp.int32, (tm, tm), 0)
    col = lax.broadcasted_iota(jnp.int32, (tm, tm), 1)
    before = _dot((col < row).astype(BF16), oh.astype(BF16))
    pos = run_ref[0:1, :] + before
    rank1 = jnp.sum(jnp.where(oh1, pos, 0.0), axis=1, keepdims=True)
    rank2 = jnp.sum(jnp.where(oh2, pos, 0.0), axis=1, keepdims=True)
    run_ref[...] = run_ref[...] + jnp.sum(oh, axis=0, keepdims=True)
    cnt_ref[...] = run_ref[...]

    rec = jnp.zeros(logits.shape, F32)
    for idx, val in ((_R_E1, i1), (_R_E2, i2), (_R_W1, w1), (_R_W2, w2),
                     (_R_RANK1, rank1), (_R_RANK2, rank2)):
        rec = jnp.where(lane == idx, val, rec)
    rec_ref[...] = rec


def _pack_bf16_pair(x):
    n = x.shape[1] // 2
    xr = x.astype(BF16).astype(F32)
    return pltpu.bitcast(xr[:, :n], jnp.uint32) | (pltpu.bitcast(xr[:, n:], jnp.uint32) >> 16)


def _unpack_bf16_pair(w):
    hi = pltpu.bitcast(w & jnp.uint32(0xFFFF0000), F32)
    lo = pltpu.bitcast(w << 16, F32)
    return jnp.concatenate([hi, lo], axis=1)


def _dispatch_kernel(dest_ref, h_ref, g_ref, rows_in_ref, rows_ref, u_ref, sem):
    del rows_in_ref
    tm = h_ref.shape[0]
    u_ref[...] = _pack_bf16_pair(_rms(h_ref[...], g_ref[...]))

    def row_copy(tok, slot):
        return pltpu.make_async_copy(u_ref.at[tok], rows_ref.at[dest_ref[2 * tok + slot]], sem)

    def issue(tok, _):
        row_copy(tok, 0).start(priority=0)
        row_copy(tok, 1).start(priority=1)
        return 0

    def drain(tok, _):
        row_copy(tok, 0).wait()
        row_copy(tok, 1).wait()
        return 0

    lax.fori_loop(0, tm, issue, 0, unroll=DMA_LOOP_UNROLL)
    lax.fori_loop(0, tm, drain, 0, unroll=DMA_LOOP_UNROLL)


def _expert_kernel(be_ref, nu_ref, nxt_ref, x_ref, wg_hbm, wu_hbm, wd_hbm, y_ref,
                   wgf_ref, wuf_ref, wdf_ref, wgb_ref, wub_ref, wdb_ref, sems, *, layer):
    i = pl.program_id(0)

    def weight_copies(e):
        return (pltpu.make_async_copy(wg_hbm.at[layer, e], wgf_ref, sems.at[0]),
                pltpu.make_async_copy(wu_hbm.at[layer, e], wuf_ref, sems.at[1]),
                pltpu.make_async_copy(wd_hbm.at[layer, e], wdf_ref, sems.at[2]))

    @pl.when(i < nu_ref[0])
    def _():
        e = be_ref[i]

        @pl.when(i == 0)
        def _():
            for c in weight_copies(e):
                c.start()

        @pl.when(jnp.logical_or(i == 0, e != be_ref[jnp.maximum(i - 1, 0)]))
        def _():
            for c in weight_copies(e):
                c.wait()
            wgb_ref[...] = wgf_ref[...].astype(BF16)
            wub_ref[...] = wuf_ref[...].astype(BF16)
            wdb_ref[...] = wdf_ref[...].astype(BF16)
            nxt = nxt_ref[i]

            @pl.when(nxt >= 0)
            def _():
                for c in weight_copies(nxt):
                    c.start()

        x = _unpack_bf16_pair(x_ref[...]).astype(BF16)
        gate = _dot(x, wgb_ref[...])
        up = _dot(x, wub_ref[...])
        hid = (gate * _sigmoid(gate) * up).astype(BF16)
        y_ref[...] = _pack_bf16_pair(_dot(hid, wdb_ref[...]))

    @pl.when(i >= nu_ref[0])
    def _():
        y_ref[...] = jnp.zeros(y_ref.shape, y_ref.dtype)


def _combine_kernel(dest_ref, h_ref, rec_ref, rows_ref, o_ref, y0_ref, y1_ref, sem):
    tm = h_ref.shape[0]

    def row_copy(tok, slot, buf):
        return pltpu.make_async_copy(rows_ref.at[dest_ref[2 * tok + slot]], buf.at[tok], sem)

    def issue(tok, _):
        row_copy(tok, 0, y0_ref).start(priority=0)
        row_copy(tok, 1, y1_ref).start(priority=1)
        return 0

    def drain(tok, _):
        row_copy(tok, 0, y0_ref).wait()
        row_copy(tok, 1, y1_ref).wait()
        return 0

    lax.fori_loop(0, tm, issue, 0, unroll=DMA_LOOP_UNROLL)
    lax.fori_loop(0, tm, drain, 0, unroll=DMA_LOOP_UNROLL)
    rec = rec_ref[...]
    w1 = rec[:, _R_W1:_R_W1 + 1]
    w2 = rec[:, _R_W2:_R_W2 + 1]
    o_ref[...] = h_ref[...] + (w1 * _unpack_bf16_pair(y0_ref[...]) + w2 * _unpack_bf16_pair(y1_ref[...]))


def _hier_moe(h, g_norm, w_group, b_group, w_expert, b_expert, layer, w_gate, w_up, w_down):
    t, d = h.shape
    ne, ng = MOE_EXPERTS, MOE_GROUPS
    ff = w_gate.shape[3]
    dp = d // 2
    tm = min(MOE_TOKEN_TILE, t)
    g2 = g_norm.reshape(1, d)
    lane_pad = V7X_LANES - ne - ng
    w_router = _split_weight(
        jnp.concatenate([w_expert, w_group, jnp.zeros((d, lane_pad), F32)], axis=1))
    b_router = jnp.concatenate([b_expert, b_group, jnp.zeros((lane_pad,), F32)]).reshape(1, V7X_LANES)
    rec, cnt = pl.pallas_call(
        _router_kernel,
        grid=(t // tm,),
        in_specs=[pl.BlockSpec((tm, d), lambda i: (i, 0)),
                  _const_spec((1, d)),
                  _const_spec((d, 2 * V7X_LANES)),
                  _const_spec((1, V7X_LANES))],
        out_specs=[pl.BlockSpec((tm, V7X_LANES), lambda i: (i, 0)),
                   _const_spec((V7X_SUBLANES, V7X_LANES))],
        out_shape=[jax.ShapeDtypeStruct((t, V7X_LANES), F32),
                   jax.ShapeDtypeStruct((V7X_SUBLANES, V7X_LANES), F32)],
        scratch_shapes=[pltpu.VMEM((V7X_SUBLANES, V7X_LANES), F32)],
        compiler_params=_cparams("arbitrary"),
        name="moe_router",
    )(h, g2, w_router, b_router)

    br = MOE_ROW_BLOCK
    n_blocks = -(-(2 * t + ne * (br - 1)) // br)
    n_rows = n_blocks * br
    counts = cnt[0, :ne].astype(jnp.int32)
    padded = (counts + br - 1) // br * br
    pad_end = jnp.cumsum(padded)
    pad_start = pad_end - padded
    eid = rec[:, _R_E1:_R_E2 + 1].astype(jnp.int32)
    rank = rec[:, _R_RANK1:_R_RANK2 + 1].astype(jnp.int32)
    experts = jnp.arange(ne, dtype=jnp.int32)
    start_of = jnp.sum(jnp.where(eid[:, :, None] == experts, pad_start, 0), axis=-1)
    dest = (start_of + rank).reshape(-1)
    block_row0 = jnp.arange(n_blocks, dtype=jnp.int32) * br
    block_expert = jnp.minimum(
        jnp.sum((pad_end[None, :] <= block_row0[:, None]).astype(jnp.int32), axis=1), ne - 1)
    n_used = (pad_end[-1:] // br).astype(jnp.int32)
    blocks = jnp.arange(n_blocks, dtype=jnp.int32)
    later_run = ((block_expert[None, :] != block_expert[:, None]) & (blocks[None, :] > blocks[:, None])
                 & (blocks[None, :] < n_used))
    next_expert = jnp.where(jnp.any(later_run, axis=1),
                            block_expert[jnp.argmax(later_run, axis=1)], -1).astype(jnp.int32)

    x_rows = pl.pallas_call(
        _dispatch_kernel,
        grid_spec=pltpu.PrefetchScalarGridSpec(
            num_scalar_prefetch=0,
            grid=(t // tm,),
            in_specs=[pl.BlockSpec((2 * tm,), lambda i: (i,), memory_space=pltpu.SMEM),
                      pl.BlockSpec((tm, d), lambda i: (i, 0)),
                      _const_spec((1, d)),
                      pl.BlockSpec(memory_space=pl.ANY)],
            out_specs=pl.BlockSpec(memory_space=pl.ANY),
            scratch_shapes=[pltpu.VMEM((tm, dp), jnp.uint32), pltpu.SemaphoreType.DMA(())]),
        out_shape=jax.ShapeDtypeStruct((n_rows, dp), jnp.uint32),
        input_output_aliases={3: 0},
        compiler_params=_cparams("arbitrary"),
        name="moe_dispatch",
    )(dest, h, g2, jnp.zeros((n_rows, dp), jnp.uint32))

    def blk(i, be, nu, nxt):
        return jnp.maximum(jnp.minimum(i, nu[0] - 1), 0)

    hbm = pl.BlockSpec(memory_space=pl.ANY)
    y_rows = pl.pallas_call(
        functools.partial(_expert_kernel, layer=layer),
        grid_spec=pltpu.PrefetchScalarGridSpec(
            num_scalar_prefetch=3,
            grid=(n_blocks,),
            in_specs=[pl.BlockSpec((br, dp), lambda i, be, nu, nxt: (blk(i, be, nu, nxt), 0)),
                      hbm, hbm, hbm],
            out_specs=pl.BlockSpec((br, dp), lambda i, be, nu, nxt: (i, 0)),
            scratch_shapes=[pltpu.VMEM((d, ff), F32), pltpu.VMEM((d, ff), F32), pltpu.VMEM((ff, d), F32),
                            pltpu.VMEM((d, ff), BF16), pltpu.VMEM((d, ff), BF16), pltpu.VMEM((ff, d), BF16),
                            pltpu.SemaphoreType.DMA((3,))]),
        out_shape=jax.ShapeDtypeStruct((n_rows, dp), jnp.uint32),
        compiler_params=_cparams("arbitrary"),
        name="moe_experts",
    )(block_expert, n_used, next_expert, x_rows, w_gate, w_up, w_down)

    return pl.pallas_call(
        _combine_kernel,
        grid_spec=pltpu.PrefetchScalarGridSpec(
            num_scalar_prefetch=0,
            grid=(t // tm,),
            in_specs=[pl.BlockSpec((2 * tm,), lambda i: (i,), memory_space=pltpu.SMEM),
                      pl.BlockSpec((tm, d), lambda i: (i, 0)),
                      pl.BlockSpec((tm, V7X_LANES), lambda i: (i, 0)),
                      pl.BlockSpec(memory_space=pl.ANY)],
            out_specs=pl.BlockSpec((tm, d), lambda i: (i, 0)),
            scratch_shapes=[pltpu.VMEM((tm, dp), jnp.uint32), pltpu.VMEM((tm, dp), jnp.uint32),
                            pltpu.SemaphoreType.DMA(())]),
        out_shape=jax.ShapeDtypeStruct((t, d), F32),
        compiler_params=_cparams("arbitrary"),
        name="moe_combine",
    )(dest, h, rec, y_rows)


def _rope_tables(positions):
    half = DIFF_HEAD_DIM // 2
    inv = ROPE_THETA ** (-jnp.arange(0, DIFF_HEAD_DIM, 2, dtype=F32) / DIFF_HEAD_DIM)
    ang = positions.astype(F32).reshape(-1, 1) * inv
    cos, sin = jnp.cos(ang), jnp.sin(ang)
    cos_t = jnp.concatenate([cos] * 4, axis=1)
    sin_t = jnp.concatenate([-sin, sin, -sin, sin], axis=1)
    assert cos_t.shape[1] == 4 * half
    return cos_t, sin_t


def kernel(x, positions, norm_mix, norm_ffn, ssd_w_in, ssd_conv_w, ssd_conv_b, ssd_dt_bias, ssd_a_log, ssd_d, ssd_norm, ssd_w_out, sb_w_in, sb_w_out, lru_w_in, lru_conv_w, lru_conv_b, lru_w_rg, lru_b_rg, lru_w_ig, lru_b_ig, lru_lambda, lru_w_out, diff_w_in, diff_q_norm, diff_k_norm, diff_lam_q1, diff_lam_k1, diff_lam_q2, diff_lam_k2, diff_sub_norm, diff_w_out, moe_w_group, moe_b_group, moe_w_expert, moe_b_expert, moe_w_gate, moe_w_up, moe_w_down):
    bsz, seqlen, d = x.shape
    depth = norm_mix.shape[0]
    h = x.reshape(bsz * seqlen, d)
    cos_t, sin_t = _rope_tables(positions)
    for i in range(depth):
        m, j = i % N_MIXERS, i // N_MIXERS
        if m == 0:
            h = _ssd_mixer(h, bsz, seqlen, norm_mix[i], ssd_w_in[j], ssd_conv_w[j], ssd_conv_b[j],
                           ssd_dt_bias[j], ssd_a_log[j], ssd_d[j], ssd_norm[j], ssd_w_out[j])
        elif m == 1:
            h = _sb_mixer(h, bsz, seqlen, norm_mix[i], sb_w_in[j], sb_w_out[j])
        elif m == 2:
            h = _lru_mixer(h, bsz, seqlen, norm_mix[i], lru_w_in[j], lru_conv_w[j], lru_conv_b[j],
                           lru_w_rg[j], lru_b_rg[j], lru_w_ig[j], lru_b_ig[j], lru_lambda[j],
                           lru_w_out[j])
        else:
            lambda_init = 0.8 - 0.6 * math.exp(-0.3 * i)
            h = _diff_mixer(h, bsz, seqlen, cos_t, sin_t, norm_mix[i], diff_w_in[j], diff_q_norm[j],
                            diff_k_norm[j], diff_lam_q1[j], diff_lam_k1[j], diff_lam_q2[j],
                            diff_lam_k2[j], diff_sub_norm[j], diff_w_out[j], lambda_init)
        h = _hier_moe(h, norm_ffn[i], moe_w_group[i], moe_b_group[i], moe_w_expert[i],
                      moe_b_expert[i], i, moe_w_gate, moe_w_up, moe_w_down)
    return h.reshape(bsz, seqlen, d)
```

```python
import functools
import math

import jax
import jax.numpy as jnp
from jax import lax
from jax.experimental import pallas as pl
from jax.experimental.pallas import tpu as pltpu

F32 = jnp.float32
BF16 = jnp.bfloat16
HIGHEST = lax.Precision.HIGHEST

RMS_EPS = 1e-6
_LOG2_E = math.log2(math.e)
N_MIXERS = 4

V7X_VMEM_BYTES = 64 * 1024 * 1024
V7X_LANES = 128
V7X_SUBLANES = 8
VMEM_LIMIT = (V7X_VMEM_BYTES * 7) // 8

SSD_HEAD_DIM = 64
SSD_GROUPS = 8
SSD_STATE = 128
SSD_CONV = 4
SSD_CHUNK = 128
SB_HEAD_DIM = 64
LRU_BLOCKS = 16
LRU_CONV = 4
LRU_C = 8.0
DIFF_HEAD_DIM = 64
ROPE_THETA = 10000.0
MOE_GROUPS = 4
MOE_EXPERTS_PER_GROUP = 8
MOE_EXPERTS = MOE_GROUPS * MOE_EXPERTS_PER_GROUP

ROW_TILE = 512
SSD_IN_TILE = 256
SSD_CONV_CHUNK = 512
ATTN_TILE = 256
SB_PAIRS_PER_STEP = 8
DIFF_HEADS_PER_STEP = 8
LRU_TIME_TILE = 64
LRU_PITCH_PAD = 4
LRU_BAND = 3 * V7X_LANES
LRU_SCAN_UNROLL = 4
MOE_TOKEN_TILE = 512
MOE_ROW_BLOCK = 256
MOE_CHUNK = V7X_SUBLANES
CONV_HALO = V7X_SUBLANES
DMA_LOOP_UNROLL = 8


def _cparams(*sem):
    return pltpu.CompilerParams(dimension_semantics=sem, vmem_limit_bytes=VMEM_LIMIT)


def _const_spec(shape):
    nd = len(shape)
    return pl.BlockSpec(shape, lambda *_: (0,) * nd)


def _rms(x, w):
    ms = jnp.mean(x * x, axis=-1, keepdims=True)
    return x * lax.rsqrt(ms + RMS_EPS) * w


def _sigmoid(x):
    return 1.0 / (1.0 + jnp.exp(-x))


def _softplus(x):
    return jnp.maximum(x, 0.0) + jnp.log(1.0 + jnp.exp(-jnp.abs(x)))


def _dot(a, b, **kw):
    return jnp.dot(a, b, preferred_element_type=F32, **kw)


def _dot_nt(a, b):
    return lax.dot_general(a, b, (((1,), (1,)), ((), ())), preferred_element_type=F32)


def _dot_tn(a, b):
    return lax.dot_general(a, b, (((0,), (0,)), ((), ())), preferred_element_type=F32)


def _split_bf16(x):
    hi = x.astype(BF16)
    return hi, (x - hi.astype(F32)).astype(BF16)


def _split_weight(w):
    return jnp.concatenate(_split_bf16(w), axis=1)


def _dot_split(x, x_hi, w2_ref):
    n = w2_ref.shape[1] // 2
    x_lo = (x - x_hi.astype(F32)).astype(BF16)
    t = _dot(x_hi, w2_ref[...])
    return t[:, :n] + t[:, n:] + _dot(x_lo, w2_ref[:, :n])


def _proj_residual_kernel(y_ref, w_ref, h_ref, o_ref):
    o_ref[...] = h_ref[...] + _dot(y_ref[...], w_ref[...])


def _proj_residual(y, w, h):
    t, k = y.shape
    d = w.shape[1]
    tm = ROW_TILE
    return pl.pallas_call(
        _proj_residual_kernel,
        grid=(t // tm,),
        in_specs=[pl.BlockSpec((tm, k), lambda i: (i, 0)),
                  _const_spec((k, d)),
                  pl.BlockSpec((tm, d), lambda i: (i, 0))],
        out_specs=pl.BlockSpec((tm, d), lambda i: (i, 0)),
        out_shape=jax.ShapeDtypeStruct((t, d), F32),
        compiler_params=_cparams("arbitrary"),
        name="proj_residual",
    )(y, w, h)


def _ssd_in_kernel(h_ref, g_ref, wz_ref, wx_ref, wdt_ref, cw_ref, cb_ref,
                   z_ref, xbc_ref, dt_ref, pad_ref, *, tiles_per_seq):
    i = pl.program_id(0)
    tm = h_ref.shape[0]
    u32 = _rms(h_ref[...], g_ref[...])
    u = u32.astype(BF16)
    z_ref[...] = _dot(u, wz_ref[...]).astype(z_ref.dtype)
    dt_ref[...] = _dot_split(u32, u, wdt_ref)

    @pl.when(i % tiles_per_seq == 0)
    def _():
        pad_ref[0:CONV_HALO, :] = jnp.zeros((CONV_HALO, pad_ref.shape[1]), F32)

    conv_dim = pad_ref.shape[1]
    for c0 in range(0, conv_dim, SSD_CONV_CHUNK):
        cs = slice(c0, c0 + SSD_CONV_CHUNK)
        pad_ref[CONV_HALO:CONV_HALO + tm, cs] = _dot(u, wx_ref[:, cs])
        acc = cb_ref[:, cs] + cw_ref[SSD_CONV - 1:SSD_CONV, cs] * pad_ref[CONV_HALO:CONV_HALO + tm, cs]
        for k in range(SSD_CONV - 1):
            off = CONV_HALO - (SSD_CONV - 1) + k
            acc = acc + cw_ref[k:k + 1, cs] * pad_ref[off:off + tm, cs]
        xbc_ref[:, cs] = (acc * _sigmoid(acc)).astype(xbc_ref.dtype)
        pad_ref[0:CONV_HALO, cs] = pad_ref[tm:tm + CONV_HALO, cs]


def _ssd_scan_kernel(xs_ref, b_ref, c_ref, z_ref, dt_ref, dtt_ref,
                     bias_ref, biast_ref, alog_ref, alogt_ref, dfull_ref, nw_ref,
                     o_ref, state_ref):
    q = xs_ref.shape[0]
    n_heads = dt_ref.shape[1]
    d_inner = xs_ref.shape[1]
    heads_per_group = n_heads // SSD_GROUPS
    gw = d_inner // SSD_GROUPS
    p = SSD_HEAD_DIM

    @pl.when(pl.program_id(1) == 0)
    def _():
        state_ref[...] = jnp.zeros(state_ref.shape, F32)

    dt = _softplus(dt_ref[...] + bias_ref[...])
    a = -jnp.exp(alog_ref[...]) * dt
    dtt = _softplus(dtt_ref[...] + biast_ref[...])
    at = -jnp.exp(alogt_ref[...]) * dtt

    row = lax.broadcasted_iota(jnp.int32, (q, q), 0)
    col = lax.broadcasted_iota(jnp.int32, (q, q), 1)
    lower = row >= col
    a_cum = _dot(lower.astype(F32), a, precision=HIGHEST)
    a_cumt = _dot(at, (row <= col).astype(F32), precision=HIGHEST)
    a_tot = a_cum[q - 1:q, :]

    hh = lax.broadcasted_iota(jnp.int32, (2 * n_heads, d_inner), 0)
    cc = lax.broadcasted_iota(jnp.int32, (2 * n_heads, d_inner), 1)
    expand = (cc // p == hh % n_heads).astype(BF16)
    per_head = jnp.concatenate(
        [dt, jnp.exp(a_cum), jnp.exp(a_tot - a_cum),
         jnp.broadcast_to(jnp.exp(a_tot), (V7X_SUBLANES, n_heads))], axis=0)
    per_channel = _dot(jnp.concatenate(_split_bf16(per_head), axis=1), expand)
    dt_full = per_channel[0:q]
    dec_out_full = per_channel[q:2 * q]
    dec_st_full = per_channel[2 * q:3 * q]
    chunk_dec_full = per_channel[3 * q:3 * q + 1]

    lane = lax.broadcasted_iota(jnp.int32, (1, gw), 1)
    for g in range(SSD_GROUPS):
        sl = slice(g * gw, (g + 1) * gw)
        bg = b_ref[:, g * SSD_STATE:(g + 1) * SSD_STATE]
        cg = c_ref[:, g * SSD_STATE:(g + 1) * SSD_STATE]
        xs_g = xs_ref[:, sl].astype(F32)
        xdt = xs_g * dt_full[:, sl]
        cb = _dot_nt(cg, bg)
        s_prev = state_ref[g]
        y = _dot(cg, s_prev.astype(BF16)) * dec_out_full[:, sl]
        for r in range(heads_per_group):
            hd = g * heads_per_group + r
            seg = a_cum[:, hd:hd + 1] - a_cumt[hd:hd + 1, :]
            m = jnp.where(lower, cb * jnp.exp(seg), 0.0).astype(BF16)
            xr = jnp.where((lane >= r * p) & (lane < (r + 1) * p), xdt, 0.0).astype(BF16)
            y = y + _dot(m, xr)
        xw = (xdt * dec_st_full[:, sl]).astype(BF16)
        state_ref[g] = s_prev * chunk_dec_full[:, sl] + _dot_tn(bg, xw)
        y = y + xs_g * dfull_ref[:, sl]
        zg = z_ref[:, sl].astype(F32)
        yz = y * (zg * _sigmoid(zg))
        o_ref[:, sl] = _rms(yz, nw_ref[:, sl]).astype(o_ref.dtype)


def _ssd_mixer(h, bsz, seqlen, g_norm, w_in, conv_w, conv_b, dt_bias, a_log, d_skip, norm_w, w_out):
    t, d = h.shape
    n_heads = dt_bias.shape[0]
    d_inner = n_heads * SSD_HEAD_DIM
    gn = SSD_GROUPS * SSD_STATE
    conv_dim = d_inner + 2 * gn
    wz = w_in[:, :d_inner].astype(BF16)
    wx = w_in[:, d_inner:d_inner + conv_dim].astype(BF16)
    wdt = _split_weight(w_in[:, d_inner + conv_dim:])
    tm = SSD_IN_TILE
    z, xbc, dt_raw = pl.pallas_call(
        functools.partial(_ssd_in_kernel, tiles_per_seq=seqlen // tm),
        grid=(t // tm,),
        in_specs=[pl.BlockSpec((tm, d), lambda i: (i, 0)),
                  _const_spec((1, d)),
                  _const_spec((d, d_inner)),
                  _const_spec((d, conv_dim)),
                  _const_spec((d, 2 * n_heads)),
                  _const_spec((SSD_CONV, conv_dim)),
                  _const_spec((1, conv_dim))],
        out_specs=[pl.BlockSpec((tm, d_inner), lambda i: (i, 0)),
                   pl.BlockSpec((tm, conv_dim), lambda i: (i, 0)),
                   pl.BlockSpec((tm, n_heads), lambda i: (i, 0))],
        out_shape=[jax.ShapeDtypeStruct((t, d_inner), BF16),
                   jax.ShapeDtypeStruct((t, conv_dim), BF16),
                   jax.ShapeDtypeStruct((t, n_heads), F32)],
        scratch_shapes=[pltpu.VMEM((tm + CONV_HALO, conv_dim), F32)],
        compiler_params=_cparams("arbitrary"),
        name="ssd_in",
    )(h, g_norm.reshape(1, d), wz, wx, wdt, conv_w, conv_b.reshape(1, conv_dim))

    q = SSD_CHUNK
    nc = seqlen // q
    xbc3 = xbc.reshape(bsz, seqlen, conv_dim)
    dt3 = dt_raw.reshape(bsz, seqlen, n_heads)
    dtt3 = jnp.swapaxes(dt3, 1, 2)
    nb_x = d_inner // gn
    yg = pl.pallas_call(
        _ssd_scan_kernel,
        grid=(bsz, nc),
        in_specs=[pl.BlockSpec((None, q, d_inner), lambda b, c: (b, c, 0)),
                  pl.BlockSpec((None, q, gn), lambda b, c: (b, c, nb_x)),
                  pl.BlockSpec((None, q, gn), lambda b, c: (b, c, nb_x + 1)),
                  pl.BlockSpec((None, q, d_inner), lambda b, c: (b, c, 0)),
                  pl.BlockSpec((None, q, n_heads), lambda b, c: (b, c, 0)),
                  pl.BlockSpec((None, n_heads, q), lambda b, c: (b, 0, c)),
                  _const_spec((1, n_heads)), _const_spec((n_heads, 1)),
                  _const_spec((1, n_heads)), _const_spec((n_heads, 1)),
                  _const_spec((1, d_inner)), _const_spec((1, d_inner))],
        out_specs=pl.BlockSpec((None, q, d_inner), lambda b, c: (b, c, 0)),
        out_shape=jax.ShapeDtypeStruct((bsz, seqlen, d_inner), BF16),
        scratch_shapes=[pltpu.VMEM((SSD_GROUPS, SSD_STATE, d_inner // SSD_GROUPS), F32)],
        compiler_params=_cparams("arbitrary", "arbitrary"),
        name="ssd_scan",
    )(xbc3, xbc3, xbc3, z.reshape(bsz, seqlen, d_inner), dt3, dtt3,
      dt_bias.reshape(1, n_heads), dt_bias.reshape(n_heads, 1),
      a_log.reshape(1, n_heads), a_log.reshape(n_heads, 1),
      jnp.repeat(d_skip, SSD_HEAD_DIM).reshape(1, d_inner), norm_w.reshape(1, d_inner))
    return _proj_residual(yg.reshape(t, d_inner), w_out.astype(BF16), h)


def _sb_in_kernel(h_ref, g_ref, w_ref, o_ref, *, q_cols, q_scale):
    u = _rms(h_ref[...], g_ref[...]).astype(BF16)
    o_ref[:, :q_cols] = (_dot(u, w_ref[:, :q_cols]) * q_scale).astype(o_ref.dtype)
    o_ref[:, q_cols:] = _dot(u, w_ref[:, q_cols:]).astype(o_ref.dtype)


def _sb_attn_kernel(q_ref, k_ref, v_ref, o_ref, acc_ref, *, tile):
    seqlen = q_ref.shape[0]
    nq = seqlen // tile
    pw = 2 * SB_HEAD_DIM
    n_pairs = q_ref.shape[1] // pw
    lane = lax.broadcasted_iota(jnp.int32, (1, pw), 1)
    first = lane < SB_HEAD_DIM
    row = lax.broadcasted_iota(jnp.int32, (tile, tile), 0)
    col = lax.broadcasted_iota(jnp.int32, (tile, tile), 1)
    neg_from = jnp.where(row >= col, -1.0, 0.0).astype(BF16)
    past = col < row
    past2 = jnp.concatenate([past, past], axis=0)

    def k_tile(qs, kj, csums, diagonal):
        rows = pl.ds(pl.multiple_of(kj * tile, tile), tile)
        out = []
        for p in range(n_pairs):
            k = k_ref[rows, p * pw:(p + 1) * pw]
            v = v_ref[rows, p * pw:(p + 1) * pw]
            t = _dot_nt(qs[p], k)
            neg_abs = pltpu.bitcast(pltpu.bitcast(t, jnp.uint32) | jnp.uint32(0x80000000), F32)
            sp = jnp.maximum(t, 0.0) + jnp.log(1.0 + jnp.exp2(neg_abs)) * _LOG2_E
            if diagonal:
                sp = jnp.where(past2, sp, 0.0)
            w = jnp.exp2(t + _dot(sp.astype(BF16), neg_from) + csums[p])
            if diagonal:
                w = jnp.where(past2, w, 0.0)
            acc_ref[p] += _dot(w.astype(BF16), v)
            out.append(csums[p] - jnp.sum(sp, axis=1, keepdims=True))
        return tuple(out)

    def q_body(qi, _):
        qrows = pl.ds(pl.multiple_of(qi * tile, tile), tile)
        qs = []
        for p in range(n_pairs):
            q = q_ref[qrows, p * pw:(p + 1) * pw]
            zero = jnp.zeros_like(q)
            qs.append(jnp.concatenate([jnp.where(first, q, zero), jnp.where(first, zero, q)], axis=0))
            acc_ref[p] = jnp.zeros(acc_ref.shape[1:], F32)
        init = tuple(jnp.zeros((2 * tile, 1), F32) for _ in range(n_pairs))
        csums = k_tile(qs, qi, init, True)
        lax.fori_loop(0, qi, lambda s, c: k_tile(qs, qi - 1 - s, c, False), csums)
        for p in range(n_pairs):
            o = jnp.where(first, acc_ref[p, 0:tile, :], acc_ref[p, tile:2 * tile, :])
            o_ref[qrows, p * pw:(p + 1) * pw] = o.astype(o_ref.dtype)
        return 0

    lax.fori_loop(0, nq, q_body, 0)


def _sb_mixer(h, bsz, seqlen, g_norm, w_in, w_out):
    t, d = h.shape
    hd = w_in.shape[1] // 3
    tm = ROW_TILE
    qkv = pl.pallas_call(
        functools.partial(_sb_in_kernel, q_cols=hd, q_scale=SB_HEAD_DIM ** -0.5 * _LOG2_E),
        grid=(t // tm,),
        in_specs=[pl.BlockSpec((tm, d), lambda i: (i, 0)),
                  _const_spec((1, d)),
                  _const_spec((d, 3 * hd))],
        out_specs=pl.BlockSpec((tm, 3 * hd), lambda i: (i, 0)),
        out_shape=jax.ShapeDtypeStruct((t, 3 * hd), BF16),
        compiler_params=_cparams("arbitrary"),
        name="sb_in",
    )(h, g_norm.reshape(1, d), w_in.astype(BF16))
    qkv3 = qkv.reshape(bsz, seqlen, 3 * hd)
    bw = SB_PAIRS_PER_STEP * 2 * SB_HEAD_DIM
    nb = hd // bw
    tile = min(ATTN_TILE, seqlen)
    o = pl.pallas_call(
        functools.partial(_sb_attn_kernel, tile=tile),
        grid=(bsz, nb),
        in_specs=[pl.BlockSpec((None, seqlen, bw), lambda b, j: (b, 0, j)),
                  pl.BlockSpec((None, seqlen, bw), lambda b, j: (b, 0, nb + j)),
                  pl.BlockSpec((None, seqlen, bw), lambda b, j: (b, 0, 2 * nb + j))],
        out_specs=pl.BlockSpec((None, seqlen, bw), lambda b, j: (b, 0, j)),
        out_shape=jax.ShapeDtypeStruct((bsz, seqlen, hd), BF16),
        scratch_shapes=[pltpu.VMEM((SB_PAIRS_PER_STEP, 2 * tile, 2 * SB_HEAD_DIM), F32)],
        compiler_params=_cparams("arbitrary", "arbitrary"),
        name="sb_attn",
    )(qkv3, qkv3, qkv3)
    return _proj_residual(o.reshape(t, hd), w_out.astype(BF16), h)


def _lru_kernel(h_ref, g_ref, win_ref, cw_ref, cb_ref, wband_ref, br_ref, bi_ref, lam_ref,
                y_ref, pad_ref, xc_ref, a_ref, b_ref, hs_ref, carry_ref, *, band_starts):
    bsz, tt, d = h_ref.shape
    width = y_ref.shape[2]

    @pl.when(pl.program_id(0) == 0)
    def _():
        carry_ref[...] = jnp.zeros(carry_ref.shape, F32)
        for b in range(bsz):
            pad_ref[b, 0:CONV_HALO, :] = jnp.zeros((CONV_HALO, width), F32)

    u = _rms(h_ref[...].reshape(bsz * tt, d), g_ref[...]).astype(BF16)
    proj = _dot(u, win_ref[...])
    gate_branch = proj[:, :width]
    for b in range(bsz):
        pad_ref[b, CONV_HALO:CONV_HALO + tt, :] = proj[b * tt:(b + 1) * tt, width:]
        acc = cb_ref[...] + cw_ref[LRU_CONV - 1:LRU_CONV, :] * pad_ref[b, CONV_HALO:CONV_HALO + tt, :]
        for k in range(LRU_CONV - 1):
            off = CONV_HALO - (LRU_CONV - 1) + k
            acc = acc + cw_ref[k:k + 1, :] * pad_ref[b, off:off + tt, :]
        xc_ref[b * tt:(b + 1) * tt, :] = acc
        pad_ref[b, 0:CONV_HALO, :] = pad_ref[b, tt:tt + CONV_HALO, :]

    xcb = xc_ref[...].astype(BF16)
    n_lt = width // V7X_LANES
    pitch = a_ref.shape[1] // bsz
    neg_c_softplus = -LRU_C * _softplus(-lam_ref[...])
    for c in range(n_lt):
        cs = slice(c * V7X_LANES, (c + 1) * V7X_LANES)
        pre = _dot(xcb[:, band_starts[c]:band_starts[c] + LRU_BAND], wband_ref[c])
        r = _sigmoid(pre[:, :V7X_LANES] + br_ref[:, cs])
        ig = _sigmoid(pre[:, V7X_LANES:] + bi_ref[:, cs])
        a = jnp.exp(neg_c_softplus[:, cs] * r)
        bterm = jnp.sqrt(1.0 - a * a) * (ig * xc_ref[:, cs])
        for b in range(bsz):
            a_ref[c, b * pitch:b * pitch + tt, :] = a[b * tt:(b + 1) * tt, :]
            b_ref[c, b * pitch:b * pitch + tt, :] = bterm[b * tt:(b + 1) * tt, :]

    def step(j, hprev):
        hnew = []
        for c in range(n_lt):
            rows_j = pl.ds(j, bsz, stride=pitch)
            hc = a_ref[c, rows_j, :] * hprev[c] + b_ref[c, rows_j, :]
            hs_ref[c, rows_j, :] = hc
            hnew.append(hc)
        return tuple(hnew)

    hlast = lax.fori_loop(0, tt, step, tuple(carry_ref[c] for c in range(n_lt)), unroll=LRU_SCAN_UNROLL)
    for c in range(n_lt):
        carry_ref[c] = hlast[c]
    c0 = math.sqrt(2.0 / math.pi)
    gelu = 0.5 * gate_branch * (1.0 + jnp.tanh(c0 * (gate_branch + 0.044715 * gate_branch ** 3)))
    for b in range(bsz):
        hs_b = jnp.concatenate([hs_ref[c, b * pitch:b * pitch + tt, :] for c in range(n_lt)], axis=1)
        y_ref[b] = (hs_b * gelu[b * tt:(b + 1) * tt, :]).astype(y_ref.dtype)


def _block_diag(w):
    nb, bd, _ = w.shape
    rows = [jnp.pad(w[b], ((0, 0), (b * bd, (nb - 1 - b) * bd))) for b in range(nb)]
    return jnp.concatenate(rows, axis=0)


def _lru_band_starts(width):
    bd = width // LRU_BLOCKS
    starts = []
    for j in range(width // V7X_LANES):
        b_lo = (j * V7X_LANES) // bd
        b_hi = ((j + 1) * V7X_LANES - 1) // bd
        lo = (b_lo * bd) // V7X_LANES * V7X_LANES
        assert (b_hi + 1) * bd - lo <= LRU_BAND
        starts.append(min(lo, width - LRU_BAND))
    return tuple(starts)


def _lru_band_weights(w_rg, w_ig, starts):
    dense_r, dense_i = _block_diag(w_rg), _block_diag(w_ig)
    tiles = []
    for j, s in enumerate(starts):
        cs = slice(j * V7X_LANES, (j + 1) * V7X_LANES)
        tiles.append(jnp.concatenate([dense_r[s:s + LRU_BAND, cs], dense_i[s:s + LRU_BAND, cs]], axis=1))
    return jnp.stack(tiles).astype(BF16)


def _lru_mixer(h, bsz, seqlen, g_norm, w_in, conv_w, conv_b, w_rg, b_rg, w_ig, b_ig, lam, w_out):
    t, d = h.shape
    width = lam.shape[0]
    tt = min(LRU_TIME_TILE, seqlen)
    rows = bsz * tt
    n_lt = width // V7X_LANES
    scan_rows = bsz * (tt + LRU_PITCH_PAD)
    starts = _lru_band_starts(width)
    y = pl.pallas_call(
        functools.partial(_lru_kernel, band_starts=starts),
        grid=(seqlen // tt,),
        in_specs=[pl.BlockSpec((bsz, tt, d), lambda i: (0, i, 0)),
                  _const_spec((1, d)),
                  _const_spec((d, 2 * width)),
                  _const_spec((LRU_CONV, width)), _const_spec((1, width)),
                  _const_spec((n_lt, LRU_BAND, 2 * V7X_LANES)),
                  _const_spec((1, width)), _const_spec((1, width)),
                  _const_spec((1, width))],
        out_specs=pl.BlockSpec((bsz, tt, width), lambda i: (0, i, 0)),
        out_shape=jax.ShapeDtypeStruct((bsz, seqlen, width), BF16),
        scratch_shapes=[pltpu.VMEM((bsz, tt + CONV_HALO, width), F32),
                        pltpu.VMEM((rows, width), F32),
                        pltpu.VMEM((n_lt, scan_rows, V7X_LANES), F32),
                        pltpu.VMEM((n_lt, scan_rows, V7X_LANES), F32),
                        pltpu.VMEM((n_lt, scan_rows, V7X_LANES), F32),
                        pltpu.VMEM((n_lt, bsz, V7X_LANES), F32)],
        compiler_params=_cparams("arbitrary"),
        name="lru",
    )(h.reshape(bsz, seqlen, d), g_norm.reshape(1, d), w_in.astype(BF16),
      conv_w, conv_b.reshape(1, width), _lru_band_weights(w_rg, w_ig, starts),
      b_rg.reshape(1, width), b_ig.reshape(1, width), lam.reshape(1, width))
    return _proj_residual(y.reshape(t, width), w_out.astype(BF16), h)


def _diff_in_kernel(h_ref, g_ref, w_ref, cos_ref, sin_ref, qn_ref, kn_ref, q_ref, k_ref, v_ref,
                    *, q_scale):
    d_q = q_ref.shape[1]
    hdim = DIFF_HEAD_DIM
    n_grp = d_q // hdim
    u = _rms(h_ref[...], g_ref[...]).astype(BF16)
    v_ref[...] = _dot(u, w_ref[:, 2 * d_q:]).astype(v_ref.dtype)

    ci = lax.broadcasted_iota(jnp.int32, (d_q, n_grp), 0)
    gi = lax.broadcasted_iota(jnp.int32, (d_q, n_grp), 1)
    gsum = (ci // hdim == gi).astype(BF16)
    gj = lax.broadcasted_iota(jnp.int32, (2 * n_grp, d_q), 0)
    cj = lax.broadcasted_iota(jnp.int32, (2 * n_grp, d_q), 1)
    gexp = (cj // hdim == gj % n_grp).astype(BF16)
    lane = lax.broadcasted_iota(jnp.int32, (1, d_q), 1)
    first_half = (lane % hdim) < (hdim // 2)
    reps = d_q // cos_ref.shape[1]
    cos_t = jnp.tile(cos_ref[...], (1, reps))
    sin_t = jnp.tile(sin_ref[...], (1, reps))

    def norm_rope(x, nw):
        sq = x * x
        sq_hi = sq.astype(BF16)
        sq_lo = (sq - sq_hi.astype(F32)).astype(BF16)
        ms = (_dot(sq_hi, gsum) + _dot(sq_lo, gsum)) * (1.0 / hdim)
        inv = lax.rsqrt(ms + RMS_EPS)
        inv_hi = inv.astype(BF16)
        inv_lo = (inv - inv_hi.astype(F32)).astype(BF16)
        inv_full = _dot(jnp.concatenate([inv_hi, inv_lo], axis=1), gexp)
        xn = x * inv_full * nw
        rot = jnp.where(first_half, pltpu.roll(xn, d_q - hdim // 2, 1), pltpu.roll(xn, hdim // 2, 1))
        return xn * cos_t + rot * sin_t

    q_ref[...] = (norm_rope(_dot(u, w_ref[:, :d_q]), qn_ref[...]) * q_scale).astype(q_ref.dtype)
    k_ref[...] = norm_rope(_dot(u, w_ref[:, d_q:2 * d_q]), kn_ref[...]).astype(k_ref.dtype)


def _diff_attn_kernel(q_ref, k_ref, v_ref, lq1_ref, lk1_ref, lq2_ref, lk2_ref, sn_ref, o_ref,
                      acc_ref, *, tile, lambda_init):
    seqlen = q_ref.shape[0]
    nq = seqlen // tile
    hdim = DIFF_HEAD_DIM
    hw = 2 * hdim
    n_heads = q_ref.shape[1] // hw
    lane = lax.broadcasted_iota(jnp.int32, (1, hw), 1)
    first = lane < hdim
    row = lax.broadcasted_iota(jnp.int32, (tile, tile), 0)
    col = lax.broadcasted_iota(jnp.int32, (tile, tile), 1)
    causal = col <= row
    causal2 = jnp.concatenate([causal, causal], axis=0)
    ones = jnp.ones((tile, hw), BF16)
    lam = (jnp.exp(jnp.sum(lq1_ref[...] * lk1_ref[...], axis=1, keepdims=True))
           - jnp.exp(jnp.sum(lq2_ref[...] * lk2_ref[...], axis=1, keepdims=True)) + lambda_init)

    def k_tile(qs, kj, maxes, diagonal):
        rows = pl.ds(pl.multiple_of(kj * tile, tile), tile)
        out = []
        for hd in range(n_heads):
            k = k_ref[rows, hd * hw:(hd + 1) * hw]
            v1 = jnp.concatenate([v_ref[rows, hd * hw:(hd + 1) * hw], ones], axis=1)
            s = _dot_nt(qs[hd], k)
            if diagonal:
                s = jnp.where(causal2, s, -jnp.inf)
            mx_new = jnp.maximum(maxes[hd], jnp.max(s, axis=1, keepdims=True))
            alpha = jnp.exp2(maxes[hd] - mx_new)
            pr = jnp.exp2(s - mx_new)
            acc_ref[hd] = acc_ref[hd] * alpha + _dot(pr.astype(BF16), v1)
            out.append(mx_new)
        return tuple(out)

    def q_body(qi, _):
        qrows = pl.ds(pl.multiple_of(qi * tile, tile), tile)
        qs = []
        for hd in range(n_heads):
            q = q_ref[qrows, hd * hw:(hd + 1) * hw]
            zero = jnp.zeros_like(q)
            qs.append(jnp.concatenate([jnp.where(first, q, zero), jnp.where(first, zero, q)], axis=0))
            acc_ref[hd] = jnp.zeros(acc_ref.shape[1:], F32)
        init = tuple(jnp.full((2 * tile, 1), -jnp.inf, F32) for _ in range(n_heads))
        maxes = k_tile(qs, qi, init, True)
        lax.fori_loop(0, qi, lambda s, c: k_tile(qs, qi - 1 - s, c, False), maxes)
        for hd in range(n_heads):
            p = acc_ref[hd, :, 0:hw] / acc_ref[hd, :, hw:2 * hw]
            o = p[0:tile] - lam * p[tile:2 * tile]
            o = _rms(o, sn_ref[...]) * (1.0 - lambda_init)
            o_ref[qrows, hd * hw:(hd + 1) * hw] = o.astype(o_ref.dtype)
        return 0

    lax.fori_loop(0, nq, q_body, 0)


def _diff_mixer(h, bsz, seqlen, cos_t, sin_t, g_norm, w_in, q_norm, k_norm, lq1, lk1, lq2, lk2,
                sub_norm, w_out, lambda_init):
    t, d = h.shape
    d_q = w_in.shape[1] // 3
    hdim = DIFF_HEAD_DIM
    tm = ROW_TILE
    tw = cos_t.shape[1]
    q, k, v = pl.pallas_call(
        functools.partial(_diff_in_kernel, q_scale=hdim ** -0.5 * _LOG2_E),
        grid=(t // tm,),
        in_specs=[pl.BlockSpec((tm, d), lambda i: (i, 0)),
                  _const_spec((1, d)),
                  _const_spec((d, 3 * d_q)),
                  pl.BlockSpec((tm, tw), lambda i: (i, 0)),
                  pl.BlockSpec((tm, tw), lambda i: (i, 0)),
                  _const_spec((1, d_q)), _const_spec((1, d_q))],
        out_specs=[pl.BlockSpec((tm, d_q), lambda i: (i, 0))] * 3,
        out_shape=[jax.ShapeDtypeStruct((t, d_q), BF16)] * 3,
        compiler_params=_cparams("arbitrary"),
        name="diff_in",
    )(h, g_norm.reshape(1, d), w_in.astype(BF16), cos_t, sin_t,
      jnp.tile(q_norm, d_q // hdim).reshape(1, d_q), jnp.tile(k_norm, d_q // hdim).reshape(1, d_q))
    hw = 2 * hdim
    bw = DIFF_HEADS_PER_STEP * hw
    tile = min(ATTN_TILE, seqlen)
    spec = pl.BlockSpec((None, seqlen, bw), lambda b, j: (b, 0, j))
    vec = _const_spec((1, hdim))
    o = pl.pallas_call(
        functools.partial(_diff_attn_kernel, tile=tile, lambda_init=lambda_init),
        grid=(bsz, d_q // bw),
        in_specs=[spec, spec, spec, vec, vec, vec, vec, _const_spec((1, hw))],
        out_specs=spec,
        out_shape=jax.ShapeDtypeStruct((bsz, seqlen, d_q), BF16),
        scratch_shapes=[pltpu.VMEM((DIFF_HEADS_PER_STEP, 2 * tile, 2 * hw), F32)],
        compiler_params=_cparams("arbitrary", "arbitrary"),
        name="diff_attn",
    )(q.reshape(bsz, seqlen, d_q), k.reshape(bsz, seqlen, d_q), v.reshape(bsz, seqlen, d_q),
      lq1.reshape(1, hdim), lk1.reshape(1, hdim), lq2.reshape(1, hdim), lk2.reshape(1, hdim),
      sub_norm.reshape(1, hw))
    return _proj_residual(o.reshape(t, d_q), w_out.astype(BF16), h)


_R_E1, _R_E2, _R_W1, _R_W2, _R_RANK1, _R_RANK2 = range(6)
_C_LOC1, _C_LOC2, _C_W1, _C_W2 = range(4)


def _router_kernel(h_ref, g_ref, w_ref, b_ref, rec_ref, base_ref, cnt_ref, run_ref):
    tm = h_ref.shape[0]
    ne, ng, epg = MOE_EXPERTS, MOE_GROUPS, MOE_EXPERTS_PER_GROUP

    @pl.when(pl.program_id(0) == 0)
    def _():
        run_ref[...] = jnp.zeros(run_ref.shape, F32)

    base_ref[...] = run_ref[...]

    u = _rms(h_ref[...], g_ref[...])
    logits = _dot_split(u, u.astype(BF16), w_ref) + b_ref[...]
    lane = lax.broadcasted_iota(jnp.int32, logits.shape, 1)
    lane_f = lane.astype(F32)
    lane_group = (lane // epg).astype(F32)
    neg = -jnp.inf
    big = float(V7X_LANES)

    gl = jnp.where((lane >= ne) & (lane < ne + ng), logits, neg)
    gmax = jnp.max(gl, axis=1, keepdims=True)
    gsel = jnp.min(jnp.where(gl == gmax, lane_f - ne, big), axis=1, keepdims=True)
    g_gate = 1.0 / jnp.sum(jnp.exp(gl - gmax), axis=1, keepdims=True)

    el = jnp.where((lane < ne) & (lane_group == gsel), logits, neg)
    m1 = jnp.max(el, axis=1, keepdims=True)
    i1 = jnp.min(jnp.where(el == m1, lane_f, big), axis=1, keepdims=True)
    el2 = jnp.where(lane_f == i1, neg, el)
    m2 = jnp.max(el2, axis=1, keepdims=True)
    i2 = jnp.min(jnp.where(el2 == m2, lane_f, big), axis=1, keepdims=True)
    e21 = jnp.exp(m2 - m1)
    w1 = g_gate / (1.0 + e21)
    w2 = g_gate * e21 / (1.0 + e21)

    oh1 = lane_f == i1
    oh2 = lane_f == i2
    oh = jnp.where(oh1 | oh2, 1.0, 0.0)
    row = lax.broadcasted_iota(jnp.int32, (tm, tm), 0)
    col = lax.broadcasted_iota(jnp.int32, (tm, tm), 1)
    before = _dot((col < row).astype(BF16), oh.astype(BF16))
    pos = run_ref[0:1, :] + before
    rank1 = jnp.sum(jnp.where(oh1, pos, 0.0), axis=1, keepdims=True)
    rank2 = jnp.sum(jnp.where(oh2, pos, 0.0), axis=1, keepdims=True)
    run_ref[...] = run_ref[...] + jnp.sum(oh, axis=0, keepdims=True)
    cnt_ref[...] = run_ref[...]

    rec = jnp.zeros(logits.shape, F32)
    for idx, val in ((_R_E1, i1), (_R_E2, i2), (_R_W1, w1), (_R_W2, w2),
                     (_R_RANK1, rank1), (_R_RANK2, rank2)):
        rec = jnp.where(lane == idx, val, rec)
    rec_ref[...] = rec


def _pack_bf16_pair(x):
    n = x.shape[1] // 2
    xr = x.astype(BF16).astype(F32)
    return pltpu.bitcast(xr[:, :n], jnp.uint32) | (pltpu.bitcast(xr[:, n:], jnp.uint32) >> 16)


def _unpack_bf16_pair(w):
    hi = pltpu.bitcast(w & jnp.uint32(0xFFFF0000), F32)
    lo = pltpu.bitcast(w << 16, F32)
    return jnp.concatenate([hi, lo], axis=1)


def _segment_copies(seg_local_ref, seg_chunks_ref, seg_global_ref, tile_chunks_ref, copy_of):
    i = pl.program_id(0)

    def rows(chunk):
        return pl.ds(pl.multiple_of(chunk * MOE_CHUNK, MOE_CHUNK), MOE_CHUNK)

    def per_expert(e, _):
        seg = i * MOE_EXPERTS + e

        def per_chunk(c, _):
            copy_of(rows(seg_local_ref[seg] + c), rows(seg_global_ref[seg] + c)).start()
            return 0

        lax.fori_loop(0, seg_chunks_ref[seg], per_chunk, 0)
        return 0

    lax.fori_loop(0, MOE_EXPERTS, per_expert, 0)

    def drain(c, _):
        copy_of(rows(0), rows(0)).wait()
        return 0

    lax.fori_loop(0, tile_chunks_ref[i], drain, 0)


def _dispatch_kernel(seg_local_ref, seg_chunks_ref, seg_global_ref, tile_chunks_ref,
                     h_ref, g_ref, loct_ref, rows_in_ref, rows_ref, sorted_ref, sem):
    del rows_in_ref
    tm = h_ref.shape[0]
    u = _rms(h_ref[...], g_ref[...]).astype(BF16)
    local_row = lax.broadcasted_iota(jnp.int32, (sorted_ref.shape[0], tm), 0)
    loc = loct_ref[...]
    pick = jnp.where(local_row == loc[0:1, :], 1.0, jnp.where(local_row == loc[1:2, :], 1.0, 0.0))
    sorted_ref[...] = _pack_bf16_pair(_dot(pick.astype(BF16), u))
    _segment_copies(seg_local_ref, seg_chunks_ref, seg_global_ref, tile_chunks_ref,
                    lambda lr, gr: pltpu.make_async_copy(sorted_ref.at[lr], rows_ref.at[gr], sem))


def _expert_kernel(be_ref, nu_ref, nxt_ref, x_ref, wg_hbm, wu_hbm, wd_hbm, y_ref,
                   wgf_ref, wuf_ref, wdf_ref, wgb_ref, wub_ref, wdb_ref, sems, *, layer):
    i = pl.program_id(0)

    def weight_copies(e):
        return (pltpu.make_async_copy(wg_hbm.at[layer, e], wgf_ref, sems.at[0]),
                pltpu.make_async_copy(wu_hbm.at[layer, e], wuf_ref, sems.at[1]),
                pltpu.make_async_copy(wd_hbm.at[layer, e], wdf_ref, sems.at[2]))

    @pl.when(i < nu_ref[0])
    def _():
        e = be_ref[i]

        @pl.when(i == 0)
        def _():
            for c in weight_copies(e):
                c.start()

        @pl.when(jnp.logical_or(i == 0, e != be_ref[jnp.maximum(i - 1, 0)]))
        def _():
            for c in weight_copies(e):
                c.wait()
            wgb_ref[...] = wgf_ref[...].astype(BF16)
            wub_ref[...] = wuf_ref[...].astype(BF16)
            wdb_ref[...] = wdf_ref[...].astype(BF16)
            nxt = nxt_ref[i]

            @pl.when(nxt >= 0)
            def _():
                for c in weight_copies(nxt):
                    c.start()

        x = _unpack_bf16_pair(x_ref[...]).astype(BF16)
        gate = _dot(x, wgb_ref[...])
        up = _dot(x, wub_ref[...])
        hid = (gate * _sigmoid(gate) * up).astype(BF16)
        y_ref[...] = _pack_bf16_pair(_dot(hid, wdb_ref[...]))

    @pl.when(i >= nu_ref[0])
    def _():
        y_ref[...] = jnp.zeros(y_ref.shape, y_ref.dtype)


def _combine_kernel(seg_local_ref, seg_chunks_ref, seg_global_ref, tile_chunks_ref,
                    h_ref, crec_ref, rows_ref, o_ref, sorted_ref, sem):
    tm = h_ref.shape[0]
    sorted_ref[...] = jnp.zeros(sorted_ref.shape, sorted_ref.dtype)
    _segment_copies(seg_local_ref, seg_chunks_ref, seg_global_ref, tile_chunks_ref,
                    lambda lr, gr: pltpu.make_async_copy(rows_ref.at[gr], sorted_ref.at[lr], sem))
    crec = crec_ref[...]
    loc0 = crec[:, _C_LOC1:_C_LOC1 + 1].astype(jnp.int32)
    loc1 = crec[:, _C_LOC2:_C_LOC2 + 1].astype(jnp.int32)
    w1 = crec[:, _C_W1:_C_W1 + 1]
    w2 = crec[:, _C_W2:_C_W2 + 1]
    local_row = lax.broadcasted_iota(jnp.int32, (tm, sorted_ref.shape[0]), 1)
    weights = jnp.where(local_row == loc0, w1, jnp.where(local_row == loc1, w2, 0.0)).astype(BF16)
    y = _unpack_bf16_pair(sorted_ref[...]).astype(BF16)
    o_ref[...] = h_ref[...] + _dot(weights, y)


def _hier_moe(h, g_norm, w_group, b_group, w_expert, b_expert, layer, w_gate, w_up, w_down):
    t, d = h.shape
    ne, ng = MOE_EXPERTS, MOE_GROUPS
    ff = w_gate.shape[3]
    dp = d // 2
    tm = min(MOE_TOKEN_TILE, t)
    g2 = g_norm.reshape(1, d)
    lane_pad = V7X_LANES - ne - ng
    w_router = _split_weight(
        jnp.concatenate([w_expert, w_group, jnp.zeros((d, lane_pad), F32)], axis=1))
    b_router = jnp.concatenate([b_expert, b_group, jnp.zeros((lane_pad,), F32)]).reshape(1, V7X_LANES)
    nt = t // tm
    rec, base, cnt = pl.pallas_call(
        _router_kernel,
        grid=(nt,),
        in_specs=[pl.BlockSpec((tm, d), lambda i: (i, 0)),
                  _const_spec((1, d)),
                  _const_spec((d, 2 * V7X_LANES)),
                  _const_spec((1, V7X_LANES))],
        out_specs=[pl.BlockSpec((tm, V7X_LANES), lambda i: (i, 0)),
                   pl.BlockSpec((V7X_SUBLANES, V7X_LANES), lambda i: (i, 0)),
                   _const_spec((V7X_SUBLANES, V7X_LANES))],
        out_shape=[jax.ShapeDtypeStruct((t, V7X_LANES), F32),
                   jax.ShapeDtypeStruct((nt * V7X_SUBLANES, V7X_LANES), F32),
                   jax.ShapeDtypeStruct((V7X_SUBLANES, V7X_LANES), F32)],
        scratch_shapes=[pltpu.VMEM((V7X_SUBLANES, V7X_LANES), F32)],
        compiler_params=_cparams("arbitrary"),
        name="moe_router",
    )(h, g2, w_router, b_router)

    br, ck = MOE_ROW_BLOCK, MOE_CHUNK
    n_blocks = -(-(2 * t + nt * ne * (ck - 1) + ne * (br - 1)) // br)
    n_rows = n_blocks * br
    local_rows = 2 * tm + ne * ck
    before = base.reshape(nt, V7X_SUBLANES, V7X_LANES)[:, 0, :ne].astype(jnp.int32)
    total = cnt[0, :ne].astype(jnp.int32)
    tile_cnt = jnp.concatenate([before[1:], total[None, :]], axis=0) - before
    seg = (tile_cnt + ck - 1) // ck * ck
    region = jnp.sum(seg, axis=0)
    padded = (region + br - 1) // br * br
    pad_end = jnp.cumsum(padded)
    pad_start = pad_end - padded
    seg_global = pad_start[None, :] + jnp.cumsum(seg, axis=0) - seg
    seg_local = jnp.cumsum(seg, axis=1) - seg
    eid = rec[:, _R_E1:_R_E2 + 1].astype(jnp.int32)
    rank = rec[:, _R_RANK1:_R_RANK2 + 1].astype(jnp.int32)
    experts = jnp.arange(ne, dtype=jnp.int32)
    shift = jnp.repeat(seg_local - before, tm, axis=0)
    loc = rank + jnp.sum(jnp.where(eid[:, :, None] == experts, shift[:, None, :], 0), axis=-1)
    loct = jnp.concatenate([jnp.swapaxes(loc.reshape(nt, tm, 2), 1, 2),
                            jnp.full((nt, V7X_SUBLANES - 2, tm), -1, jnp.int32)], axis=1)
    crec = jnp.concatenate([loc.astype(F32), rec[:, _R_W1:_R_W2 + 1],
                            jnp.zeros((t, V7X_SUBLANES - 4), F32)], axis=1)
    seg_args = ((seg_local // ck).reshape(-1), (seg // ck).reshape(-1), (seg_global // ck).reshape(-1),
                jnp.sum(seg // ck, axis=1))
    block_row0 = jnp.arange(n_blocks, dtype=jnp.int32) * br
    block_expert = jnp.minimum(
        jnp.sum((pad_end[None, :] <= block_row0[:, None]).astype(jnp.int32), axis=1), ne - 1)
    n_used = (pad_end[-1:] // br).astype(jnp.int32)
    blocks = jnp.arange(n_blocks, dtype=jnp.int32)
    later_run = ((block_expert[None, :] != block_expert[:, None]) & (blocks[None, :] > blocks[:, None])
                 & (blocks[None, :] < n_used))
    next_expert = jnp.where(jnp.any(later_run, axis=1),
                            block_expert[jnp.argmax(later_run, axis=1)], -1).astype(jnp.int32)

    x_rows = pl.pallas_call(
        _dispatch_kernel,
        grid_spec=pltpu.PrefetchScalarGridSpec(
            num_scalar_prefetch=4,
            grid=(nt,),
            in_specs=[pl.BlockSpec((tm, d), lambda i, *_: (i, 0)),
                      _const_spec((1, d)),
                      pl.BlockSpec((None, V7X_SUBLANES, tm), lambda i, *_: (i, 0, 0)),
                      pl.BlockSpec(memory_space=pl.ANY)],
            out_specs=pl.BlockSpec(memory_space=pl.ANY),
            scratch_shapes=[pltpu.VMEM((local_rows, dp), jnp.uint32), pltpu.SemaphoreType.DMA(())]),
        out_shape=jax.ShapeDtypeStruct((n_rows, dp), jnp.uint32),
        input_output_aliases={7: 0},
        compiler_params=_cparams("arbitrary"),
        name="moe_dispatch",
    )(*seg_args, h, g2, loct, jnp.zeros((n_rows, dp), jnp.uint32))

    def blk(i, be, nu, nxt):
        return jnp.maximum(jnp.minimum(i, nu[0] - 1), 0)

    hbm = pl.BlockSpec(memory_space=pl.ANY)
    y_rows = pl.pallas_call(
        functools.partial(_expert_kernel, layer=layer),
        grid_spec=pltpu.PrefetchScalarGridSpec(
            num_scalar_prefetch=3,
            grid=(n_blocks,),
            in_specs=[pl.BlockSpec((br, dp), lambda i, be, nu, nxt: (blk(i, be, nu, nxt), 0)),
                      hbm, hbm, hbm],
            out_specs=pl.BlockSpec((br, dp), lambda i, be, nu, nxt: (i, 0)),
            scratch_shapes=[pltpu.VMEM((d, ff), F32), pltpu.VMEM((d, ff), F32), pltpu.VMEM((ff, d), F32),
                            pltpu.VMEM((d, ff), BF16), pltpu.VMEM((d, ff), BF16), pltpu.VMEM((ff, d), BF16),
                            pltpu.SemaphoreType.DMA((3,))]),
        out_shape=jax.ShapeDtypeStruct((n_rows, dp), jnp.uint32),
        compiler_params=_cparams("arbitrary"),
        name="moe_experts",
    )(block_expert, n_used, next_expert, x_rows, w_gate, w_up, w_down)

    return pl.pallas_call(
        _combine_kernel,
        grid_spec=pltpu.PrefetchScalarGridSpec(
            num_scalar_prefetch=4,
            grid=(nt,),
            in_specs=[pl.BlockSpec((tm, d), lambda i, *_: (i, 0)),
                      pl.BlockSpec((tm, V7X_SUBLANES), lambda i, *_: (i, 0)),
                      pl.BlockSpec(memory_space=pl.ANY)],
            out_specs=pl.BlockSpec((tm, d), lambda i, *_: (i, 0)),
            scratch_shapes=[pltpu.VMEM((local_rows, dp), jnp.uint32), pltpu.SemaphoreType.DMA(())]),
        out_shape=jax.ShapeDtypeStruct((t, d), F32),
        compiler_params=_cparams("arbitrary"),
        name="moe_combine",
    )(*seg_args, h, crec, y_rows)


def _rope_tables(positions):
    half = DIFF_HEAD_DIM // 2
    inv = ROPE_THETA ** (-jnp.arange(0, DIFF_HEAD_DIM, 2, dtype=F32) / DIFF_HEAD_DIM)
    ang = positions.astype(F32).reshape(-1, 1) * inv
    cos, sin = jnp.cos(ang), jnp.sin(ang)
    cos_t = jnp.concatenate([cos] * 4, axis=1)
    sin_t = jnp.concatenate([-sin, sin, -sin, sin], axis=1)
    assert cos_t.shape[1] == 4 * half
    return cos_t, sin_t


def kernel(x, positions, norm_mix, norm_ffn, ssd_w_in, ssd_conv_w, ssd_conv_b, ssd_dt_bias, ssd_a_log, ssd_d, ssd_norm, ssd_w_out, sb_w_in, sb_w_out, lru_w_in, lru_conv_w, lru_conv_b, lru_w_rg, lru_b_rg, lru_w_ig, lru_b_ig, lru_lambda, lru_w_out, diff_w_in, diff_q_norm, diff_k_norm, diff_lam_q1, diff_lam_k1, diff_lam_q2, diff_lam_k2, diff_sub_norm, diff_w_out, moe_w_group, moe_b_group, moe_w_expert, moe_b_expert, moe_w_gate, moe_w_up, moe_w_down):
    bsz, seqlen, d = x.shape
    depth = norm_mix.shape[0]
    h = x.reshape(bsz * seqlen, d)
    cos_t, sin_t = _rope_tables(positions)
    for i in range(depth):
        m, j = i % N_MIXERS, i // N_MIXERS
        if m == 0:
            h = _ssd_mixer(h, bsz, seqlen, norm_mix[i], ssd_w_in[j], ssd_conv_w[j], ssd_conv_b[j],
                           ssd_dt_bias[j], ssd_a_log[j], ssd_d[j], ssd_norm[j], ssd_w_out[j])
        elif m == 1:
            h = _sb_mixer(h, bsz, seqlen, norm_mix[i], sb_w_in[j], sb_w_out[j])
        elif m == 2:
            h = _lru_mixer(h, bsz, seqlen, norm_mix[i], lru_w_in[j], lru_conv_w[j], lru_conv_b[j],
                           lru_w_rg[j], lru_b_rg[j], lru_w_ig[j], lru_b_ig[j], lru_lambda[j],
                           lru_w_out[j])
        else:
            lambda_init = 0.8 - 0.6 * math.exp(-0.3 * i)
            h = _diff_mixer(h, bsz, seqlen, cos_t, sin_t, norm_mix[i], diff_w_in[j], diff_q_norm[j],
                            diff_k_norm[j], diff_lam_q1[j], diff_lam_k1[j], diff_lam_q2[j],
                            diff_lam_k2[j], diff_sub_norm[j], diff_w_out[j], lambda_init)
        h = _hier_moe(h, norm_ffn[i], moe_w_group[i], moe_b_group[i], moe_w_expert[i],
                      moe_b_expert[i], i, moe_w_gate, moe_w_up, moe_w_down)
    return h.reshape(bsz, seqlen, d)
```

```python
import functools
import math

import jax
import jax.numpy as jnp
from jax import lax
from jax.experimental import pallas as pl
from jax.experimental.pallas import tpu as pltpu

F32 = jnp.float32
BF16 = jnp.bfloat16
HIGHEST = lax.Precision.HIGHEST

RMS_EPS = 1e-6
_LOG2_E = math.log2(math.e)
N_MIXERS = 4

V7X_VMEM_BYTES = 64 * 1024 * 1024
V7X_LANES = 128
V7X_SUBLANES = 8
VMEM_LIMIT = (V7X_VMEM_BYTES * 7) // 8

SSD_HEAD_DIM = 64
SSD_GROUPS = 8
SSD_STATE = 128
SSD_CONV = 4
SSD_CHUNK = 128
SB_HEAD_DIM = 64
LRU_BLOCKS = 16
LRU_CONV = 4
LRU_C = 8.0
DIFF_HEAD_DIM = 64
ROPE_THETA = 10000.0
MOE_GROUPS = 4
MOE_EXPERTS_PER_GROUP = 8
MOE_EXPERTS = MOE_GROUPS * MOE_EXPERTS_PER_GROUP

ROW_TILE = 512
SSD_IN_TILE = 256
SSD_CONV_CHUNK = 512
ATTN_TILE = 256
SB_PAIRS_PER_STEP = 8
DIFF_HEADS_PER_STEP = 8
LRU_TIME_TILE = 64
LRU_PITCH_PAD = 4
LRU_BAND = 3 * V7X_LANES
LRU_SCAN_UNROLL = 4
MOE_TOKEN_TILE = 512
MOE_ROW_BLOCK = 256
MOE_CHUNK = V7X_SUBLANES
CONV_HALO = V7X_SUBLANES
DMA_LOOP_UNROLL = 8


def _cparams(*sem):
    return pltpu.CompilerParams(dimension_semantics=sem, vmem_limit_bytes=VMEM_LIMIT)


def _const_spec(shape):
    nd = len(shape)
    return pl.BlockSpec(shape, lambda *_: (0,) * nd)


def _rms(x, w):
    ms = jnp.mean(x * x, axis=-1, keepdims=True)
    return x * lax.rsqrt(ms + RMS_EPS) * w


def _sigmoid(x):
    return 1.0 / (1.0 + jnp.exp(-x))


def _softplus(x):
    return jnp.maximum(x, 0.0) + jnp.log(1.0 + jnp.exp(-jnp.abs(x)))


def _dot(a, b, **kw):
    return jnp.dot(a, b, preferred_element_type=F32, **kw)


def _dot_nt(a, b):
    return lax.dot_general(a, b, (((1,), (1,)), ((), ())), preferred_element_type=F32)


def _dot_tn(a, b):
    return lax.dot_general(a, b, (((0,), (0,)), ((), ())), preferred_element_type=F32)


def _split_bf16(x):
    hi = x.astype(BF16)
    return hi, (x - hi.astype(F32)).astype(BF16)


def _split_weight(w):
    return jnp.concatenate(_split_bf16(w), axis=1)


def _dot_split(x, x_hi, w2_ref):
    n = w2_ref.shape[1] // 2
    x_lo = (x - x_hi.astype(F32)).astype(BF16)
    t = _dot(x_hi, w2_ref[...])
    return t[:, :n] + t[:, n:] + _dot(x_lo, w2_ref[:, :n])


def _proj_residual_kernel(y_ref, w_ref, h_ref, o_ref):
    o_ref[...] = h_ref[...] + _dot(y_ref[...], w_ref[...])


def _proj_residual(y, w, h):
    t, k = y.shape
    d = w.shape[1]
    tm = ROW_TILE
    return pl.pallas_call(
        _proj_residual_kernel,
        grid=(t // tm,),
        in_specs=[pl.BlockSpec((tm, k), lambda i: (i, 0)),
                  _const_spec((k, d)),
                  pl.BlockSpec((tm, d), lambda i: (i, 0))],
        out_specs=pl.BlockSpec((tm, d), lambda i: (i, 0)),
        out_shape=jax.ShapeDtypeStruct((t, d), F32),
        compiler_params=_cparams("arbitrary"),
        name="proj_residual",
    )(y, w, h)


def _ssd_in_kernel(h_ref, g_ref, wz_ref, wx_ref, wdt_ref, cw_ref, cb_ref,
                   z_ref, xbc_ref, dt_ref, pad_ref, *, tiles_per_seq):
    i = pl.program_id(0)
    tm = h_ref.shape[0]
    u32 = _rms(h_ref[...], g_ref[...])
    u = u32.astype(BF16)
    z_ref[...] = _dot(u, wz_ref[...]).astype(z_ref.dtype)
    dt_ref[...] = _dot_split(u32, u, wdt_ref)

    @pl.when(i % tiles_per_seq == 0)
    def _():
        pad_ref[0:CONV_HALO, :] = jnp.zeros((CONV_HALO, pad_ref.shape[1]), F32)

    conv_dim = pad_ref.shape[1]
    for c0 in range(0, conv_dim, SSD_CONV_CHUNK):
        cs = slice(c0, c0 + SSD_CONV_CHUNK)
        pad_ref[CONV_HALO:CONV_HALO + tm, cs] = _dot(u, wx_ref[:, cs])
        acc = cb_ref[:, cs] + cw_ref[SSD_CONV - 1:SSD_CONV, cs] * pad_ref[CONV_HALO:CONV_HALO + tm, cs]
        for k in range(SSD_CONV - 1):
            off = CONV_HALO - (SSD_CONV - 1) + k
            acc = acc + cw_ref[k:k + 1, cs] * pad_ref[off:off + tm, cs]
        xbc_ref[:, cs] = (acc * _sigmoid(acc)).astype(xbc_ref.dtype)
        pad_ref[0:CONV_HALO, cs] = pad_ref[tm:tm + CONV_HALO, cs]


def _ssd_scan_kernel(xs_ref, b_ref, c_ref, z_ref, dt_ref, dtt_ref,
                     bias_ref, biast_ref, alog_ref, alogt_ref, dfull_ref, nw_ref,
                     o_ref, state_ref):
    q = xs_ref.shape[0]
    n_heads = dt_ref.shape[1]
    d_inner = xs_ref.shape[1]
    heads_per_group = n_heads // SSD_GROUPS
    gw = d_inner // SSD_GROUPS
    p = SSD_HEAD_DIM

    @pl.when(pl.program_id(1) == 0)
    def _():
        state_ref[...] = jnp.zeros(state_ref.shape, F32)

    dt = _softplus(dt_ref[...] + bias_ref[...])
    a = -jnp.exp(alog_ref[...]) * dt
    dtt = _softplus(dtt_ref[...] + biast_ref[...])
    at = -jnp.exp(alogt_ref[...]) * dtt

    row = lax.broadcasted_iota(jnp.int32, (q, q), 0)
    col = lax.broadcasted_iota(jnp.int32, (q, q), 1)
    lower = row >= col
    a_cum = _dot(lower.astype(F32), a, precision=HIGHEST)
    a_cumt = _dot(at, (row <= col).astype(F32), precision=HIGHEST)
    a_tot = a_cum[q - 1:q, :]

    hh = lax.broadcasted_iota(jnp.int32, (2 * n_heads, d_inner), 0)
    cc = lax.broadcasted_iota(jnp.int32, (2 * n_heads, d_inner), 1)
    expand = (cc // p == hh % n_heads).astype(BF16)
    per_head = jnp.concatenate(
        [dt, jnp.exp(a_cum), jnp.exp(a_tot - a_cum),
         jnp.broadcast_to(jnp.exp(a_tot), (V7X_SUBLANES, n_heads))], axis=0)
    per_channel = _dot(jnp.concatenate(_split_bf16(per_head), axis=1), expand)
    dt_full = per_channel[0:q]
    dec_out_full = per_channel[q:2 * q]
    dec_st_full = per_channel[2 * q:3 * q]
    chunk_dec_full = per_channel[3 * q:3 * q + 1]

    lane = lax.broadcasted_iota(jnp.int32, (1, gw), 1)
    for g in range(SSD_GROUPS):
        sl = slice(g * gw, (g + 1) * gw)
        bg = b_ref[:, g * SSD_STATE:(g + 1) * SSD_STATE]
        cg = c_ref[:, g * SSD_STATE:(g + 1) * SSD_STATE]
        xs_g = xs_ref[:, sl].astype(F32)
        xdt = xs_g * dt_full[:, sl]
        cb = _dot_nt(cg, bg)
        s_prev = state_ref[g]
        y = _dot(cg, s_prev.astype(BF16)) * dec_out_full[:, sl]
        for r in range(heads_per_group):
            hd = g * heads_per_group + r
            seg = a_cum[:, hd:hd + 1] - a_cumt[hd:hd + 1, :]
            m = jnp.where(lower, cb * jnp.exp(seg), 0.0).astype(BF16)
            xr = jnp.where((lane >= r * p) & (lane < (r + 1) * p), xdt, 0.0).astype(BF16)
            y = y + _dot(m, xr)
        xw = (xdt * dec_st_full[:, sl]).astype(BF16)
        state_ref[g] = s_prev * chunk_dec_full[:, sl] + _dot_tn(bg, xw)
        y = y + xs_g * dfull_ref[:, sl]
        zg = z_ref[:, sl].astype(F32)
        yz = y * (zg * _sigmoid(zg))
        o_ref[:, sl] = _rms(yz, nw_ref[:, sl]).astype(o_ref.dtype)


def _ssd_mixer(h, bsz, seqlen, g_norm, w_in, conv_w, conv_b, dt_bias, a_log, d_skip, norm_w, w_out):
    t, d = h.shape
    n_heads = dt_bias.shape[0]
    d_inner = n_heads * SSD_HEAD_DIM
    gn = SSD_GROUPS * SSD_STATE
    conv_dim = d_inner + 2 * gn
    wz = w_in[:, :d_inner].astype(BF16)
    wx = w_in[:, d_inner:d_inner + conv_dim].astype(BF16)
    wdt = _split_weight(w_in[:, d_inner + conv_dim:])
    tm = SSD_IN_TILE
    z, xbc, dt_raw = pl.pallas_call(
        functools.partial(_ssd_in_kernel, tiles_per_seq=seqlen // tm),
        grid=(t // tm,),
        in_specs=[pl.BlockSpec((tm, d), lambda i: (i, 0)),
                  _const_spec((1, d)),
                  _const_spec((d, d_inner)),
                  _const_spec((d, conv_dim)),
                  _const_spec((d, 2 * n_heads)),
                  _const_spec((SSD_CONV, conv_dim)),
                  _const_spec((1, conv_dim))],
        out_specs=[pl.BlockSpec((tm, d_inner), lambda i: (i, 0)),
                   pl.BlockSpec((tm, conv_dim), lambda i: (i, 0)),
                   pl.BlockSpec((tm, n_heads), lambda i: (i, 0))],
        out_shape=[jax.ShapeDtypeStruct((t, d_inner), BF16),
                   jax.ShapeDtypeStruct((t, conv_dim), BF16),
                   jax.ShapeDtypeStruct((t, n_heads), F32)],
        scratch_shapes=[pltpu.VMEM((tm + CONV_HALO, conv_dim), F32)],
        compiler_params=_cparams("arbitrary"),
        name="ssd_in",
    )(h, g_norm.reshape(1, d), wz, wx, wdt, conv_w, conv_b.reshape(1, conv_dim))

    q = SSD_CHUNK
    nc = seqlen // q
    xbc3 = xbc.reshape(bsz, seqlen, conv_dim)
    dt3 = dt_raw.reshape(bsz, seqlen, n_heads)
    dtt3 = jnp.swapaxes(dt3, 1, 2)
    nb_x = d_inner // gn
    yg = pl.pallas_call(
        _ssd_scan_kernel,
        grid=(bsz, nc),
        in_specs=[pl.BlockSpec((None, q, d_inner), lambda b, c: (b, c, 0)),
                  pl.BlockSpec((None, q, gn), lambda b, c: (b, c, nb_x)),
                  pl.BlockSpec((None, q, gn), lambda b, c: (b, c, nb_x + 1)),
                  pl.BlockSpec((None, q, d_inner), lambda b, c: (b, c, 0)),
                  pl.BlockSpec((None, q, n_heads), lambda b, c: (b, c, 0)),
                  pl.BlockSpec((None, n_heads, q), lambda b, c: (b, 0, c)),
                  _const_spec((1, n_heads)), _const_spec((n_heads, 1)),
                  _const_spec((1, n_heads)), _const_spec((n_heads, 1)),
                  _const_spec((1, d_inner)), _const_spec((1, d_inner))],
        out_specs=pl.BlockSpec((None, q, d_inner), lambda b, c: (b, c, 0)),
        out_shape=jax.ShapeDtypeStruct((bsz, seqlen, d_inner), BF16),
        scratch_shapes=[pltpu.VMEM((SSD_GROUPS, SSD_STATE, d_inner // SSD_GROUPS), F32)],
        compiler_params=_cparams("arbitrary", "arbitrary"),
        name="ssd_scan",
    )(xbc3, xbc3, xbc3, z.reshape(bsz, seqlen, d_inner), dt3, dtt3,
      dt_bias.reshape(1, n_heads), dt_bias.reshape(n_heads, 1),
      a_log.reshape(1, n_heads), a_log.reshape(n_heads, 1),
      jnp.repeat(d_skip, SSD_HEAD_DIM).reshape(1, d_inner), norm_w.reshape(1, d_inner))
    return _proj_residual(yg.reshape(t, d_inner), w_out.astype(BF16), h)


def _sb_in_kernel(h_ref, g_ref, w_ref, o_ref, *, q_cols, q_scale):
    u = _rms(h_ref[...], g_ref[...]).astype(BF16)
    o_ref[:, :q_cols] = (_dot(u, w_ref[:, :q_cols]) * q_scale).astype(o_ref.dtype)
    o_ref[:, q_cols:] = _dot(u, w_ref[:, q_cols:]).astype(o_ref.dtype)


def _sb_attn_kernel(q_ref, k_ref, v_ref, o_ref, acc_ref, *, tile):
    seqlen = q_ref.shape[0]
    nq = seqlen // tile
    pw = 2 * SB_HEAD_DIM
    n_pairs = q_ref.shape[1] // pw
    lane = lax.broadcasted_iota(jnp.int32, (1, pw), 1)
    first = lane < SB_HEAD_DIM
    row = lax.broadcasted_iota(jnp.int32, (tile, tile), 0)
    col = lax.broadcasted_iota(jnp.int32, (tile, tile), 1)
    neg_from = jnp.where(row >= col, -1.0, 0.0).astype(BF16)
    past = col < row
    past2 = jnp.concatenate([past, past], axis=0)

    def k_tile(qs, kj, csums, diagonal):
        rows = pl.ds(pl.multiple_of(kj * tile, tile), tile)
        out = []
        for p in range(n_pairs):
            k = k_ref[rows, p * pw:(p + 1) * pw]
            v = v_ref[rows, p * pw:(p + 1) * pw]
            t = _dot_nt(qs[p], k)
            neg_abs = pltpu.bitcast(pltpu.bitcast(t, jnp.uint32) | jnp.uint32(0x80000000), F32)
            sp = jnp.maximum(t, 0.0) + jnp.log(1.0 + jnp.exp2(neg_abs)) * _LOG2_E
            if diagonal:
                sp = jnp.where(past2, sp, 0.0)
            w = jnp.exp2(t + _dot(sp.astype(BF16), neg_from) + csums[p])
            if diagonal:
                w = jnp.where(past2, w, 0.0)
            acc_ref[p] += _dot(w.astype(BF16), v)
            out.append(csums[p] - jnp.sum(sp, axis=1, keepdims=True))
        return tuple(out)

    def q_body(qi, _):
        qrows = pl.ds(pl.multiple_of(qi * tile, tile), tile)
        qs = []
        for p in range(n_pairs):
            q = q_ref[qrows, p * pw:(p + 1) * pw]
            zero = jnp.zeros_like(q)
            qs.append(jnp.concatenate([jnp.where(first, q, zero), jnp.where(first, zero, q)], axis=0))
            acc_ref[p] = jnp.zeros(acc_ref.shape[1:], F32)
        init = tuple(jnp.zeros((2 * tile, 1), F32) for _ in range(n_pairs))
        csums = k_tile(qs, qi, init, True)
        lax.fori_loop(0, qi, lambda s, c: k_tile(qs, qi - 1 - s, c, False), csums)
        for p in range(n_pairs):
            o = jnp.where(first, acc_ref[p, 0:tile, :], acc_ref[p, tile:2 * tile, :])
            o_ref[qrows, p * pw:(p + 1) * pw] = o.astype(o_ref.dtype)
        return 0

    lax.fori_loop(0, nq, q_body, 0)


def _sb_mixer(h, bsz, seqlen, g_norm, w_in, w_out):
    t, d = h.shape
    hd = w_in.shape[1] // 3
    tm = ROW_TILE
    qkv = pl.pallas_call(
        functools.partial(_sb_in_kernel, q_cols=hd, q_scale=SB_HEAD_DIM ** -0.5 * _LOG2_E),
        grid=(t // tm,),
        in_specs=[pl.BlockSpec((tm, d), lambda i: (i, 0)),
                  _const_spec((1, d)),
                  _const_spec((d, 3 * hd))],
        out_specs=pl.BlockSpec((tm, 3 * hd), lambda i: (i, 0)),
        out_shape=jax.ShapeDtypeStruct((t, 3 * hd), BF16),
        compiler_params=_cparams("arbitrary"),
        name="sb_in",
    )(h, g_norm.reshape(1, d), w_in.astype(BF16))
    qkv3 = qkv.reshape(bsz, seqlen, 3 * hd)
    bw = SB_PAIRS_PER_STEP * 2 * SB_HEAD_DIM
    nb = hd // bw
    tile = min(ATTN_TILE, seqlen)
    o = pl.pallas_call(
        functools.partial(_sb_attn_kernel, tile=tile),
        grid=(bsz, nb),
        in_specs=[pl.BlockSpec((None, seqlen, bw), lambda b, j: (b, 0, j)),
                  pl.BlockSpec((None, seqlen, bw), lambda b, j: (b, 0, nb + j)),
                  pl.BlockSpec((None, seqlen, bw), lambda b, j: (b, 0, 2 * nb + j))],
        out_specs=pl.BlockSpec((None, seqlen, bw), lambda b, j: (b, 0, j)),
        out_shape=jax.ShapeDtypeStruct((bsz, seqlen, hd), BF16),
        scratch_shapes=[pltpu.VMEM((SB_PAIRS_PER_STEP, 2 * tile, 2 * SB_HEAD_DIM), F32)],
        compiler_params=_cparams("arbitrary", "arbitrary"),
        name="sb_attn",
    )(qkv3, qkv3, qkv3)
    return _proj_residual(o.reshape(t, hd), w_out.astype(BF16), h)


def _lru_kernel(h_ref, g_ref, win_ref, cw_ref, cb_ref, wband_ref, br_ref, bi_ref, lam_ref,
                y_ref, pad_ref, xc_ref, a_ref, b_ref, hs_ref, carry_ref, *, band_starts):
    bsz, tt, d = h_ref.shape
    width = y_ref.shape[2]

    @pl.when(pl.program_id(0) == 0)
    def _():
        carry_ref[...] = jnp.zeros(carry_ref.shape, F32)
        for b in range(bsz):
            pad_ref[b, 0:CONV_HALO, :] = jnp.zeros((CONV_HALO, width), F32)

    u = _rms(h_ref[...].reshape(bsz * tt, d), g_ref[...]).astype(BF16)
    proj = _dot(u, win_ref[...])
    gate_branch = proj[:, :width]
    for b in range(bsz):
        pad_ref[b, CONV_HALO:CONV_HALO + tt, :] = proj[b * tt:(b + 1) * tt, width:]
        acc = cb_ref[...] + cw_ref[LRU_CONV - 1:LRU_CONV, :] * pad_ref[b, CONV_HALO:CONV_HALO + tt, :]
        for k in range(LRU_CONV - 1):
            off = CONV_HALO - (LRU_CONV - 1) + k
            acc = acc + cw_ref[k:k + 1, :] * pad_ref[b, off:off + tt, :]
        xc_ref[b * tt:(b + 1) * tt, :] = acc
        pad_ref[b, 0:CONV_HALO, :] = pad_ref[b, tt:tt + CONV_HALO, :]

    xcb = xc_ref[...].astype(BF16)
    n_lt = width // V7X_LANES
    pitch = a_ref.shape[1] // bsz
    neg_c_softplus = -LRU_C * _softplus(-lam_ref[...])
    for c in range(n_lt):
        cs = slice(c * V7X_LANES, (c + 1) * V7X_LANES)
        pre = _dot(xcb[:, band_starts[c]:band_starts[c] + LRU_BAND], wband_ref[c])
        r = _sigmoid(pre[:, :V7X_LANES] + br_ref[:, cs])
        ig = _sigmoid(pre[:, V7X_LANES:] + bi_ref[:, cs])
        a = jnp.exp(neg_c_softplus[:, cs] * r)
        bterm = jnp.sqrt(1.0 - a * a) * (ig * xc_ref[:, cs])
        for b in range(bsz):
            a_ref[c, b * pitch:b * pitch + tt, :] = a[b * tt:(b + 1) * tt, :]
            b_ref[c, b * pitch:b * pitch + tt, :] = bterm[b * tt:(b + 1) * tt, :]

    def step(j, hprev):
        hnew = []
        for c in range(n_lt):
            rows_j = pl.ds(j, bsz, stride=pitch)
            hc = a_ref[c, rows_j, :] * hprev[c] + b_ref[c, rows_j, :]
            hs_ref[c, rows_j, :] = hc
            hnew.append(hc)
        return tuple(hnew)

    hlast = lax.fori_loop(0, tt, step, tuple(carry_ref[c] for c in range(n_lt)), unroll=LRU_SCAN_UNROLL)
    for c in range(n_lt):
        carry_ref[c] = hlast[c]
    c0 = math.sqrt(2.0 / math.pi)
    gelu = 0.5 * gate_branch * (1.0 + jnp.tanh(c0 * (gate_branch + 0.044715 * gate_branch ** 3)))
    for b in range(bsz):
        hs_b = jnp.concatenate([hs_ref[c, b * pitch:b * pitch + tt, :] for c in range(n_lt)], axis=1)
        y_ref[b] = (hs_b * gelu[b * tt:(b + 1) * tt, :]).astype(y_ref.dtype)


def _block_diag(w):
    nb, bd, _ = w.shape
    rows = [jnp.pad(w[b], ((0, 0), (b * bd, (nb - 1 - b) * bd))) for b in range(nb)]
    return jnp.concatenate(rows, axis=0)


def _lru_band_starts(width):
    bd = width // LRU_BLOCKS
    starts = []
    for j in range(width // V7X_LANES):
        b_lo = (j * V7X_LANES) // bd
        b_hi = ((j + 1) * V7X_LANES - 1) // bd
        lo = (b_lo * bd) // V7X_LANES * V7X_LANES
        assert (b_hi + 1) * bd - lo <= LRU_BAND
        starts.append(min(lo, width - LRU_BAND))
    return tuple(starts)


def _lru_band_weights(w_rg, w_ig, starts):
    dense_r, dense_i = _block_diag(w_rg), _block_diag(w_ig)
    tiles = []
    for j, s in enumerate(starts):
        cs = slice(j * V7X_LANES, (j + 1) * V7X_LANES)
        tiles.append(jnp.concatenate([dense_r[s:s + LRU_BAND, cs], dense_i[s:s + LRU_BAND, cs]], axis=1))
    return jnp.stack(tiles).astype(BF16)


def _lru_mixer(h, bsz, seqlen, g_norm, w_in, conv_w, conv_b, w_rg, b_rg, w_ig, b_ig, lam, w_out):
    t, d = h.shape
    width = lam.shape[0]
    tt = min(LRU_TIME_TILE, seqlen)
    rows = bsz * tt
    n_lt = width // V7X_LANES
    scan_rows = bsz * (tt + LRU_PITCH_PAD)
    starts = _lru_band_starts(width)
    y = pl.pallas_call(
        functools.partial(_lru_kernel, band_starts=starts),
        grid=(seqlen // tt,),
        in_specs=[pl.BlockSpec((bsz, tt, d), lambda i: (0, i, 0)),
                  _const_spec((1, d)),
                  _const_spec((d, 2 * width)),
                  _const_spec((LRU_CONV, width)), _const_spec((1, width)),
                  _const_spec((n_lt, LRU_BAND, 2 * V7X_LANES)),
                  _const_spec((1, width)), _const_spec((1, width)),
                  _const_spec((1, width))],
        out_specs=pl.BlockSpec((bsz, tt, width), lambda i: (0, i, 0)),
        out_shape=jax.ShapeDtypeStruct((bsz, seqlen, width), BF16),
        scratch_shapes=[pltpu.VMEM((bsz, tt + CONV_HALO, width), F32),
                        pltpu.VMEM((rows, width), F32),
                        pltpu.VMEM((n_lt, scan_rows, V7X_LANES), F32),
                        pltpu.VMEM((n_lt, scan_rows, V7X_LANES), F32),
                        pltpu.VMEM((n_lt, scan_rows, V7X_LANES), F32),
                        pltpu.VMEM((n_lt, bsz, V7X_LANES), F32)],
        compiler_params=_cparams("arbitrary"),
        name="lru",
    )(h.reshape(bsz, seqlen, d), g_norm.reshape(1, d), w_in.astype(BF16),
      conv_w, conv_b.reshape(1, width), _lru_band_weights(w_rg, w_ig, starts),
      b_rg.reshape(1, width), b_ig.reshape(1, width), lam.reshape(1, width))
    return _proj_residual(y.reshape(t, width), w_out.astype(BF16), h)


def _diff_in_kernel(h_ref, g_ref, w_ref, cos_ref, sin_ref, qn_ref, kn_ref, q_ref, k_ref, v_ref,
                    *, q_scale):
    d_q = q_ref.shape[1]
    hdim = DIFF_HEAD_DIM
    n_grp = d_q // hdim
    u = _rms(h_ref[...], g_ref[...]).astype(BF16)
    v_ref[...] = _dot(u, w_ref[:, 2 * d_q:]).astype(v_ref.dtype)

    ci = lax.broadcasted_iota(jnp.int32, (d_q, n_grp), 0)
    gi = lax.broadcasted_iota(jnp.int32, (d_q, n_grp), 1)
    gsum = (ci // hdim == gi).astype(BF16)
    gj = lax.broadcasted_iota(jnp.int32, (2 * n_grp, d_q), 0)
    cj = lax.broadcasted_iota(jnp.int32, (2 * n_grp, d_q), 1)
    gexp = (cj // hdim == gj % n_grp).astype(BF16)
    lane = lax.broadcasted_iota(jnp.int32, (1, d_q), 1)
    first_half = (lane % hdim) < (hdim // 2)
    reps = d_q // cos_ref.shape[1]
    cos_t = jnp.tile(cos_ref[...], (1, reps))
    sin_t = jnp.tile(sin_ref[...], (1, reps))

    def norm_rope(x, nw):
        sq = x * x
        sq_hi = sq.astype(BF16)
        sq_lo = (sq - sq_hi.astype(F32)).astype(BF16)
        ms = (_dot(sq_hi, gsum) + _dot(sq_lo, gsum)) * (1.0 / hdim)
        inv = lax.rsqrt(ms + RMS_EPS)
        inv_hi = inv.astype(BF16)
        inv_lo = (inv - inv_hi.astype(F32)).astype(BF16)
        inv_full = _dot(jnp.concatenate([inv_hi, inv_lo], axis=1), gexp)
        xn = x * inv_full * nw
        rot = jnp.where(first_half, pltpu.roll(xn, d_q - hdim // 2, 1), pltpu.roll(xn, hdim // 2, 1))
        return xn * cos_t + rot * sin_t

    q_ref[...] = (norm_rope(_dot(u, w_ref[:, :d_q]), qn_ref[...]) * q_scale).astype(q_ref.dtype)
    k_ref[...] = norm_rope(_dot(u, w_ref[:, d_q:2 * d_q]), kn_ref[...]).astype(k_ref.dtype)


def _diff_attn_kernel(q_ref, k_ref, v_ref, lq1_ref, lk1_ref, lq2_ref, lk2_ref, sn_ref, o_ref,
                      acc_ref, *, tile, lambda_init):
    seqlen = q_ref.shape[0]
    nq = seqlen // tile
    hdim = DIFF_HEAD_DIM
    hw = 2 * hdim
    n_heads = q_ref.shape[1] // hw
    lane = lax.broadcasted_iota(jnp.int32, (1, hw), 1)
    first = lane < hdim
    row = lax.broadcasted_iota(jnp.int32, (tile, tile), 0)
    col = lax.broadcasted_iota(jnp.int32, (tile, tile), 1)
    causal = col <= row
    causal2 = jnp.concatenate([causal, causal], axis=0)
    ones = jnp.ones((tile, hw), BF16)
    lam = (jnp.exp(jnp.sum(lq1_ref[...] * lk1_ref[...], axis=1, keepdims=True))
           - jnp.exp(jnp.sum(lq2_ref[...] * lk2_ref[...], axis=1, keepdims=True)) + lambda_init)

    def k_tile(qs, kj, maxes, diagonal):
        rows = pl.ds(pl.multiple_of(kj * tile, tile), tile)
        out = []
        for hd in range(n_heads):
            k = k_ref[rows, hd * hw:(hd + 1) * hw]
            v1 = jnp.concatenate([v_ref[rows, hd * hw:(hd + 1) * hw], ones], axis=1)
            s = _dot_nt(qs[hd], k)
            if diagonal:
                s = jnp.where(causal2, s, -jnp.inf)
            mx_new = jnp.maximum(maxes[hd], jnp.max(s, axis=1, keepdims=True))
            alpha = jnp.exp2(maxes[hd] - mx_new)
            pr = jnp.exp2(s - mx_new)
            acc_ref[hd] = acc_ref[hd] * alpha + _dot(pr.astype(BF16), v1)
            out.append(mx_new)
        return tuple(out)

    def q_body(qi, _):
        qrows = pl.ds(pl.multiple_of(qi * tile, tile), tile)
        qs = []
        for hd in range(n_heads):
            q = q_ref[qrows, hd * hw:(hd + 1) * hw]
            zero = jnp.zeros_like(q)
            qs.append(jnp.concatenate([jnp.where(first, q, zero), jnp.where(first, zero, q)], axis=0))
            acc_ref[hd] = jnp.zeros(acc_ref.shape[1:], F32)
        init = tuple(jnp.full((2 * tile, 1), -jnp.inf, F32) for _ in range(n_heads))
        maxes = k_tile(qs, qi, init, True)
        lax.fori_loop(0, qi, lambda s, c: k_tile(qs, qi - 1 - s, c, False), maxes)
        for hd in range(n_heads):
            p = acc_ref[hd, :, 0:hw] / acc_ref[hd, :, hw:2 * hw]
            o = p[0:tile] - lam * p[tile:2 * tile]
            o = _rms(o, sn_ref[...]) * (1.0 - lambda_init)
            o_ref[qrows, hd * hw:(hd + 1) * hw] = o.astype(o_ref.dtype)
        return 0

    lax.fori_loop(0, nq, q_body, 0)


def _diff_mixer(h, bsz, seqlen, cos_t, sin_t, g_norm, w_in, q_norm, k_norm, lq1, lk1, lq2, lk2,
                sub_norm, w_out, lambda_init):
    t, d = h.shape
    d_q = w_in.shape[1] // 3
    hdim = DIFF_HEAD_DIM
    tm = ROW_TILE
    tw = cos_t.shape[1]
    q, k, v = pl.pallas_call(
        functools.partial(_diff_in_kernel, q_scale=hdim ** -0.5 * _LOG2_E),
        grid=(t // tm,),
        in_specs=[pl.BlockSpec((tm, d), lambda i: (i, 0)),
                  _const_spec((1, d)),
                  _const_spec((d, 3 * d_q)),
                  pl.BlockSpec((tm, tw), lambda i: (i, 0)),
                  pl.BlockSpec((tm, tw), lambda i: (i, 0)),
                  _const_spec((1, d_q)), _const_spec((1, d_q))],
        out_specs=[pl.BlockSpec((tm, d_q), lambda i: (i, 0))] * 3,
        out_shape=[jax.ShapeDtypeStruct((t, d_q), BF16)] * 3,
        compiler_params=_cparams("arbitrary"),
        name="diff_in",
    )(h, g_norm.reshape(1, d), w_in.astype(BF16), cos_t, sin_t,
      jnp.tile(q_norm, d_q // hdim).reshape(1, d_q), jnp.tile(k_norm, d_q // hdim).reshape(1, d_q))
    hw = 2 * hdim
    bw = DIFF_HEADS_PER_STEP * hw
    tile = min(ATTN_TILE, seqlen)
    spec = pl.BlockSpec((None, seqlen, bw), lambda b, j: (b, 0, j))
    vec = _const_spec((1, hdim))
    o = pl.pallas_call(
        functools.partial(_diff_attn_kernel, tile=tile, lambda_init=lambda_init),
        grid=(bsz, d_q // bw),
        in_specs=[spec, spec, spec, vec, vec, vec, vec, _const_spec((1, hw))],
        out_specs=spec,
        out_shape=jax.ShapeDtypeStruct((bsz, seqlen, d_q), BF16),
        scratch_shapes=[pltpu.VMEM((DIFF_HEADS_PER_STEP, 2 * tile, 2 * hw), F32)],
        compiler_params=_cparams("arbitrary", "arbitrary"),
        name="diff_attn",
    )(q.reshape(bsz, seqlen, d_q), k.reshape(bsz, seqlen, d_q), v.reshape(bsz, seqlen, d_q),
      lq1.reshape(1, hdim), lk1.reshape(1, hdim), lq2.reshape(1, hdim), lk2.reshape(1, hdim),
      sub_norm.reshape(1, hw))
    return _proj_residual(o.reshape(t, d_q), w_out.astype(BF16), h)


_R_E1, _R_E2, _R_W1, _R_W2, _R_RANK1, _R_RANK2 = range(6)
_C_LOC1, _C_LOC2, _C_W1, _C_W2 = range(4)


def _router_kernel(h_ref, g_ref, w_ref, b_ref, rec_ref, base_ref, cnt_ref, run_ref):
    tm = h_ref.shape[0]
    ne, ng, epg = MOE_EXPERTS, MOE_GROUPS, MOE_EXPERTS_PER_GROUP

    @pl.when(pl.program_id(0) == 0)
    def _():
        run_ref[...] = jnp.zeros(run_ref.shape, F32)

    base_ref[...] = run_ref[...]

    u = _rms(h_ref[...], g_ref[...])
    logits = _dot_split(u, u.astype(BF16), w_ref) + b_ref[...]
    lane = lax.broadcasted_iota(jnp.int32, logits.shape, 1)
    lane_f = lane.astype(F32)
    lane_group = (lane // epg).astype(F32)
    neg = -jnp.inf
    big = float(V7X_LANES)

    gl = jnp.where((lane >= ne) & (lane < ne + ng), logits, neg)
    gmax = jnp.max(gl, axis=1, keepdims=True)
    gsel = jnp.min(jnp.where(gl == gmax, lane_f - ne, big), axis=1, keepdims=True)
    g_gate = 1.0 / jnp.sum(jnp.exp(gl - gmax), axis=1, keepdims=True)

    el = jnp.where((lane < ne) & (lane_group == gsel), logits, neg)
    m1 = jnp.max(el, axis=1, keepdims=True)
    i1 = jnp.min(jnp.where(el == m1, lane_f, big), axis=1, keepdims=True)
    el2 = jnp.where(lane_f == i1, neg, el)
    m2 = jnp.max(el2, axis=1, keepdims=True)
    i2 = jnp.min(jnp.where(el2 == m2, lane_f, big), axis=1, keepdims=True)
    e21 = jnp.exp(m2 - m1)
    w1 = g_gate / (1.0 + e21)
    w2 = g_gate * e21 / (1.0 + e21)

    oh1 = lane_f == i1
    oh2 = lane_f == i2
    oh = jnp.where(oh1 | oh2, 1.0, 0.0)
    row = lax.broadcasted_iota(jnp.int32, (tm, tm), 0)
    col = lax.broadcasted_iota(jnp.int32, (tm, tm), 1)
    before = _dot((col < row).astype(BF16), oh.astype(BF16))
    pos = run_ref[0:1, :] + before
    rank1 = jnp.sum(jnp.where(oh1, pos, 0.0), axis=1, keepdims=True)
    rank2 = jnp.sum(jnp.where(oh2, pos, 0.0), axis=1, keepdims=True)
    run_ref[...] = run_ref[...] + jnp.sum(oh, axis=0, keepdims=True)
    cnt_ref[...] = run_ref[...]

    rec = jnp.zeros(logits.shape, F32)
    for idx, val in ((_R_E1, i1), (_R_E2, i2), (_R_W1, w1), (_R_W2, w2),
                     (_R_RANK1, rank1), (_R_RANK2, rank2)):
        rec = jnp.where(lane == idx, val, rec)
    rec_ref[...] = rec


def _pack_bf16_pair(x):
    n = x.shape[1] // 2
    xr = x.astype(BF16).astype(F32)
    return pltpu.bitcast(xr[:, :n], jnp.uint32) | (pltpu.bitcast(xr[:, n:], jnp.uint32) >> 16)


def _unpack_bf16_pair(w):
    hi = pltpu.bitcast(w & jnp.uint32(0xFFFF0000), F32)
    lo = pltpu.bitcast(w << 16, F32)
    return jnp.concatenate([hi, lo], axis=1)


def _chunk_copies(chunk_of_ref, n_chunks, copy_of):
    i = pl.program_id(0)

    def rows(chunk):
        return pl.ds(pl.multiple_of(chunk * MOE_CHUNK, MOE_CHUNK), MOE_CHUNK)

    def issue(j, _):
        copy_of(rows(j), rows(chunk_of_ref[i * n_chunks + j])).start()
        return 0

    def drain(j, _):
        copy_of(rows(0), rows(0)).wait()
        return 0

    lax.fori_loop(0, n_chunks, issue, 0, unroll=DMA_LOOP_UNROLL)
    lax.fori_loop(0, n_chunks, drain, 0, unroll=DMA_LOOP_UNROLL)


def _dispatch_kernel(chunk_of_ref, h_ref, g_ref, loct_ref, rows_in_ref, rows_ref, sorted_ref, sem):
    del rows_in_ref
    tm = h_ref.shape[0]
    u = _rms(h_ref[...], g_ref[...]).astype(BF16)
    local_row = lax.broadcasted_iota(jnp.int32, (sorted_ref.shape[0], tm), 0)
    loc = loct_ref[...]
    pick = jnp.where(local_row == loc[0:1, :], 1.0, jnp.where(local_row == loc[1:2, :], 1.0, 0.0))
    sorted_ref[...] = _pack_bf16_pair(_dot(pick.astype(BF16), u))
    _chunk_copies(chunk_of_ref, sorted_ref.shape[0] // MOE_CHUNK,
                  lambda lr, gr: pltpu.make_async_copy(sorted_ref.at[lr], rows_ref.at[gr], sem))


def _expert_kernel(be_ref, nu_ref, nxt_ref, x_ref, wg_hbm, wu_hbm, wd_hbm, y_ref,
                   wgf_ref, wuf_ref, wdf_ref, wgb_ref, wub_ref, wdb_ref, sems, *, layer):
    i = pl.program_id(0)

    def weight_copies(e):
        return (pltpu.make_async_copy(wg_hbm.at[layer, e], wgf_ref, sems.at[0]),
                pltpu.make_async_copy(wu_hbm.at[layer, e], wuf_ref, sems.at[1]),
                pltpu.make_async_copy(wd_hbm.at[layer, e], wdf_ref, sems.at[2]))

    @pl.when(i < nu_ref[0])
    def _():
        e = be_ref[i]

        @pl.when(i == 0)
        def _():
            for c in weight_copies(e):
                c.start()

        @pl.when(jnp.logical_or(i == 0, e != be_ref[jnp.maximum(i - 1, 0)]))
        def _():
            for c in weight_copies(e):
                c.wait()
            wgb_ref[...] = wgf_ref[...].astype(BF16)
            wub_ref[...] = wuf_ref[...].astype(BF16)
            wdb_ref[...] = wdf_ref[...].astype(BF16)
            nxt = nxt_ref[i]

            @pl.when(nxt >= 0)
            def _():
                for c in weight_copies(nxt):
                    c.start()

        x = _unpack_bf16_pair(x_ref[...]).astype(BF16)
        gate = _dot(x, wgb_ref[...])
        up = _dot(x, wub_ref[...])
        hid = (gate * _sigmoid(gate) * up).astype(BF16)
        y_ref[...] = _pack_bf16_pair(_dot(hid, wdb_ref[...]))

    @pl.when(i >= nu_ref[0])
    def _():
        y_ref[...] = jnp.zeros(y_ref.shape, y_ref.dtype)


def _combine_kernel(chunk_of_ref, h_ref, crec_ref, rows_ref, o_ref, sorted_ref, sem):
    tm = h_ref.shape[0]
    _chunk_copies(chunk_of_ref, sorted_ref.shape[0] // MOE_CHUNK,
                  lambda lr, gr: pltpu.make_async_copy(rows_ref.at[gr], sorted_ref.at[lr], sem))
    crec = crec_ref[...]
    loc0 = crec[:, _C_LOC1:_C_LOC1 + 1].astype(jnp.int32)
    loc1 = crec[:, _C_LOC2:_C_LOC2 + 1].astype(jnp.int32)
    w1 = crec[:, _C_W1:_C_W1 + 1]
    w2 = crec[:, _C_W2:_C_W2 + 1]
    local_row = lax.broadcasted_iota(jnp.int32, (tm, sorted_ref.shape[0]), 1)
    weights = jnp.where(local_row == loc0, w1, jnp.where(local_row == loc1, w2, 0.0)).astype(BF16)
    y = _unpack_bf16_pair(sorted_ref[...]).astype(BF16)
    o_ref[...] = h_ref[...] + _dot(weights, y)


def _hier_moe(h, g_norm, w_group, b_group, w_expert, b_expert, layer, w_gate, w_up, w_down):
    t, d = h.shape
    ne, ng = MOE_EXPERTS, MOE_GROUPS
    ff = w_gate.shape[3]
    dp = d // 2
    tm = min(MOE_TOKEN_TILE, t)
    g2 = g_norm.reshape(1, d)
    lane_pad = V7X_LANES - ne - ng
    w_router = _split_weight(
        jnp.concatenate([w_expert, w_group, jnp.zeros((d, lane_pad), F32)], axis=1))
    b_router = jnp.concatenate([b_expert, b_group, jnp.zeros((lane_pad,), F32)]).reshape(1, V7X_LANES)
    nt = t // tm
    rec, base, cnt = pl.pallas_call(
        _router_kernel,
        grid=(nt,),
        in_specs=[pl.BlockSpec((tm, d), lambda i: (i, 0)),
                  _const_spec((1, d)),
                  _const_spec((d, 2 * V7X_LANES)),
                  _const_spec((1, V7X_LANES))],
        out_specs=[pl.BlockSpec((tm, V7X_LANES), lambda i: (i, 0)),
                   pl.BlockSpec((V7X_SUBLANES, V7X_LANES), lambda i: (i, 0)),
                   _const_spec((V7X_SUBLANES, V7X_LANES))],
        out_shape=[jax.ShapeDtypeStruct((t, V7X_LANES), F32),
                   jax.ShapeDtypeStruct((nt * V7X_SUBLANES, V7X_LANES), F32),
                   jax.ShapeDtypeStruct((V7X_SUBLANES, V7X_LANES), F32)],
        scratch_shapes=[pltpu.VMEM((V7X_SUBLANES, V7X_LANES), F32)],
        compiler_params=_cparams("arbitrary"),
        name="moe_router",
    )(h, g2, w_router, b_router)

    br, ck = MOE_ROW_BLOCK, MOE_CHUNK
    n_blocks = -(-(2 * t + nt * ne * (ck - 1) + ne * (br - 1)) // br)
    n_rows = n_blocks * br
    local_rows = 2 * tm + ne * ck
    before = base.reshape(nt, V7X_SUBLANES, V7X_LANES)[:, 0, :ne].astype(jnp.int32)
    total = cnt[0, :ne].astype(jnp.int32)
    tile_cnt = jnp.concatenate([before[1:], total[None, :]], axis=0) - before
    seg = (tile_cnt + ck - 1) // ck * ck
    region = jnp.sum(seg, axis=0)
    padded = (region + br - 1) // br * br
    pad_end = jnp.cumsum(padded)
    pad_start = pad_end - padded
    seg_global = pad_start[None, :] + jnp.cumsum(seg, axis=0) - seg
    seg_local = jnp.cumsum(seg, axis=1) - seg
    eid = rec[:, _R_E1:_R_E2 + 1].astype(jnp.int32)
    rank = rec[:, _R_RANK1:_R_RANK2 + 1].astype(jnp.int32)
    experts = jnp.arange(ne, dtype=jnp.int32)
    shift = jnp.repeat(seg_local - before, tm, axis=0)
    loc = rank + jnp.sum(jnp.where(eid[:, :, None] == experts, shift[:, None, :], 0), axis=-1)
    loct = jnp.concatenate([jnp.swapaxes(loc.reshape(nt, tm, 2), 1, 2),
                            jnp.full((nt, V7X_SUBLANES - 2, tm), -1, jnp.int32)], axis=1)
    crec = jnp.concatenate([loc.astype(F32), rec[:, _R_W1:_R_W2 + 1],
                            jnp.zeros((t, V7X_SUBLANES - 4), F32)], axis=1)
    n_local = local_rows // ck
    j = jnp.arange(n_local, dtype=jnp.int32)
    seg_end = (seg_local + seg) // ck
    owner = jnp.sum((seg_end[:, None, :] <= j[None, :, None]).astype(jnp.int32), axis=-1)
    offset = (seg_global - seg_local) // ck
    chunk_of = j[None, :] + jnp.sum(jnp.where(owner[:, :, None] == experts, offset[:, None, :], 0), axis=-1)
    in_use = owner < ne
    scatter_chunk = jnp.where(in_use, chunk_of, n_rows // ck + j[None, :]).reshape(-1)
    gather_chunk = jnp.where(in_use, chunk_of, 0).reshape(-1)
    block_row0 = jnp.arange(n_blocks, dtype=jnp.int32) * br
    block_expert = jnp.minimum(
        jnp.sum((pad_end[None, :] <= block_row0[:, None]).astype(jnp.int32), axis=1), ne - 1)
    n_used = (pad_end[-1:] // br).astype(jnp.int32)
    blocks = jnp.arange(n_blocks, dtype=jnp.int32)
    later_run = ((block_expert[None, :] != block_expert[:, None]) & (blocks[None, :] > blocks[:, None])
                 & (blocks[None, :] < n_used))
    next_expert = jnp.where(jnp.any(later_run, axis=1),
                            block_expert[jnp.argmax(later_run, axis=1)], -1).astype(jnp.int32)

    x_rows = pl.pallas_call(
        _dispatch_kernel,
        grid_spec=pltpu.PrefetchScalarGridSpec(
            num_scalar_prefetch=1,
            grid=(nt,),
            in_specs=[pl.BlockSpec((tm, d), lambda i, *_: (i, 0)),
                      _const_spec((1, d)),
                      pl.BlockSpec((None, V7X_SUBLANES, tm), lambda i, *_: (i, 0, 0)),
                      pl.BlockSpec(memory_space=pl.ANY)],
            out_specs=pl.BlockSpec(memory_space=pl.ANY),
            scratch_shapes=[pltpu.VMEM((local_rows, dp), jnp.uint32), pltpu.SemaphoreType.DMA(())]),
        out_shape=jax.ShapeDtypeStruct((n_rows + local_rows, dp), jnp.uint32),
        input_output_aliases={4: 0},
        compiler_params=_cparams("arbitrary"),
        name="moe_dispatch",
    )(scatter_chunk, h, g2, loct, jnp.zeros((n_rows + local_rows, dp), jnp.uint32))

    def blk(i, be, nu, nxt):
        return jnp.maximum(jnp.minimum(i, nu[0] - 1), 0)

    hbm = pl.BlockSpec(memory_space=pl.ANY)
    y_rows = pl.pallas_call(
        functools.partial(_expert_kernel, layer=layer),
        grid_spec=pltpu.PrefetchScalarGridSpec(
            num_scalar_prefetch=3,
            grid=(n_blocks,),
            in_specs=[pl.BlockSpec((br, dp), lambda i, be, nu, nxt: (blk(i, be, nu, nxt), 0)),
                      hbm, hbm, hbm],
            out_specs=pl.BlockSpec((br, dp), lambda i, be, nu, nxt: (i, 0)),
            scratch_shapes=[pltpu.VMEM((d, ff), F32), pltpu.VMEM((d, ff), F32), pltpu.VMEM((ff, d), F32),
                            pltpu.VMEM((d, ff), BF16), pltpu.VMEM((d, ff), BF16), pltpu.VMEM((ff, d), BF16),
                            pltpu.SemaphoreType.DMA((3,))]),
        out_shape=jax.ShapeDtypeStruct((n_rows, dp), jnp.uint32),
        compiler_params=_cparams("arbitrary"),
        name="moe_experts",
    )(block_expert, n_used, next_expert, x_rows, w_gate, w_up, w_down)

    return pl.pallas_call(
        _combine_kernel,
        grid_spec=pltpu.PrefetchScalarGridSpec(
            num_scalar_prefetch=1,
            grid=(nt,),
            in_specs=[pl.BlockSpec((tm, d), lambda i, *_: (i, 0)),
                      pl.BlockSpec((tm, V7X_SUBLANES), lambda i, *_: (i, 0)),
                      pl.BlockSpec(memory_space=pl.ANY)],
            out_specs=pl.BlockSpec((tm, d), lambda i, *_: (i, 0)),
            scratch_shapes=[pltpu.VMEM((local_rows, dp), jnp.uint32), pltpu.SemaphoreType.DMA(())]),
        out_shape=jax.ShapeDtypeStruct((t, d), F32),
        compiler_params=_cparams("arbitrary"),
        name="moe_combine",
    )(gather_chunk, h, crec, y_rows)


def _rope_tables(positions):
    half = DIFF_HEAD_DIM // 2
    inv = ROPE_THETA ** (-jnp.arange(0, DIFF_HEAD_DIM, 2, dtype=F32) / DIFF_HEAD_DIM)
    ang = positions.astype(F32).reshape(-1, 1) * inv
    cos, sin = jnp.cos(ang), jnp.sin(ang)
    cos_t = jnp.concatenate([cos] * 4, axis=1)
    sin_t = jnp.concatenate([-sin, sin, -sin, sin], axis=1)
    assert cos_t.shape[1] == 4 * half
    return cos_t, sin_t


def kernel(x, positions, norm_mix, norm_ffn, ssd_w_in, ssd_conv_w, ssd_conv_b, ssd_dt_bias, ssd_a_log, ssd_d, ssd_norm, ssd_w_out, sb_w_in, sb_w_out, lru_w_in, lru_conv_w, lru_conv_b, lru_w_rg, lru_b_rg, lru_w_ig, lru_b_ig, lru_lambda, lru_w_out, diff_w_in, diff_q_norm, diff_k_norm, diff_lam_q1, diff_lam_k1, diff_lam_q2, diff_lam_k2, diff_sub_norm, diff_w_out, moe_w_group, moe_b_group, moe_w_expert, moe_b_expert, moe_w_gate, moe_w_up, moe_w_down):
    bsz, seqlen, d = x.shape
    depth = norm_mix.shape[0]
    h = x.reshape(bsz * seqlen, d)
    cos_t, sin_t = _rope_tables(positions)
    for i in range(depth):
        m, j = i % N_MIXERS, i // N_MIXERS
        if m == 0:
            h = _ssd_mixer(h, bsz, seqlen, norm_mix[i], ssd_w_in[j], ssd_conv_w[j], ssd_conv_b[j],
                           ssd_dt_bias[j], ssd_a_log[j], ssd_d[j], ssd_norm[j], ssd_w_out[j])
        elif m == 1:
            h = _sb_mixer(h, bsz, seqlen, norm_mix[i], sb_w_in[j], sb_w_out[j])
        elif m == 2:
            h = _lru_mixer(h, bsz, seqlen, norm_mix[i], lru_w_in[j], lru_conv_w[j], lru_conv_b[j],
                           lru_w_rg[j], lru_b_rg[j], lru_w_ig[j], lru_b_ig[j], lru_lambda[j],
                           lru_w_out[j])
        else:
            lambda_init = 0.8 - 0.6 * math.exp(-0.3 * i)
            h = _diff_mixer(h, bsz, seqlen, cos_t, sin_t, norm_mix[i], diff_w_in[j], diff_q_norm[j],
                            diff_k_norm[j], diff_lam_q1[j], diff_lam_k1[j], diff_lam_q2[j],
                            diff_lam_k2[j], diff_sub_norm[j], diff_w_out[j], lambda_init)
        h = _hier_moe(h, norm_ffn[i], moe_w_group[i], moe_b_group[i], moe_w_expert[i],
                      moe_b_expert[i], i, moe_w_gate, moe_w_up, moe_w_down)
    return h.reshape(bsz, seqlen, d)
```

```python
import functools
import math

import jax
import jax.numpy as jnp
import numpy as np
from jax import lax
from jax.experimental import pallas as pl
from jax.experimental.pallas import tpu as pltpu

F32 = jnp.float32
BF16 = jnp.bfloat16
HIGHEST = lax.Precision.HIGHEST

RMS_EPS = 1e-6
_LOG2_E = math.log2(math.e)
N_MIXERS = 4

V7X_VMEM_BYTES = 64 * 1024 * 1024
V7X_LANES = 128
V7X_SUBLANES = 8
VMEM_LIMIT = (V7X_VMEM_BYTES * 7) // 8

SSD_HEAD_DIM = 64
SSD_GROUPS = 8
SSD_STATE = 128
SSD_CONV = 4
SSD_CHUNK = 128
SB_HEAD_DIM = 64
LRU_BLOCKS = 16
LRU_CONV = 4
LRU_C = 8.0
DIFF_HEAD_DIM = 64
ROPE_THETA = 10000.0
MOE_GROUPS = 4
MOE_EXPERTS_PER_GROUP = 8
MOE_EXPERTS = MOE_GROUPS * MOE_EXPERTS_PER_GROUP

ROW_TILE = 512
SSD_IN_TILE = 256
SSD_CONV_CHUNK = 512
ATTN_TILE = 256
SB_PAIRS_PER_STEP = 8
DIFF_HEADS_PER_STEP = 8
LRU_TIME_TILE = 64
LRU_PITCH_PAD = 4
LRU_BAND = 3 * V7X_LANES
LRU_SCAN_UNROLL = 4
MOE_TOKEN_TILE = 512
MOE_ROW_BLOCK = 256
MOE_CHUNK = V7X_SUBLANES
CONV_HALO = V7X_SUBLANES
DMA_LOOP_UNROLL = 8


def _cparams(*sem):
    return pltpu.CompilerParams(dimension_semantics=sem, vmem_limit_bytes=VMEM_LIMIT)


def _const_spec(shape):
    nd = len(shape)
    return pl.BlockSpec(shape, lambda *_: (0,) * nd)


def _rms(x, w):
    ms = jnp.mean(x * x, axis=-1, keepdims=True)
    return x * lax.rsqrt(ms + RMS_EPS) * w


def _sigmoid(x):
    return 1.0 / (1.0 + jnp.exp(-x))


def _softplus(x):
    return jnp.maximum(x, 0.0) + jnp.log(1.0 + jnp.exp(-jnp.abs(x)))


def _dot(a, b, **kw):
    return jnp.dot(a, b, preferred_element_type=F32, **kw)


def _dot_nt(a, b):
    return lax.dot_general(a, b, (((1,), (1,)), ((), ())), preferred_element_type=F32)


def _dot_tn(a, b):
    return lax.dot_general(a, b, (((0,), (0,)), ((), ())), preferred_element_type=F32)


def _split_bf16(x):
    hi = x.astype(BF16)
    return hi, (x - hi.astype(F32)).astype(BF16)


def _split_weight(w):
    return jnp.concatenate(_split_bf16(w), axis=1)


def _dot_split(x, x_hi, w2_ref):
    n = w2_ref.shape[1] // 2
    x_lo = (x - x_hi.astype(F32)).astype(BF16)
    t = _dot(x_hi, w2_ref[...])
    return t[:, :n] + t[:, n:] + _dot(x_lo, w2_ref[:, :n])


def _proj_residual_kernel(y_ref, w_ref, h_ref, o_ref):
    o_ref[...] = h_ref[...] + _dot(y_ref[...], w_ref[...])


def _proj_residual(y, w, h):
    t, k = y.shape
    d = w.shape[1]
    tm = ROW_TILE
    return pl.pallas_call(
        _proj_residual_kernel,
        grid=(t // tm,),
        in_specs=[pl.BlockSpec((tm, k), lambda i: (i, 0)),
                  _const_spec((k, d)),
                  pl.BlockSpec((tm, d), lambda i: (i, 0))],
        out_specs=pl.BlockSpec((tm, d), lambda i: (i, 0)),
        out_shape=jax.ShapeDtypeStruct((t, d), F32),
        compiler_params=_cparams("arbitrary"),
        name="proj_residual",
    )(y, w, h)


def _ssd_in_kernel(h_ref, g_ref, wz_ref, wx_ref, wdt_ref, cw_ref, cb_ref,
                   z_ref, xbc_ref, dt_ref, pad_ref, *, tiles_per_seq):
    i = pl.program_id(0)
    tm = h_ref.shape[0]
    u32 = _rms(h_ref[...], g_ref[...])
    u = u32.astype(BF16)
    z_ref[...] = _dot(u, wz_ref[...]).astype(z_ref.dtype)
    dt_ref[...] = _dot_split(u32, u, wdt_ref)

    @pl.when(i % tiles_per_seq == 0)
    def _():
        pad_ref[0:CONV_HALO, :] = jnp.zeros((CONV_HALO, pad_ref.shape[1]), F32)

    conv_dim = pad_ref.shape[1]
    for c0 in range(0, conv_dim, SSD_CONV_CHUNK):
        cs = slice(c0, c0 + SSD_CONV_CHUNK)
        pad_ref[CONV_HALO:CONV_HALO + tm, cs] = _dot(u, wx_ref[:, cs])
        acc = cb_ref[:, cs] + cw_ref[SSD_CONV - 1:SSD_CONV, cs] * pad_ref[CONV_HALO:CONV_HALO + tm, cs]
        for k in range(SSD_CONV - 1):
            off = CONV_HALO - (SSD_CONV - 1) + k
            acc = acc + cw_ref[k:k + 1, cs] * pad_ref[off:off + tm, cs]
        xbc_ref[:, cs] = (acc * _sigmoid(acc)).astype(xbc_ref.dtype)
        pad_ref[0:CONV_HALO, cs] = pad_ref[tm:tm + CONV_HALO, cs]


def _ssd_scan_kernel(xs_ref, b_ref, c_ref, z_ref, dt_ref, dtt_ref,
                     bias_ref, biast_ref, alog_ref, alogt_ref, dfull_ref, nw_ref,
                     o_ref, state_ref):
    q = xs_ref.shape[0]
    n_heads = dt_ref.shape[1]
    d_inner = xs_ref.shape[1]
    heads_per_group = n_heads // SSD_GROUPS
    gw = d_inner // SSD_GROUPS
    p = SSD_HEAD_DIM

    @pl.when(pl.program_id(1) == 0)
    def _():
        state_ref[...] = jnp.zeros(state_ref.shape, F32)

    dt = _softplus(dt_ref[...] + bias_ref[...])
    a = -jnp.exp(alog_ref[...]) * dt
    dtt = _softplus(dtt_ref[...] + biast_ref[...])
    at = -jnp.exp(alogt_ref[...]) * dtt

    row = lax.broadcasted_iota(jnp.int32, (q, q), 0)
    col = lax.broadcasted_iota(jnp.int32, (q, q), 1)
    lower = row >= col
    a_cum = _dot(lower.astype(F32), a, precision=HIGHEST)
    a_cumt = _dot(at, (row <= col).astype(F32), precision=HIGHEST)
    a_tot = a_cum[q - 1:q, :]

    hh = lax.broadcasted_iota(jnp.int32, (2 * n_heads, d_inner), 0)
    cc = lax.broadcasted_iota(jnp.int32, (2 * n_heads, d_inner), 1)
    expand = (cc // p == hh % n_heads).astype(BF16)
    per_head = jnp.concatenate(
        [dt, jnp.exp(a_cum), jnp.exp(a_tot - a_cum),
         jnp.broadcast_to(jnp.exp(a_tot), (V7X_SUBLANES, n_heads))], axis=0)
    per_channel = _dot(jnp.concatenate(_split_bf16(per_head), axis=1), expand)
    dt_full = per_channel[0:q]
    dec_out_full = per_channel[q:2 * q]
    dec_st_full = per_channel[2 * q:3 * q]
    chunk_dec_full = per_channel[3 * q:3 * q + 1]

    lane = lax.broadcasted_iota(jnp.int32, (1, gw), 1)
    for g in range(SSD_GROUPS):
        sl = slice(g * gw, (g + 1) * gw)
        bg = b_ref[:, g * SSD_STATE:(g + 1) * SSD_STATE]
        cg = c_ref[:, g * SSD_STATE:(g + 1) * SSD_STATE]
        xs_g = xs_ref[:, sl].astype(F32)
        xdt = xs_g * dt_full[:, sl]
        cb = _dot_nt(cg, bg)
        s_prev = state_ref[g]
        y = _dot(cg, s_prev.astype(BF16)) * dec_out_full[:, sl]
        for r in range(heads_per_group):
            hd = g * heads_per_group + r
            seg = a_cum[:, hd:hd + 1] - a_cumt[hd:hd + 1, :]
            m = jnp.where(lower, cb * jnp.exp(seg), 0.0).astype(BF16)
            xr = jnp.where((lane >= r * p) & (lane < (r + 1) * p), xdt, 0.0).astype(BF16)
            y = y + _dot(m, xr)
        xw = (xdt * dec_st_full[:, sl]).astype(BF16)
        state_ref[g] = s_prev * chunk_dec_full[:, sl] + _dot_tn(bg, xw)
        y = y + xs_g * dfull_ref[:, sl]
        zg = z_ref[:, sl].astype(F32)
        yz = y * (zg * _sigmoid(zg))
        o_ref[:, sl] = _rms(yz, nw_ref[:, sl]).astype(o_ref.dtype)


def _ssd_mixer(h, bsz, seqlen, g_norm, w_in, conv_w, conv_b, dt_bias, a_log, d_skip, norm_w, w_out):
    t, d = h.shape
    n_heads = dt_bias.shape[0]
    d_inner = n_heads * SSD_HEAD_DIM
    gn = SSD_GROUPS * SSD_STATE
    conv_dim = d_inner + 2 * gn
    wz = w_in[:, :d_inner].astype(BF16)
    wx = w_in[:, d_inner:d_inner + conv_dim].astype(BF16)
    wdt = _split_weight(w_in[:, d_inner + conv_dim:])
    tm = SSD_IN_TILE
    z, xbc, dt_raw = pl.pallas_call(
        functools.partial(_ssd_in_kernel, tiles_per_seq=seqlen // tm),
        grid=(t // tm,),
        in_specs=[pl.BlockSpec((tm, d), lambda i: (i, 0)),
                  _const_spec((1, d)),
                  _const_spec((d, d_inner)),
                  _const_spec((d, conv_dim)),
                  _const_spec((d, 2 * n_heads)),
                  _const_spec((SSD_CONV, conv_dim)),
                  _const_spec((1, conv_dim))],
        out_specs=[pl.BlockSpec((tm, d_inner), lambda i: (i, 0)),
                   pl.BlockSpec((tm, conv_dim), lambda i: (i, 0)),
                   pl.BlockSpec((tm, n_heads), lambda i: (i, 0))],
        out_shape=[jax.ShapeDtypeStruct((t, d_inner), BF16),
                   jax.ShapeDtypeStruct((t, conv_dim), BF16),
                   jax.ShapeDtypeStruct((t, n_heads), F32)],
        scratch_shapes=[pltpu.VMEM((tm + CONV_HALO, conv_dim), F32)],
        compiler_params=_cparams("arbitrary"),
        name="ssd_in",
    )(h, g_norm.reshape(1, d), wz, wx, wdt, conv_w, conv_b.reshape(1, conv_dim))

    q = SSD_CHUNK
    nc = seqlen // q
    xbc3 = xbc.reshape(bsz, seqlen, conv_dim)
    dt3 = dt_raw.reshape(bsz, seqlen, n_heads)
    dtt3 = jnp.swapaxes(dt3, 1, 2)
    nb_x = d_inner // gn
    yg = pl.pallas_call(
        _ssd_scan_kernel,
        grid=(bsz, nc),
        in_specs=[pl.BlockSpec((None, q, d_inner), lambda b, c: (b, c, 0)),
                  pl.BlockSpec((None, q, gn), lambda b, c: (b, c, nb_x)),
                  pl.BlockSpec((None, q, gn), lambda b, c: (b, c, nb_x + 1)),
                  pl.BlockSpec((None, q, d_inner), lambda b, c: (b, c, 0)),
                  pl.BlockSpec((None, q, n_heads), lambda b, c: (b, c, 0)),
                  pl.BlockSpec((None, n_heads, q), lambda b, c: (b, 0, c)),
                  _const_spec((1, n_heads)), _const_spec((n_heads, 1)),
                  _const_spec((1, n_heads)), _const_spec((n_heads, 1)),
                  _const_spec((1, d_inner)), _const_spec((1, d_inner))],
        out_specs=pl.BlockSpec((None, q, d_inner), lambda b, c: (b, c, 0)),
        out_shape=jax.ShapeDtypeStruct((bsz, seqlen, d_inner), BF16),
        scratch_shapes=[pltpu.VMEM((SSD_GROUPS, SSD_STATE, d_inner // SSD_GROUPS), F32)],
        compiler_params=_cparams("arbitrary", "arbitrary"),
        name="ssd_scan",
    )(xbc3, xbc3, xbc3, z.reshape(bsz, seqlen, d_inner), dt3, dtt3,
      dt_bias.reshape(1, n_heads), dt_bias.reshape(n_heads, 1),
      a_log.reshape(1, n_heads), a_log.reshape(n_heads, 1),
      jnp.repeat(d_skip, SSD_HEAD_DIM).reshape(1, d_inner), norm_w.reshape(1, d_inner))
    return _proj_residual(yg.reshape(t, d_inner), w_out.astype(BF16), h)


def _sb_in_kernel(h_ref, g_ref, w_ref, o_ref, *, q_cols, q_scale):
    u = _rms(h_ref[...], g_ref[...]).astype(BF16)
    o_ref[:, :q_cols] = (_dot(u, w_ref[:, :q_cols]) * q_scale).astype(o_ref.dtype)
    o_ref[:, q_cols:] = _dot(u, w_ref[:, q_cols:]).astype(o_ref.dtype)


def _sb_attn_kernel(q_ref, k_ref, v_ref, o_ref, acc_ref, *, tile):
    seqlen = q_ref.shape[0]
    nq = seqlen // tile
    pw = 2 * SB_HEAD_DIM
    n_pairs = q_ref.shape[1] // pw
    lane = lax.broadcasted_iota(jnp.int32, (1, pw), 1)
    first = lane < SB_HEAD_DIM
    row = lax.broadcasted_iota(jnp.int32, (tile, tile), 0)
    col = lax.broadcasted_iota(jnp.int32, (tile, tile), 1)
    neg_from = jnp.where(row >= col, -1.0, 0.0).astype(BF16)
    past = col < row
    past2 = jnp.concatenate([past, past], axis=0)

    def k_tile(qs, kj, csums, diagonal):
        rows = pl.ds(pl.multiple_of(kj * tile, tile), tile)
        out = []
        for p in range(n_pairs):
            k = k_ref[rows, p * pw:(p + 1) * pw]
            v = v_ref[rows, p * pw:(p + 1) * pw]
            t = _dot_nt(qs[p], k)
            neg_abs = pltpu.bitcast(pltpu.bitcast(t, jnp.uint32) | jnp.uint32(0x80000000), F32)
            sp = jnp.maximum(t, 0.0) + jnp.log(1.0 + jnp.exp2(neg_abs)) * _LOG2_E
            if diagonal:
                sp = jnp.where(past2, sp, 0.0)
            w = jnp.exp2(t + _dot(sp.astype(BF16), neg_from) + csums[p])
            if diagonal:
                w = jnp.where(past2, w, 0.0)
            acc_ref[p] += _dot(w.astype(BF16), v)
            out.append(csums[p] - jnp.sum(sp, axis=1, keepdims=True))
        return tuple(out)

    def q_body(qi, _):
        qrows = pl.ds(pl.multiple_of(qi * tile, tile), tile)
        qs = []
        for p in range(n_pairs):
            q = q_ref[qrows, p * pw:(p + 1) * pw]
            zero = jnp.zeros_like(q)
            qs.append(jnp.concatenate([jnp.where(first, q, zero), jnp.where(first, zero, q)], axis=0))
            acc_ref[p] = jnp.zeros(acc_ref.shape[1:], F32)
        init = tuple(jnp.zeros((2 * tile, 1), F32) for _ in range(n_pairs))
        csums = k_tile(qs, qi, init, True)
        lax.fori_loop(0, qi, lambda s, c: k_tile(qs, qi - 1 - s, c, False), csums)
        for p in range(n_pairs):
            o = jnp.where(first, acc_ref[p, 0:tile, :], acc_ref[p, tile:2 * tile, :])
            o_ref[qrows, p * pw:(p + 1) * pw] = o.astype(o_ref.dtype)
        return 0

    lax.fori_loop(0, nq, q_body, 0)


def _sb_mixer(h, bsz, seqlen, g_norm, w_in, w_out):
    t, d = h.shape
    hd = w_in.shape[1] // 3
    tm = ROW_TILE
    qkv = pl.pallas_call(
        functools.partial(_sb_in_kernel, q_cols=hd, q_scale=SB_HEAD_DIM ** -0.5 * _LOG2_E),
        grid=(t // tm,),
        in_specs=[pl.BlockSpec((tm, d), lambda i: (i, 0)),
                  _const_spec((1, d)),
                  _const_spec((d, 3 * hd))],
        out_specs=pl.BlockSpec((tm, 3 * hd), lambda i: (i, 0)),
        out_shape=jax.ShapeDtypeStruct((t, 3 * hd), BF16),
        compiler_params=_cparams("arbitrary"),
        name="sb_in",
    )(h, g_norm.reshape(1, d), w_in.astype(BF16))
    qkv3 = qkv.reshape(bsz, seqlen, 3 * hd)
    bw = SB_PAIRS_PER_STEP * 2 * SB_HEAD_DIM
    nb = hd // bw
    tile = min(ATTN_TILE, seqlen)
    o = pl.pallas_call(
        functools.partial(_sb_attn_kernel, tile=tile),
        grid=(bsz, nb),
        in_specs=[pl.BlockSpec((None, seqlen, bw), lambda b, j: (b, 0, j)),
                  pl.BlockSpec((None, seqlen, bw), lambda b, j: (b, 0, nb + j)),
                  pl.BlockSpec((None, seqlen, bw), lambda b, j: (b, 0, 2 * nb + j))],
        out_specs=pl.BlockSpec((None, seqlen, bw), lambda b, j: (b, 0, j)),
        out_shape=jax.ShapeDtypeStruct((bsz, seqlen, hd), BF16),
        scratch_shapes=[pltpu.VMEM((SB_PAIRS_PER_STEP, 2 * tile, 2 * SB_HEAD_DIM), F32)],
        compiler_params=_cparams("arbitrary", "arbitrary"),
        name="sb_attn",
    )(qkv3, qkv3, qkv3)
    return _proj_residual(o.reshape(t, hd), w_out.astype(BF16), h)


def _lru_kernel(h_ref, g_ref, win_ref, cw_ref, cb_ref, wband_ref, br_ref, bi_ref, lam_ref,
                y_ref, pad_ref, xc_ref, a_ref, b_ref, hs_ref, carry_ref, *, band_starts):
    bsz, tt, d = h_ref.shape
    width = y_ref.shape[2]

    @pl.when(pl.program_id(0) == 0)
    def _():
        carry_ref[...] = jnp.zeros(carry_ref.shape, F32)
        for b in range(bsz):
            pad_ref[b, 0:CONV_HALO, :] = jnp.zeros((CONV_HALO, width), F32)

    u = _rms(h_ref[...].reshape(bsz * tt, d), g_ref[...]).astype(BF16)
    proj = _dot(u, win_ref[...])
    gate_branch = proj[:, :width]
    for b in range(bsz):
        pad_ref[b, CONV_HALO:CONV_HALO + tt, :] = proj[b * tt:(b + 1) * tt, width:]
        acc = cb_ref[...] + cw_ref[LRU_CONV - 1:LRU_CONV, :] * pad_ref[b, CONV_HALO:CONV_HALO + tt, :]
        for k in range(LRU_CONV - 1):
            off = CONV_HALO - (LRU_CONV - 1) + k
            acc = acc + cw_ref[k:k + 1, :] * pad_ref[b, off:off + tt, :]
        xc_ref[b * tt:(b + 1) * tt, :] = acc
        pad_ref[b, 0:CONV_HALO, :] = pad_ref[b, tt:tt + CONV_HALO, :]

    xcb = xc_ref[...].astype(BF16)
    n_lt = width // V7X_LANES
    pitch = a_ref.shape[1] // bsz
    neg_c_softplus = -LRU_C * _softplus(-lam_ref[...])
    for c in range(n_lt):
        cs = slice(c * V7X_LANES, (c + 1) * V7X_LANES)
        pre = _dot(xcb[:, band_starts[c]:band_starts[c] + LRU_BAND], wband_ref[c])
        r = _sigmoid(pre[:, :V7X_LANES] + br_ref[:, cs])
        ig = _sigmoid(pre[:, V7X_LANES:] + bi_ref[:, cs])
        a = jnp.exp(neg_c_softplus[:, cs] * r)
        bterm = jnp.sqrt(1.0 - a * a) * (ig * xc_ref[:, cs])
        for b in range(bsz):
            a_ref[c, b * pitch:b * pitch + tt, :] = a[b * tt:(b + 1) * tt, :]
            b_ref[c, b * pitch:b * pitch + tt, :] = bterm[b * tt:(b + 1) * tt, :]

    def step(j, hprev):
        hnew = []
        for c in range(n_lt):
            rows_j = pl.ds(j, bsz, stride=pitch)
            hc = a_ref[c, rows_j, :] * hprev[c] + b_ref[c, rows_j, :]
            hs_ref[c, rows_j, :] = hc
            hnew.append(hc)
        return tuple(hnew)

    hlast = lax.fori_loop(0, tt, step, tuple(carry_ref[c] for c in range(n_lt)), unroll=LRU_SCAN_UNROLL)
    for c in range(n_lt):
        carry_ref[c] = hlast[c]
    c0 = math.sqrt(2.0 / math.pi)
    gelu = 0.5 * gate_branch * (1.0 + jnp.tanh(c0 * (gate_branch + 0.044715 * gate_branch ** 3)))
    for b in range(bsz):
        hs_b = jnp.concatenate([hs_ref[c, b * pitch:b * pitch + tt, :] for c in range(n_lt)], axis=1)
        y_ref[b] = (hs_b * gelu[b * tt:(b + 1) * tt, :]).astype(y_ref.dtype)


def _lru_band_starts(width):
    bd = width // LRU_BLOCKS
    starts = []
    for j in range(width // V7X_LANES):
        b_lo = (j * V7X_LANES) // bd
        b_hi = ((j + 1) * V7X_LANES - 1) // bd
        lo = (b_lo * bd) // V7X_LANES * V7X_LANES
        assert (b_hi + 1) * bd - lo <= LRU_BAND
        starts.append(min(lo, width - LRU_BAND))
    return tuple(starts)


def _lru_band_weights(w_rg, w_ig, starts):
    bd = w_rg.shape[1]
    row = np.asarray(starts)[:, None, None] + np.arange(LRU_BAND)[None, :, None]
    col = (np.arange(len(starts)) * V7X_LANES)[:, None, None] + np.arange(V7X_LANES)[None, None, :]
    blk = col // bd
    inside = (row // bd) == blk
    blk, r_in, c_in = np.broadcast_arrays(blk, np.clip(row - blk * bd, 0, bd - 1), col - blk * bd)

    def band(w):
        return jnp.where(inside, w[blk, r_in, c_in], 0.0)

    return jnp.concatenate([band(w_rg), band(w_ig)], axis=2).astype(BF16)


def _lru_mixer(h, bsz, seqlen, g_norm, w_in, conv_w, conv_b, w_rg, b_rg, w_ig, b_ig, lam, w_out):
    t, d = h.shape
    width = lam.shape[0]
    tt = min(LRU_TIME_TILE, seqlen)
    rows = bsz * tt
    n_lt = width // V7X_LANES
    scan_rows = bsz * (tt + LRU_PITCH_PAD)
    starts = _lru_band_starts(width)
    y = pl.pallas_call(
        functools.partial(_lru_kernel, band_starts=starts),
        grid=(seqlen // tt,),
        in_specs=[pl.BlockSpec((bsz, tt, d), lambda i: (0, i, 0)),
                  _const_spec((1, d)),
                  _const_spec((d, 2 * width)),
                  _const_spec((LRU_CONV, width)), _const_spec((1, width)),
                  _const_spec((n_lt, LRU_BAND, 2 * V7X_LANES)),
                  _const_spec((1, width)), _const_spec((1, width)),
                  _const_spec((1, width))],
        out_specs=pl.BlockSpec((bsz, tt, width), lambda i: (0, i, 0)),
        out_shape=jax.ShapeDtypeStruct((bsz, seqlen, width), BF16),
        scratch_shapes=[pltpu.VMEM((bsz, tt + CONV_HALO, width), F32),
                        pltpu.VMEM((rows, width), F32),
                        pltpu.VMEM((n_lt, scan_rows, V7X_LANES), F32),
                        pltpu.VMEM((n_lt, scan_rows, V7X_LANES), F32),
                        pltpu.VMEM((n_lt, scan_rows, V7X_LANES), F32),
                        pltpu.VMEM((n_lt, bsz, V7X_LANES), F32)],
        compiler_params=_cparams("arbitrary"),
        name="lru",
    )(h.reshape(bsz, seqlen, d), g_norm.reshape(1, d), w_in.astype(BF16),
      conv_w, conv_b.reshape(1, width), _lru_band_weights(w_rg, w_ig, starts),
      b_rg.reshape(1, width), b_ig.reshape(1, width), lam.reshape(1, width))
    return _proj_residual(y.reshape(t, width), w_out.astype(BF16), h)


def _diff_in_kernel(h_ref, g_ref, w_ref, cos_ref, sin_ref, qn_ref, kn_ref, q_ref, k_ref, v_ref,
                    *, q_scale):
    d_q = q_ref.shape[1]
    hdim = DIFF_HEAD_DIM
    n_grp = d_q // hdim
    u = _rms(h_ref[...], g_ref[...]).astype(BF16)
    v_ref[...] = _dot(u, w_ref[:, 2 * d_q:]).astype(v_ref.dtype)

    ci = lax.broadcasted_iota(jnp.int32, (d_q, n_grp), 0)
    gi = lax.broadcasted_iota(jnp.int32, (d_q, n_grp), 1)
    gsum = (ci // hdim == gi).astype(BF16)
    gj = lax.broadcasted_iota(jnp.int32, (2 * n_grp, d_q), 0)
    cj = lax.broadcasted_iota(jnp.int32, (2 * n_grp, d_q), 1)
    gexp = (cj // hdim == gj % n_grp).astype(BF16)
    lane = lax.broadcasted_iota(jnp.int32, (1, d_q), 1)
    first_half = (lane % hdim) < (hdim // 2)
    reps = d_q // cos_ref.shape[1]
    cos_t = jnp.tile(cos_ref[...], (1, reps))
    sin_t = jnp.tile(sin_ref[...], (1, reps))

    def norm_rope(x, nw):
        sq = x * x
        sq_hi = sq.astype(BF16)
        sq_lo = (sq - sq_hi.astype(F32)).astype(BF16)
        ms = (_dot(sq_hi, gsum) + _dot(sq_lo, gsum)) * (1.0 / hdim)
        inv = lax.rsqrt(ms + RMS_EPS)
        inv_hi = inv.astype(BF16)
        inv_lo = (inv - inv_hi.astype(F32)).astype(BF16)
        inv_full = _dot(jnp.concatenate([inv_hi, inv_lo], axis=1), gexp)
        xn = x * inv_full * nw
        rot = jnp.where(first_half, pltpu.roll(xn, d_q - hdim // 2, 1), pltpu.roll(xn, hdim // 2, 1))
        return xn * cos_t + rot * sin_t

    q_ref[...] = (norm_rope(_dot(u, w_ref[:, :d_q]), qn_ref[...]) * q_scale).astype(q_ref.dtype)
    k_ref[...] = norm_rope(_dot(u, w_ref[:, d_q:2 * d_q]), kn_ref[...]).astype(k_ref.dtype)


def _diff_attn_kernel(q_ref, k_ref, v_ref, lq1_ref, lk1_ref, lq2_ref, lk2_ref, sn_ref, o_ref,
                      acc_ref, *, tile, lambda_init):
    seqlen = q_ref.shape[0]
    nq = seqlen // tile
    hdim = DIFF_HEAD_DIM
    hw = 2 * hdim
    n_heads = q_ref.shape[1] // hw
    lane = lax.broadcasted_iota(jnp.int32, (1, hw), 1)
    first = lane < hdim
    row = lax.broadcasted_iota(jnp.int32, (tile, tile), 0)
    col = lax.broadcasted_iota(jnp.int32, (tile, tile), 1)
    causal = col <= row
    causal2 = jnp.concatenate([causal, causal], axis=0)
    ones = jnp.ones((tile, hw), BF16)
    lam = (jnp.exp(jnp.sum(lq1_ref[...] * lk1_ref[...], axis=1, keepdims=True))
           - jnp.exp(jnp.sum(lq2_ref[...] * lk2_ref[...], axis=1, keepdims=True)) + lambda_init)

    def k_tile(qs, kj, maxes, diagonal):
        rows = pl.ds(pl.multiple_of(kj * tile, tile), tile)
        out = []
        for hd in range(n_heads):
            k = k_ref[rows, hd * hw:(hd + 1) * hw]
            v1 = jnp.concatenate([v_ref[rows, hd * hw:(hd + 1) * hw], ones], axis=1)
            s = _dot_nt(qs[hd], k)
            if diagonal:
                s = jnp.where(causal2, s, -jnp.inf)
            mx_new = jnp.maximum(maxes[hd], jnp.max(s, axis=1, keepdims=True))
            alpha = jnp.exp2(maxes[hd] - mx_new)
            pr = jnp.exp2(s - mx_new)
            acc_ref[hd] = acc_ref[hd] * alpha + _dot(pr.astype(BF16), v1)
            out.append(mx_new)
        return tuple(out)

    def q_body(qi, _):
        qrows = pl.ds(pl.multiple_of(qi * tile, tile), tile)
        qs = []
        for hd in range(n_heads):
            q = q_ref[qrows, hd * hw:(hd + 1) * hw]
            zero = jnp.zeros_like(q)
            qs.append(jnp.concatenate([jnp.where(first, q, zero), jnp.where(first, zero, q)], axis=0))
            acc_ref[hd] = jnp.zeros(acc_ref.shape[1:], F32)
        init = tuple(jnp.full((2 * tile, 1), -jnp.inf, F32) for _ in range(n_heads))
        maxes = k_tile(qs, qi, init, True)
        lax.fori_loop(0, qi, lambda s, c: k_tile(qs, qi - 1 - s, c, False), maxes)
        for hd in range(n_heads):
            p = acc_ref[hd, :, 0:hw] / acc_ref[hd, :, hw:2 * hw]
            o = p[0:tile] - lam * p[tile:2 * tile]
            o = _rms(o, sn_ref[...]) * (1.0 - lambda_init)
            o_ref[qrows, hd * hw:(hd + 1) * hw] = o.astype(o_ref.dtype)
        return 0

    lax.fori_loop(0, nq, q_body, 0)


def _diff_mixer(h, bsz, seqlen, cos_t, sin_t, g_norm, w_in, q_norm, k_norm, lq1, lk1, lq2, lk2,
                sub_norm, w_out, lambda_init):
    t, d = h.shape
    d_q = w_in.shape[1] // 3
    hdim = DIFF_HEAD_DIM
    tm = ROW_TILE
    tw = cos_t.shape[1]
    q, k, v = pl.pallas_call(
        functools.partial(_diff_in_kernel, q_scale=hdim ** -0.5 * _LOG2_E),
        grid=(t // tm,),
        in_specs=[pl.BlockSpec((tm, d), lambda i: (i, 0)),
                  _const_spec((1, d)),
                  _const_spec((d, 3 * d_q)),
                  pl.BlockSpec((tm, tw), lambda i: (i, 0)),
                  pl.BlockSpec((tm, tw), lambda i: (i, 0)),
                  _const_spec((1, d_q)), _const_spec((1, d_q))],
        out_specs=[pl.BlockSpec((tm, d_q), lambda i: (i, 0))] * 3,
        out_shape=[jax.ShapeDtypeStruct((t, d_q), BF16)] * 3,
        compiler_params=_cparams("arbitrary"),
        name="diff_in",
    )(h, g_norm.reshape(1, d), w_in.astype(BF16), cos_t, sin_t,
      jnp.tile(q_norm, d_q // hdim).reshape(1, d_q), jnp.tile(k_norm, d_q // hdim).reshape(1, d_q))
    hw = 2 * hdim
    bw = DIFF_HEADS_PER_STEP * hw
    tile = min(ATTN_TILE, seqlen)
    spec = pl.BlockSpec((None, seqlen, bw), lambda b, j: (b, 0, j))
    vec = _const_spec((1, hdim))
    o = pl.pallas_call(
        functools.partial(_diff_attn_kernel, tile=tile, lambda_init=lambda_init),
        grid=(bsz, d_q // bw),
        in_specs=[spec, spec, spec, vec, vec, vec, vec, _const_spec((1, hw))],
        out_specs=spec,
        out_shape=jax.ShapeDtypeStruct((bsz, seqlen, d_q), BF16),
        scratch_shapes=[pltpu.VMEM((DIFF_HEADS_PER_STEP, 2 * tile, 2 * hw), F32)],
        compiler_params=_cparams("arbitrary", "arbitrary"),
        name="diff_attn",
    )(q.reshape(bsz, seqlen, d_q), k.reshape(bsz, seqlen, d_q), v.reshape(bsz, seqlen, d_q),
      lq1.reshape(1, hdim), lk1.reshape(1, hdim), lq2.reshape(1, hdim), lk2.reshape(1, hdim),
      sub_norm.reshape(1, hw))
    return _proj_residual(o.reshape(t, d_q), w_out.astype(BF16), h)


_R_E1, _R_E2, _R_W1, _R_W2, _R_RANK1, _R_RANK2 = range(6)
_C_LOC1, _C_LOC2, _C_W1, _C_W2 = range(4)


def _router_kernel(h_ref, g_ref, w_ref, b_ref, rec_ref, base_ref, cnt_ref, run_ref):
    tm = h_ref.shape[0]
    ne, ng, epg = MOE_EXPERTS, MOE_GROUPS, MOE_EXPERTS_PER_GROUP

    @pl.when(pl.program_id(0) == 0)
    def _():
        run_ref[...] = jnp.zeros(run_ref.shape, F32)

    base_ref[...] = run_ref[...]

    u = _rms(h_ref[...], g_ref[...])
    logits = _dot_split(u, u.astype(BF16), w_ref) + b_ref[...]
    lane = lax.broadcasted_iota(jnp.int32, logits.shape, 1)
    lane_f = lane.astype(F32)
    lane_group = (lane // epg).astype(F32)
    neg = -jnp.inf
    big = float(V7X_LANES)

    gl = jnp.where((lane >= ne) & (lane < ne + ng), logits, neg)
    gmax = jnp.max(gl, axis=1, keepdims=True)
    gsel = jnp.min(jnp.where(gl == gmax, lane_f - ne, big), axis=1, keepdims=True)
    g_gate = 1.0 / jnp.sum(jnp.exp(gl - gmax), axis=1, keepdims=True)

    el = jnp.where((lane < ne) & (lane_group == gsel), logits, neg)
    m1 = jnp.max(el, axis=1, keepdims=True)
    i1 = jnp.min(jnp.where(el == m1, lane_f, big), axis=1, keepdims=True)
    el2 = jnp.where(lane_f == i1, neg, el)
    m2 = jnp.max(el2, axis=1, keepdims=True)
    i2 = jnp.min(jnp.where(el2 == m2, lane_f, big), axis=1, keepdims=True)
    e21 = jnp.exp(m2 - m1)
    w1 = g_gate / (1.0 + e21)
    w2 = g_gate * e21 / (1.0 + e21)

    oh1 = lane_f == i1
    oh2 = lane_f == i2
    oh = jnp.where(oh1 | oh2, 1.0, 0.0)
    row = lax.broadcasted_iota(jnp.int32, (tm, tm), 0)
    col = lax.broadcasted_iota(jnp.int32, (tm, tm), 1)
    before = _dot((col < row).astype(BF16), oh.astype(BF16))
    pos = run_ref[0:1, :] + before
    rank1 = jnp.sum(jnp.where(oh1, pos, 0.0), axis=1, keepdims=True)
    rank2 = jnp.sum(jnp.where(oh2, pos, 0.0), axis=1, keepdims=True)
    run_ref[...] = run_ref[...] + jnp.sum(oh, axis=0, keepdims=True)
    cnt_ref[...] = run_ref[...]

    rec = jnp.zeros(logits.shape, F32)
    for idx, val in ((_R_E1, i1), (_R_E2, i2), (_R_W1, w1), (_R_W2, w2),
                     (_R_RANK1, rank1), (_R_RANK2, rank2)):
        rec = jnp.where(lane == idx, val, rec)
    rec_ref[...] = rec


def _pack_bf16_pair(x):
    n = x.shape[1] // 2
    xr = x.astype(BF16).astype(F32)
    return pltpu.bitcast(xr[:, :n], jnp.uint32) | (pltpu.bitcast(xr[:, n:], jnp.uint32) >> 16)


def _unpack_bf16_pair(w):
    hi = pltpu.bitcast(w & jnp.uint32(0xFFFF0000), F32)
    lo = pltpu.bitcast(w << 16, F32)
    return jnp.concatenate([hi, lo], axis=1)


def _chunk_rows(chunk):
    return pl.ds(pl.multiple_of(chunk * MOE_CHUNK, MOE_CHUNK), MOE_CHUNK)


def _start_chunk_copies(chunk_of_ref, tile, n_chunks, copy_of):
    def issue(j, _):
        copy_of(_chunk_rows(j), _chunk_rows(chunk_of_ref[tile * n_chunks + j])).start()
        return 0

    lax.fori_loop(0, n_chunks, issue, 0, unroll=DMA_LOOP_UNROLL)


def _wait_chunk_copies(n_chunks, copy_of):
    def drain(j, _):
        copy_of(_chunk_rows(0), _chunk_rows(0)).wait()
        return 0

    lax.fori_loop(0, n_chunks, drain, 0, unroll=DMA_LOOP_UNROLL)


def _dispatch_kernel(chunk_of_ref, h_ref, g_ref, loct_ref, rows_in_ref, rows_ref, sorted_ref, sems):
    del rows_in_ref
    i, nt = pl.program_id(0), pl.num_programs(0)
    slot = i % 2
    tm = h_ref.shape[0]
    n_chunks = sorted_ref.shape[1] // MOE_CHUNK

    def copy_of(slot_):
        return lambda lr, gr: pltpu.make_async_copy(sorted_ref.at[slot_, lr], rows_ref.at[gr],
                                                    sems.at[slot_])

    @pl.when(i >= 2)
    def _():
        _wait_chunk_copies(n_chunks, copy_of(slot))

    u = _rms(h_ref[...], g_ref[...]).astype(BF16)
    local_row = lax.broadcasted_iota(jnp.int32, (sorted_ref.shape[1], tm), 0)
    loc = loct_ref[...]
    pick = jnp.where(local_row == loc[0:1, :], 1.0, jnp.where(local_row == loc[1:2, :], 1.0, 0.0))
    sorted_ref[slot] = _pack_bf16_pair(_dot(pick.astype(BF16), u))
    _start_chunk_copies(chunk_of_ref, i, n_chunks, copy_of(slot))

    @pl.when(i == nt - 1)
    def _():
        _wait_chunk_copies(n_chunks, copy_of(slot))

    @pl.when(jnp.logical_and(i == nt - 1, i >= 1))
    def _():
        _wait_chunk_copies(n_chunks, copy_of(1 - slot))


def _expert_kernel(be_ref, nu_ref, nxt_ref, x_ref, wg_hbm, wu_hbm, wd_hbm, y_ref,
                   wgf_ref, wuf_ref, wdf_ref, wgb_ref, wub_ref, wdb_ref, sems, *, layer):
    i = pl.program_id(0)

    def weight_copies(e):
        return (pltpu.make_async_copy(wg_hbm.at[layer, e], wgf_ref, sems.at[0]),
                pltpu.make_async_copy(wu_hbm.at[layer, e], wuf_ref, sems.at[1]),
                pltpu.make_async_copy(wd_hbm.at[layer, e], wdf_ref, sems.at[2]))

    @pl.when(i < nu_ref[0])
    def _():
        e = be_ref[i]

        @pl.when(i == 0)
        def _():
            for c in weight_copies(e):
                c.start()

        @pl.when(jnp.logical_or(i == 0, e != be_ref[jnp.maximum(i - 1, 0)]))
        def _():
            for c in weight_copies(e):
                c.wait()
            wgb_ref[...] = wgf_ref[...].astype(BF16)
            wub_ref[...] = wuf_ref[...].astype(BF16)
            wdb_ref[...] = wdf_ref[...].astype(BF16)
            nxt = nxt_ref[i]

            @pl.when(nxt >= 0)
            def _():
                for c in weight_copies(nxt):
                    c.start()

        x = _unpack_bf16_pair(x_ref[...]).astype(BF16)
        gate = _dot(x, wgb_ref[...])
        up = _dot(x, wub_ref[...])
        hid = (gate * _sigmoid(gate) * up).astype(BF16)
        y_ref[...] = _pack_bf16_pair(_dot(hid, wdb_ref[...]))

    @pl.when(i >= nu_ref[0])
    def _():
        y_ref[...] = jnp.zeros(y_ref.shape, y_ref.dtype)


def _combine_kernel(chunk_of_ref, h_ref, crec_ref, rows_ref, o_ref, sorted_ref, sems):
    i, nt = pl.program_id(0), pl.num_programs(0)
    slot = i % 2
    tm = h_ref.shape[0]
    n_chunks = sorted_ref.shape[1] // MOE_CHUNK

    def copy_of(slot_):
        return lambda lr, gr: pltpu.make_async_copy(rows_ref.at[gr], sorted_ref.at[slot_, lr],
                                                    sems.at[slot_])

    @pl.when(i == 0)
    def _():
        _start_chunk_copies(chunk_of_ref, i, n_chunks, copy_of(slot))

    @pl.when(i + 1 < nt)
    def _():
        _start_chunk_copies(chunk_of_ref, i + 1, n_chunks, copy_of(1 - slot))

    _wait_chunk_copies(n_chunks, copy_of(slot))
    crec = crec_ref[...]
    loc0 = crec[:, _C_LOC1:_C_LOC1 + 1].astype(jnp.int32)
    loc1 = crec[:, _C_LOC2:_C_LOC2 + 1].astype(jnp.int32)
    w1 = crec[:, _C_W1:_C_W1 + 1]
    w2 = crec[:, _C_W2:_C_W2 + 1]
    local_row = lax.broadcasted_iota(jnp.int32, (tm, sorted_ref.shape[1]), 1)
    weights = jnp.where(local_row == loc0, w1, jnp.where(local_row == loc1, w2, 0.0)).astype(BF16)
    y = _unpack_bf16_pair(sorted_ref[slot]).astype(BF16)
    o_ref[...] = h_ref[...] + _dot(weights, y)


def _hier_moe(h, g_norm, w_group, b_group, w_expert, b_expert, layer, w_gate, w_up, w_down):
    t, d = h.shape
    ne, ng = MOE_EXPERTS, MOE_GROUPS
    ff = w_gate.shape[3]
    dp = d // 2
    tm = min(MOE_TOKEN_TILE, t)
    g2 = g_norm.reshape(1, d)
    lane_pad = V7X_LANES - ne - ng
    w_router = _split_weight(
        jnp.concatenate([w_expert, w_group, jnp.zeros((d, lane_pad), F32)], axis=1))
    b_router = jnp.concatenate([b_expert, b_group, jnp.zeros((lane_pad,), F32)]).reshape(1, V7X_LANES)
    nt = t // tm
    rec, base, cnt = pl.pallas_call(
        _router_kernel,
        grid=(nt,),
        in_specs=[pl.BlockSpec((tm, d), lambda i: (i, 0)),
                  _const_spec((1, d)),
                  _const_spec((d, 2 * V7X_LANES)),
                  _const_spec((1, V7X_LANES))],
        out_specs=[pl.BlockSpec((tm, V7X_LANES), lambda i: (i, 0)),
                   pl.BlockSpec((V7X_SUBLANES, V7X_LANES), lambda i: (i, 0)),
                   _const_spec((V7X_SUBLANES, V7X_LANES))],
        out_shape=[jax.ShapeDtypeStruct((t, V7X_LANES), F32),
                   jax.ShapeDtypeStruct((nt * V7X_SUBLANES, V7X_LANES), F32),
                   jax.ShapeDtypeStruct((V7X_SUBLANES, V7X_LANES), F32)],
        scratch_shapes=[pltpu.VMEM((V7X_SUBLANES, V7X_LANES), F32)],
        compiler_params=_cparams("arbitrary"),
        name="moe_router",
    )(h, g2, w_router, b_router)

    br, ck = MOE_ROW_BLOCK, MOE_CHUNK
    n_blocks = -(-(2 * t + nt * ne * (ck - 1) + ne * (br - 1)) // br)
    n_rows = n_blocks * br
    local_rows = 2 * tm + ne * ck
    before = base.reshape(nt, V7X_SUBLANES, V7X_LANES)[:, 0, :ne].astype(jnp.int32)
    total = cnt[0, :ne].astype(jnp.int32)
    tile_cnt = jnp.concatenate([before[1:], total[None, :]], axis=0) - before
    seg = (tile_cnt + ck - 1) // ck * ck
    region = jnp.sum(seg, axis=0)
    padded = (region + br - 1) // br * br
    pad_end = jnp.cumsum(padded)
    pad_start = pad_end - padded
    seg_global = pad_start[None, :] + jnp.cumsum(seg, axis=0) - seg
    seg_local = jnp.cumsum(seg, axis=1) - seg
    eid = rec[:, _R_E1:_R_E2 + 1].astype(jnp.int32)
    rank = rec[:, _R_RANK1:_R_RANK2 + 1].astype(jnp.int32)
    experts = jnp.arange(ne, dtype=jnp.int32)
    shift = jnp.repeat(seg_local - before, tm, axis=0)
    loc = rank + jnp.sum(jnp.where(eid[:, :, None] == experts, shift[:, None, :], 0), axis=-1)
    loct = jnp.concatenate([jnp.swapaxes(loc.reshape(nt, tm, 2), 1, 2),
                            jnp.full((nt, V7X_SUBLANES - 2, tm), -1, jnp.int32)], axis=1)
    crec = jnp.concatenate([loc.astype(F32), rec[:, _R_W1:_R_W2 + 1],
                            jnp.zeros((t, V7X_SUBLANES - 4), F32)], axis=1)
    n_local = local_rows // ck
    j = jnp.arange(n_local, dtype=jnp.int32)
    seg_end = (seg_local + seg) // ck
    owner = jnp.sum((seg_end[:, None, :] <= j[None, :, None]).astype(jnp.int32), axis=-1)
    offset = (seg_global - seg_local) // ck
    chunk_of = j[None, :] + jnp.sum(jnp.where(owner[:, :, None] == experts, offset[:, None, :], 0), axis=-1)
    in_use = owner < ne
    parity = (jnp.arange(nt, dtype=jnp.int32) % 2)[:, None]
    scatter_chunk = jnp.where(in_use, chunk_of, n_rows // ck + parity * n_local + j[None, :]).reshape(-1)
    gather_chunk = jnp.where(in_use, chunk_of, 0).reshape(-1)
    block_row0 = jnp.arange(n_blocks, dtype=jnp.int32) * br
    block_expert = jnp.minimum(
        jnp.sum((pad_end[None, :] <= block_row0[:, None]).astype(jnp.int32), axis=1), ne - 1)
    n_used = (pad_end[-1:] // br).astype(jnp.int32)
    blocks = jnp.arange(n_blocks, dtype=jnp.int32)
    later_run = ((block_expert[None, :] != block_expert[:, None]) & (blocks[None, :] > blocks[:, None])
                 & (blocks[None, :] < n_used))
    next_expert = jnp.where(jnp.any(later_run, axis=1),
                            block_expert[jnp.argmax(later_run, axis=1)], -1).astype(jnp.int32)

    x_rows = pl.pallas_call(
        _dispatch_kernel,
        grid_spec=pltpu.PrefetchScalarGridSpec(
            num_scalar_prefetch=1,
            grid=(nt,),
            in_specs=[pl.BlockSpec((tm, d), lambda i, *_: (i, 0)),
                      _const_spec((1, d)),
                      pl.BlockSpec((None, V7X_SUBLANES, tm), lambda i, *_: (i, 0, 0)),
                      pl.BlockSpec(memory_space=pl.ANY)],
            out_specs=pl.BlockSpec(memory_space=pl.ANY),
            scratch_shapes=[pltpu.VMEM((2, local_rows, dp), jnp.uint32), pltpu.SemaphoreType.DMA((2,))]),
        out_shape=jax.ShapeDtypeStruct((n_rows + 2 * local_rows, dp), jnp.uint32),
        input_output_aliases={4: 0},
        compiler_params=_cparams("arbitrary"),
        name="moe_dispatch",
    )(scatter_chunk, h, g2, loct, jnp.zeros((n_rows + 2 * local_rows, dp), jnp.uint32))

    def blk(i, be, nu, nxt):
        return jnp.maximum(jnp.minimum(i, nu[0] - 1), 0)

    hbm = pl.BlockSpec(memory_space=pl.ANY)
    y_rows = pl.pallas_call(
        functools.partial(_expert_kernel, layer=layer),
        grid_spec=pltpu.PrefetchScalarGridSpec(
            num_scalar_prefetch=3,
            grid=(n_blocks,),
            in_specs=[pl.BlockSpec((br, dp), lambda i, be, nu, nxt: (blk(i, be, nu, nxt), 0)),
                      hbm, hbm, hbm],
            out_specs=pl.BlockSpec((br, dp), lambda i, be, nu, nxt: (i, 0)),
            scratch_shapes=[pltpu.VMEM((d, ff), F32), pltpu.VMEM((d, ff), F32), pltpu.VMEM((ff, d), F32),
                            pltpu.VMEM((d, ff), BF16), pltpu.VMEM((d, ff), BF16), pltpu.VMEM((ff, d), BF16),
                            pltpu.SemaphoreType.DMA((3,))]),
        out_shape=jax.ShapeDtypeStruct((n_rows, dp), jnp.uint32),
        compiler_params=_cparams("arbitrary"),
        name="moe_experts",
    )(block_expert, n_used, next_expert, x_rows, w_gate, w_up, w_down)

    return pl.pallas_call(
        _combine_kernel,
        grid_spec=pltpu.PrefetchScalarGridSpec(
            num_scalar_prefetch=1,
            grid=(nt,),
            in_specs=[pl.BlockSpec((tm, d), lambda i, *_: (i, 0)),
                      pl.BlockSpec((tm, V7X_SUBLANES), lambda i, *_: (i, 0)),
                      pl.BlockSpec(memory_space=pl.ANY)],
            out_specs=pl.BlockSpec((tm, d), lambda i, *_: (i, 0)),
            scratch_shapes=[pltpu.VMEM((2, local_rows, dp), jnp.uint32), pltpu.SemaphoreType.DMA((2,))]),
        out_shape=jax.ShapeDtypeStruct((t, d), F32),
        compiler_params=_cparams("arbitrary"),
        name="moe_combine",
    )(gather_chunk, h, crec, y_rows)


def _rope_tables(positions):
    half = DIFF_HEAD_DIM // 2
    inv = ROPE_THETA ** (-jnp.arange(0, DIFF_HEAD_DIM, 2, dtype=F32) / DIFF_HEAD_DIM)
    ang = positions.astype(F32).reshape(-1, 1) * inv
    cos, sin = jnp.cos(ang), jnp.sin(ang)
    cos_t = jnp.concatenate([cos] * 4, axis=1)
    sin_t = jnp.concatenate([-sin, sin, -sin, sin], axis=1)
    assert cos_t.shape[1] == 4 * half
    return cos_t, sin_t


def kernel(x, positions, norm_mix, norm_ffn, ssd_w_in, ssd_conv_w, ssd_conv_b, ssd_dt_bias, ssd_a_log, ssd_d, ssd_norm, ssd_w_out, sb_w_in, sb_w_out, lru_w_in, lru_conv_w, lru_conv_b, lru_w_rg, lru_b_rg, lru_w_ig, lru_b_ig, lru_lambda, lru_w_out, diff_w_in, diff_q_norm, diff_k_norm, diff_lam_q1, diff_lam_k1, diff_lam_q2, diff_lam_k2, diff_sub_norm, diff_w_out, moe_w_group, moe_b_group, moe_w_expert, moe_b_expert, moe_w_gate, moe_w_up, moe_w_down):
    bsz, seqlen, d = x.shape
    depth = norm_mix.shape[0]
    h = x.reshape(bsz * seqlen, d)
    cos_t, sin_t = _rope_tables(positions)
    for i in range(depth):
        m, j = i % N_MIXERS, i // N_MIXERS
        if m == 0:
            h = _ssd_mixer(h, bsz, seqlen, norm_mix[i], ssd_w_in[j], ssd_conv_w[j], ssd_conv_b[j],
                           ssd_dt_bias[j], ssd_a_log[j], ssd_d[j], ssd_norm[j], ssd_w_out[j])
        elif m == 1:
            h = _sb_mixer(h, bsz, seqlen, norm_mix[i], sb_w_in[j], sb_w_out[j])
        elif m == 2:
            h = _lru_mixer(h, bsz, seqlen, norm_mix[i], lru_w_in[j], lru_conv_w[j], lru_conv_b[j],
                           lru_w_rg[j], lru_b_rg[j], lru_w_ig[j], lru_b_ig[j], lru_lambda[j],
                           lru_w_out[j])
        else:
            lambda_init = 0.8 - 0.6 * math.exp(-0.3 * i)
            h = _diff_mixer(h, bsz, seqlen, cos_t, sin_t, norm_mix[i], diff_w_in[j], diff_q_norm[j],
                            diff_k_norm[j], diff_lam_q1[j], diff_lam_k1[j], diff_lam_q2[j],
                            diff_lam_k2[j], diff_sub_norm[j], diff_w_out[j], lambda_init)
        h = _hier_moe(h, norm_ffn[i], moe_w_group[i], moe_b_group[i], moe_w_expert[i],
                      moe_b_expert[i], i, moe_w_gate, moe_w_up, moe_w_down)
    return h.reshape(bsz, seqlen, d)
```

```python
import functools
import math

import jax
import jax.numpy as jnp
from jax import lax
from jax.experimental import pallas as pl
from jax.experimental.pallas import tpu as pltpu

F32 = jnp.float32
BF16 = jnp.bfloat16
HIGHEST = lax.Precision.HIGHEST

RMS_EPS = 1e-6
_LOG2_E = math.log2(math.e)
N_MIXERS = 4

V7X_VMEM_BYTES = 64 * 1024 * 1024
V7X_LANES = 128
V7X_SUBLANES = 8
VMEM_LIMIT = (V7X_VMEM_BYTES * 7) // 8

SSD_HEAD_DIM = 64
SSD_GROUPS = 8
SSD_STATE = 128
SSD_CONV = 4
SSD_CHUNK = 128
SB_HEAD_DIM = 64
LRU_BLOCKS = 16
LRU_CONV = 4
LRU_C = 8.0
DIFF_HEAD_DIM = 64
ROPE_THETA = 10000.0
MOE_GROUPS = 4
MOE_EXPERTS_PER_GROUP = 8
MOE_EXPERTS = MOE_GROUPS * MOE_EXPERTS_PER_GROUP

ROW_TILE = 512
SSD_IN_TILE = 256
SSD_CONV_CHUNK = 512
ATTN_TILE = 256
SB_PAIRS_PER_STEP = 8
DIFF_HEADS_PER_STEP = 8
LRU_TIME_TILE = 64
LRU_PITCH_PAD = 4
LRU_BAND = 3 * V7X_LANES
LRU_SCAN_UNROLL = 4
MOE_TOKEN_TILE = 512
MOE_ROW_BLOCK = 256
MOE_CHUNK = V7X_SUBLANES
CONV_HALO = V7X_SUBLANES
DMA_LOOP_UNROLL = 8


def _cparams(*sem):
    return pltpu.CompilerParams(dimension_semantics=sem, vmem_limit_bytes=VMEM_LIMIT)


def _const_spec(shape):
    nd = len(shape)
    return pl.BlockSpec(shape, lambda *_: (0,) * nd)


def _rms(x, w):
    ms = jnp.mean(x * x, axis=-1, keepdims=True)
    return x * lax.rsqrt(ms + RMS_EPS) * w


def _sigmoid(x):
    return 1.0 / (1.0 + jnp.exp(-x))


def _softplus(x):
    return jnp.maximum(x, 0.0) + jnp.log(1.0 + jnp.exp(-jnp.abs(x)))


def _dot(a, b, **kw):
    return jnp.dot(a, b, preferred_element_type=F32, **kw)


def _dot_nt(a, b):
    return lax.dot_general(a, b, (((1,), (1,)), ((), ())), preferred_element_type=F32)


def _dot_tn(a, b):
    return lax.dot_general(a, b, (((0,), (0,)), ((), ())), preferred_element_type=F32)


def _split_bf16(x):
    hi = x.astype(BF16)
    return hi, (x - hi.astype(F32)).astype(BF16)


def _split_weight(w):
    return jnp.concatenate(_split_bf16(w), axis=1)


def _dot_split(x, x_hi, w2_ref):
    n = w2_ref.shape[1] // 2
    x_lo = (x - x_hi.astype(F32)).astype(BF16)
    t = _dot(x_hi, w2_ref[...])
    return t[:, :n] + t[:, n:] + _dot(x_lo, w2_ref[:, :n])


def _proj_residual_kernel(y_ref, w_ref, h_ref, o_ref):
    o_ref[...] = h_ref[...] + _dot(y_ref[...], w_ref[...])


def _proj_residual(y, w, h):
    t, k = y.shape
    d = w.shape[1]
    tm = ROW_TILE
    return pl.pallas_call(
        _proj_residual_kernel,
        grid=(t // tm,),
        in_specs=[pl.BlockSpec((tm, k), lambda i: (i, 0)),
                  _const_spec((k, d)),
                  pl.BlockSpec((tm, d), lambda i: (i, 0))],
        out_specs=pl.BlockSpec((tm, d), lambda i: (i, 0)),
        out_shape=jax.ShapeDtypeStruct((t, d), F32),
        compiler_params=_cparams("arbitrary"),
        name="proj_residual",
    )(y, w, h)


def _ssd_in_kernel(h_ref, g_ref, wz_ref, wx_ref, wdt_ref, cw_ref, cb_ref,
                   z_ref, xbc_ref, dt_ref, pad_ref, *, tiles_per_seq):
    i = pl.program_id(0)
    tm = h_ref.shape[0]
    u32 = _rms(h_ref[...], g_ref[...])
    u = u32.astype(BF16)
    z_ref[...] = _dot(u, wz_ref[...]).astype(z_ref.dtype)
    dt_ref[...] = _dot_split(u32, u, wdt_ref)

    @pl.when(i % tiles_per_seq == 0)
    def _():
        pad_ref[0:CONV_HALO, :] = jnp.zeros((CONV_HALO, pad_ref.shape[1]), F32)

    conv_dim = pad_ref.shape[1]
    for c0 in range(0, conv_dim, SSD_CONV_CHUNK):
        cs = slice(c0, c0 + SSD_CONV_CHUNK)
        pad_ref[CONV_HALO:CONV_HALO + tm, cs] = _dot(u, wx_ref[:, cs])
        acc = cb_ref[:, cs] + cw_ref[SSD_CONV - 1:SSD_CONV, cs] * pad_ref[CONV_HALO:CONV_HALO + tm, cs]
        for k in range(SSD_CONV - 1):
            off = CONV_HALO - (SSD_CONV - 1) + k
            acc = acc + cw_ref[k:k + 1, cs] * pad_ref[off:off + tm, cs]
        xbc_ref[:, cs] = (acc * _sigmoid(acc)).astype(xbc_ref.dtype)
        pad_ref[0:CONV_HALO, cs] = pad_ref[tm:tm + CONV_HALO, cs]


def _ssd_scan_kernel(xs_ref, b_ref, c_ref, z_ref, dt_ref, dtt_ref,
                     bias_ref, biast_ref, alog_ref, alogt_ref, dfull_ref, nw_ref,
                     o_ref, state_ref):
    q = xs_ref.shape[0]
    n_heads = dt_ref.shape[1]
    d_inner = xs_ref.shape[1]
    heads_per_group = n_heads // SSD_GROUPS
    gw = d_inner // SSD_GROUPS
    p = SSD_HEAD_DIM

    @pl.when(pl.program_id(1) == 0)
    def _():
        state_ref[...] = jnp.zeros(state_ref.shape, F32)

    dt = _softplus(dt_ref[...] + bias_ref[...])
    a = -jnp.exp(alog_ref[...]) * dt
    dtt = _softplus(dtt_ref[...] + biast_ref[...])
    at = -jnp.exp(alogt_ref[...]) * dtt

    row = lax.broadcasted_iota(jnp.int32, (q, q), 0)
    col = lax.broadcasted_iota(jnp.int32, (q, q), 1)
    lower = row >= col
    a_cum = _dot(lower.astype(F32), a, precision=HIGHEST)
    a_cumt = _dot(at, (row <= col).astype(F32), precision=HIGHEST)
    a_tot = a_cum[q - 1:q, :]

    hh = lax.broadcasted_iota(jnp.int32, (2 * n_heads, d_inner), 0)
    cc = lax.broadcasted_iota(jnp.int32, (2 * n_heads, d_inner), 1)
    expand = (cc // p == hh % n_heads).astype(BF16)
    per_head = jnp.concatenate(
        [dt, jnp.exp(a_cum), jnp.exp(a_tot - a_cum),
         jnp.broadcast_to(jnp.exp(a_tot), (V7X_SUBLANES, n_heads))], axis=0)
    per_channel = _dot(jnp.concatenate(_split_bf16(per_head), axis=1), expand)
    dt_full = per_channel[0:q]
    dec_out_full = per_channel[q:2 * q]
    dec_st_full = per_channel[2 * q:3 * q]
    chunk_dec_full = per_channel[3 * q:3 * q + 1]

    lane = lax.broadcasted_iota(jnp.int32, (1, gw), 1)
    for g in range(SSD_GROUPS):
        sl = slice(g * gw, (g + 1) * gw)
        bg = b_ref[:, g * SSD_STATE:(g + 1) * SSD_STATE]
        cg = c_ref[:, g * SSD_STATE:(g + 1) * SSD_STATE]
        xs_g = xs_ref[:, sl].astype(F32)
        xdt = xs_g * dt_full[:, sl]
        cb = _dot_nt(cg, bg)
        s_prev = state_ref[g]
        y = _dot(cg, s_prev.astype(BF16)) * dec_out_full[:, sl]
        for r in range(heads_per_group):
            hd = g * heads_per_group + r
            seg = a_cum[:, hd:hd + 1] - a_cumt[hd:hd + 1, :]
            m = jnp.where(lower, cb * jnp.exp(seg), 0.0).astype(BF16)
            xr = jnp.where((lane >= r * p) & (lane < (r + 1) * p), xdt, 0.0).astype(BF16)
            y = y + _dot(m, xr)
        xw = (xdt * dec_st_full[:, sl]).astype(BF16)
        state_ref[g] = s_prev * chunk_dec_full[:, sl] + _dot_tn(bg, xw)
        y = y + xs_g * dfull_ref[:, sl]
        zg = z_ref[:, sl].astype(F32)
        yz = y * (zg * _sigmoid(zg))
        o_ref[:, sl] = _rms(yz, nw_ref[:, sl]).astype(o_ref.dtype)


def _ssd_mixer(h, bsz, seqlen, g_norm, w_in, conv_w, conv_b, dt_bias, a_log, d_skip, norm_w, w_out):
    t, d = h.shape
    n_heads = dt_bias.shape[0]
    d_inner = n_heads * SSD_HEAD_DIM
    gn = SSD_GROUPS * SSD_STATE
    conv_dim = d_inner + 2 * gn
    wz = w_in[:, :d_inner].astype(BF16)
    wx = w_in[:, d_inner:d_inner + conv_dim].astype(BF16)
    wdt = _split_weight(w_in[:, d_inner + conv_dim:])
    tm = SSD_IN_TILE
    z, xbc, dt_raw = pl.pallas_call(
        functools.partial(_ssd_in_kernel, tiles_per_seq=seqlen // tm),
        grid=(t // tm,),
        in_specs=[pl.BlockSpec((tm, d), lambda i: (i, 0)),
                  _const_spec((1, d)),
                  _const_spec((d, d_inner)),
                  _const_spec((d, conv_dim)),
                  _const_spec((d, 2 * n_heads)),
                  _const_spec((SSD_CONV, conv_dim)),
                  _const_spec((1, conv_dim))],
        out_specs=[pl.BlockSpec((tm, d_inner), lambda i: (i, 0)),
                   pl.BlockSpec((tm, conv_dim), lambda i: (i, 0)),
                   pl.BlockSpec((tm, n_heads), lambda i: (i, 0))],
        out_shape=[jax.ShapeDtypeStruct((t, d_inner), BF16),
                   jax.ShapeDtypeStruct((t, conv_dim), BF16),
                   jax.ShapeDtypeStruct((t, n_heads), F32)],
        scratch_shapes=[pltpu.VMEM((tm + CONV_HALO, conv_dim), F32)],
        compiler_params=_cparams("arbitrary"),
        name="ssd_in",
    )(h, g_norm.reshape(1, d), wz, wx, wdt, conv_w, conv_b.reshape(1, conv_dim))

    q = SSD_CHUNK
    nc = seqlen // q
    xbc3 = xbc.reshape(bsz, seqlen, conv_dim)
    dt3 = dt_raw.reshape(bsz, seqlen, n_heads)
    dtt3 = jnp.swapaxes(dt3, 1, 2)
    nb_x = d_inner // gn
    yg = pl.pallas_call(
        _ssd_scan_kernel,
        grid=(bsz, nc),
        in_specs=[pl.BlockSpec((None, q, d_inner), lambda b, c: (b, c, 0)),
                  pl.BlockSpec((None, q, gn), lambda b, c: (b, c, nb_x)),
                  pl.BlockSpec((None, q, gn), lambda b, c: (b, c, nb_x + 1)),
                  pl.BlockSpec((None, q, d_inner), lambda b, c: (b, c, 0)),
                  pl.BlockSpec((None, q, n_heads), lambda b, c: (b, c, 0)),
                  pl.BlockSpec((None, n_heads, q), lambda b, c: (b, 0, c)),
                  _const_spec((1, n_heads)), _const_spec((n_heads, 1)),
                  _const_spec((1, n_heads)), _const_spec((n_heads, 1)),
                  _const_spec((1, d_inner)), _const_spec((1, d_inner))],
        out_specs=pl.BlockSpec((None, q, d_inner), lambda b, c: (b, c, 0)),
        out_shape=jax.ShapeDtypeStruct((bsz, seqlen, d_inner), BF16),
        scratch_shapes=[pltpu.VMEM((SSD_GROUPS, SSD_STATE, d_inner // SSD_GROUPS), F32)],
        compiler_params=_cparams("arbitrary", "arbitrary"),
        name="ssd_scan",
    )(xbc3, xbc3, xbc3, z.reshape(bsz, seqlen, d_inner), dt3, dtt3,
      dt_bias.reshape(1, n_heads), dt_bias.reshape(n_heads, 1),
      a_log.reshape(1, n_heads), a_log.reshape(n_heads, 1),
      jnp.repeat(d_skip, SSD_HEAD_DIM).reshape(1, d_inner), norm_w.reshape(1, d_inner))
    return _proj_residual(yg.reshape(t, d_inner), w_out.astype(BF16), h)


def _sb_in_kernel(h_ref, g_ref, w_ref, o_ref, *, q_cols, q_scale):
    u = _rms(h_ref[...], g_ref[...]).astype(BF16)
    o_ref[:, :q_cols] = (_dot(u, w_ref[:, :q_cols]) * q_scale).astype(o_ref.dtype)
    o_ref[:, q_cols:] = _dot(u, w_ref[:, q_cols:]).astype(o_ref.dtype)


def _sb_attn_kernel(q_ref, k_ref, v_ref, o_ref, acc_ref, *, tile):
    seqlen = q_ref.shape[0]
    nq = seqlen // tile
    pw = 2 * SB_HEAD_DIM
    n_pairs = q_ref.shape[1] // pw
    lane = lax.broadcasted_iota(jnp.int32, (1, pw), 1)
    first = lane < SB_HEAD_DIM
    row = lax.broadcasted_iota(jnp.int32, (tile, tile), 0)
    col = lax.broadcasted_iota(jnp.int32, (tile, tile), 1)
    neg_from = jnp.where(row >= col, -1.0, 0.0).astype(BF16)
    past = col < row
    past2 = jnp.concatenate([past, past], axis=0)

    def k_tile(qs, kj, csums, diagonal):
        rows = pl.ds(pl.multiple_of(kj * tile, tile), tile)
        out = []
        for p in range(n_pairs):
            k = k_ref[rows, p * pw:(p + 1) * pw]
            v = v_ref[rows, p * pw:(p + 1) * pw]
            t = _dot_nt(qs[p], k)
            neg_abs = pltpu.bitcast(pltpu.bitcast(t, jnp.uint32) | jnp.uint32(0x80000000), F32)
            sp = jnp.maximum(t, 0.0) + jnp.log(1.0 + jnp.exp2(neg_abs)) * _LOG2_E
            if diagonal:
                sp = jnp.where(past2, sp, 0.0)
            w = jnp.exp2(t + _dot(sp.astype(BF16), neg_from) + csums[p])
            if diagonal:
                w = jnp.where(past2, w, 0.0)
            acc_ref[p] += _dot(w.astype(BF16), v)
            out.append(csums[p] - jnp.sum(sp, axis=1, keepdims=True))
        return tuple(out)

    def q_body(qi, _):
        qrows = pl.ds(pl.multiple_of(qi * tile, tile), tile)
        qs = []
        for p in range(n_pairs):
            q = q_ref[qrows, p * pw:(p + 1) * pw]
            zero = jnp.zeros_like(q)
            qs.append(jnp.concatenate([jnp.where(first, q, zero), jnp.where(first, zero, q)], axis=0))
            acc_ref[p] = jnp.zeros(acc_ref.shape[1:], F32)
        init = tuple(jnp.zeros((2 * tile, 1), F32) for _ in range(n_pairs))
        csums = k_tile(qs, qi, init, True)
        lax.fori_loop(0, qi, lambda s, c: k_tile(qs, qi - 1 - s, c, False), csums)
        for p in range(n_pairs):
            o = jnp.where(first, acc_ref[p, 0:tile, :], acc_ref[p, tile:2 * tile, :])
            o_ref[qrows, p * pw:(p + 1) * pw] = o.astype(o_ref.dtype)
        return 0

    lax.fori_loop(0, nq, q_body, 0)


def _sb_mixer(h, bsz, seqlen, g_norm, w_in, w_out):
    t, d = h.shape
    hd = w_in.shape[1] // 3
    tm = ROW_TILE
    qkv = pl.pallas_call(
        functools.partial(_sb_in_kernel, q_cols=hd, q_scale=SB_HEAD_DIM ** -0.5 * _LOG2_E),
        grid=(t // tm,),
        in_specs=[pl.BlockSpec((tm, d), lambda i: (i, 0)),
                  _const_spec((1, d)),
                  _const_spec((d, 3 * hd))],
        out_specs=pl.BlockSpec((tm, 3 * hd), lambda i: (i, 0)),
        out_shape=jax.ShapeDtypeStruct((t, 3 * hd), BF16),
        compiler_params=_cparams("arbitrary"),
        name="sb_in",
    )(h, g_norm.reshape(1, d), w_in.astype(BF16))
    qkv3 = qkv.reshape(bsz, seqlen, 3 * hd)
    bw = SB_PAIRS_PER_STEP * 2 * SB_HEAD_DIM
    nb = hd // bw
    tile = min(ATTN_TILE, seqlen)
    o = pl.pallas_call(
        functools.partial(_sb_attn_kernel, tile=tile),
        grid=(bsz, nb),
        in_specs=[pl.BlockSpec((None, seqlen, bw), lambda b, j: (b, 0, j)),
                  pl.BlockSpec((None, seqlen, bw), lambda b, j: (b, 0, nb + j)),
                  pl.BlockSpec((None, seqlen, bw), lambda b, j: (b, 0, 2 * nb + j))],
        out_specs=pl.BlockSpec((None, seqlen, bw), lambda b, j: (b, 0, j)),
        out_shape=jax.ShapeDtypeStruct((bsz, seqlen, hd), BF16),
        scratch_shapes=[pltpu.VMEM((SB_PAIRS_PER_STEP, 2 * tile, 2 * SB_HEAD_DIM), F32)],
        compiler_params=_cparams("arbitrary", "arbitrary"),
        name="sb_attn",
    )(qkv3, qkv3, qkv3)
    return _proj_residual(o.reshape(t, hd), w_out.astype(BF16), h)


def _lru_kernel(h_ref, g_ref, win_ref, cw_ref, cb_ref, wband_ref, br_ref, bi_ref, lam_ref,
                y_ref, pad_ref, xc_ref, a_ref, b_ref, hs_ref, carry_ref, *, band_starts):
    bsz, tt, d = h_ref.shape
    width = y_ref.shape[2]

    @pl.when(pl.program_id(0) == 0)
    def _():
        carry_ref[...] = jnp.zeros(carry_ref.shape, F32)
        for b in range(bsz):
            pad_ref[b, 0:CONV_HALO, :] = jnp.zeros((CONV_HALO, width), F32)

    u = _rms(h_ref[...].reshape(bsz * tt, d), g_ref[...]).astype(BF16)
    proj = _dot(u, win_ref[...])
    gate_branch = proj[:, :width]
    for b in range(bsz):
        pad_ref[b, CONV_HALO:CONV_HALO + tt, :] = proj[b * tt:(b + 1) * tt, width:]
        acc = cb_ref[...] + cw_ref[LRU_CONV - 1:LRU_CONV, :] * pad_ref[b, CONV_HALO:CONV_HALO + tt, :]
        for k in range(LRU_CONV - 1):
            off = CONV_HALO - (LRU_CONV - 1) + k
            acc = acc + cw_ref[k:k + 1, :] * pad_ref[b, off:off + tt, :]
        xc_ref[b * tt:(b + 1) * tt, :] = acc
        pad_ref[b, 0:CONV_HALO, :] = pad_ref[b, tt:tt + CONV_HALO, :]

    xcb = xc_ref[...].astype(BF16)
    n_lt = width // V7X_LANES
    pitch = a_ref.shape[1] // bsz
    neg_c_softplus = -LRU_C * _softplus(-lam_ref[...])
    for c in range(n_lt):
        cs = slice(c * V7X_LANES, (c + 1) * V7X_LANES)
        pre = _dot(xcb[:, band_starts[c]:band_starts[c] + LRU_BAND], wband_ref[c])
        r = _sigmoid(pre[:, :V7X_LANES] + br_ref[:, cs])
        ig = _sigmoid(pre[:, V7X_LANES:] + bi_ref[:, cs])
        a = jnp.exp(neg_c_softplus[:, cs] * r)
        bterm = jnp.sqrt(1.0 - a * a) * (ig * xc_ref[:, cs])
        for b in range(bsz):
            a_ref[c, b * pitch:b * pitch + tt, :] = a[b * tt:(b + 1) * tt, :]
            b_ref[c, b * pitch:b * pitch + tt, :] = bterm[b * tt:(b + 1) * tt, :]

    def step(j, hprev):
        hnew = []
        for c in range(n_lt):
            rows_j = pl.ds(j, bsz, stride=pitch)
            hc = a_ref[c, rows_j, :] * hprev[c] + b_ref[c, rows_j, :]
            hs_ref[c, rows_j, :] = hc
            hnew.append(hc)
        return tuple(hnew)

    hlast = lax.fori_loop(0, tt, step, tuple(carry_ref[c] for c in range(n_lt)), unroll=LRU_SCAN_UNROLL)
    for c in range(n_lt):
        carry_ref[c] = hlast[c]
    c0 = math.sqrt(2.0 / math.pi)
    gelu = 0.5 * gate_branch * (1.0 + jnp.tanh(c0 * (gate_branch + 0.044715 * gate_branch ** 3)))
    for b in range(bsz):
        hs_b = jnp.concatenate([hs_ref[c, b * pitch:b * pitch + tt, :] for c in range(n_lt)], axis=1)
        y_ref[b] = (hs_b * gelu[b * tt:(b + 1) * tt, :]).astype(y_ref.dtype)


def _lru_band_starts(width):
    bd = width // LRU_BLOCKS
    starts = []
    for j in range(width // V7X_LANES):
        b_lo = (j * V7X_LANES) // bd
        b_hi = ((j + 1) * V7X_LANES - 1) // bd
        lo = (b_lo * bd) // V7X_LANES * V7X_LANES
        assert (b_hi + 1) * bd - lo <= LRU_BAND
        starts.append(min(lo, width - LRU_BAND))
    return tuple(starts)


def _lru_band_weights(w_rg, w_ig, starts):
    def dense(w):
        nb, bd, _ = w.shape
        eye = jnp.eye(nb, dtype=w.dtype)
        return (eye[:, None, :, None] * w[:, :, None, :]).reshape(nb * bd, nb * bd)

    dense_r, dense_i = dense(w_rg), dense(w_ig)
    tiles = []
    for j, s in enumerate(starts):
        cs = slice(j * V7X_LANES, (j + 1) * V7X_LANES)
        tiles.append(jnp.concatenate([dense_r[s:s + LRU_BAND, cs], dense_i[s:s + LRU_BAND, cs]], axis=1))
    return jnp.stack(tiles).astype(BF16)


def _lru_mixer(h, bsz, seqlen, g_norm, w_in, conv_w, conv_b, w_rg, b_rg, w_ig, b_ig, lam, w_out):
    t, d = h.shape
    width = lam.shape[0]
    tt = min(LRU_TIME_TILE, seqlen)
    rows = bsz * tt
    n_lt = width // V7X_LANES
    scan_rows = bsz * (tt + LRU_PITCH_PAD)
    starts = _lru_band_starts(width)
    y = pl.pallas_call(
        functools.partial(_lru_kernel, band_starts=starts),
        grid=(seqlen // tt,),
        in_specs=[pl.BlockSpec((bsz, tt, d), lambda i: (0, i, 0)),
                  _const_spec((1, d)),
                  _const_spec((d, 2 * width)),
                  _const_spec((LRU_CONV, width)), _const_spec((1, width)),
                  _const_spec((n_lt, LRU_BAND, 2 * V7X_LANES)),
                  _const_spec((1, width)), _const_spec((1, width)),
                  _const_spec((1, width))],
        out_specs=pl.BlockSpec((bsz, tt, width), lambda i: (0, i, 0)),
        out_shape=jax.ShapeDtypeStruct((bsz, seqlen, width), BF16),
        scratch_shapes=[pltpu.VMEM((bsz, tt + CONV_HALO, width), F32),
                        pltpu.VMEM((rows, width), F32),
                        pltpu.VMEM((n_lt, scan_rows, V7X_LANES), F32),
                        pltpu.VMEM((n_lt, scan_rows, V7X_LANES), F32),
                        pltpu.VMEM((n_lt, scan_rows, V7X_LANES), F32),
                        pltpu.VMEM((n_lt, bsz, V7X_LANES), F32)],
        compiler_params=_cparams("arbitrary"),
        name="lru",
    )(h.reshape(bsz, seqlen, d), g_norm.reshape(1, d), w_in.astype(BF16),
      conv_w, conv_b.reshape(1, width), _lru_band_weights(w_rg, w_ig, starts),
      b_rg.reshape(1, width), b_ig.reshape(1, width), lam.reshape(1, width))
    return _proj_residual(y.reshape(t, width), w_out.astype(BF16), h)


def _diff_in_kernel(h_ref, g_ref, w_ref, cos_ref, sin_ref, qn_ref, kn_ref, q_ref, k_ref, v_ref,
                    *, q_scale):
    d_q = q_ref.shape[1]
    hdim = DIFF_HEAD_DIM
    n_grp = d_q // hdim
    u = _rms(h_ref[...], g_ref[...]).astype(BF16)
    v_ref[...] = _dot(u, w_ref[:, 2 * d_q:]).astype(v_ref.dtype)

    ci = lax.broadcasted_iota(jnp.int32, (d_q, n_grp), 0)
    gi = lax.broadcasted_iota(jnp.int32, (d_q, n_grp), 1)
    gsum = (ci // hdim == gi).astype(BF16)
    gj = lax.broadcasted_iota(jnp.int32, (2 * n_grp, d_q), 0)
    cj = lax.broadcasted_iota(jnp.int32, (2 * n_grp, d_q), 1)
    gexp = (cj // hdim == gj % n_grp).astype(BF16)
    lane = lax.broadcasted_iota(jnp.int32, (1, d_q), 1)
    first_half = (lane % hdim) < (hdim // 2)
    reps = d_q // cos_ref.shape[1]
    cos_t = jnp.tile(cos_ref[...], (1, reps))
    sin_t = jnp.tile(sin_ref[...], (1, reps))

    def norm_rope(x, nw):
        sq = x * x
        sq_hi = sq.astype(BF16)
        sq_lo = (sq - sq_hi.astype(F32)).astype(BF16)
        ms = (_dot(sq_hi, gsum) + _dot(sq_lo, gsum)) * (1.0 / hdim)
        inv = lax.rsqrt(ms + RMS_EPS)
        inv_hi = inv.astype(BF16)
        inv_lo = (inv - inv_hi.astype(F32)).astype(BF16)
        inv_full = _dot(jnp.concatenate([inv_hi, inv_lo], axis=1), gexp)
        xn = x * inv_full * nw
        rot = jnp.where(first_half, pltpu.roll(xn, d_q - hdim // 2, 1), pltpu.roll(xn, hdim // 2, 1))
        return xn * cos_t + rot * sin_t

    q_ref[...] = (norm_rope(_dot(u, w_ref[:, :d_q]), qn_ref[...]) * q_scale).astype(q_ref.dtype)
    k_ref[...] = norm_rope(_dot(u, w_ref[:, d_q:2 * d_q]), kn_ref[...]).astype(k_ref.dtype)


def _diff_attn_kernel(q_ref, k_ref, v_ref, lq1_ref, lk1_ref, lq2_ref, lk2_ref, sn_ref, o_ref,
                      acc_ref, *, tile, lambda_init):
    seqlen = q_ref.shape[0]
    nq = seqlen // tile
    hdim = DIFF_HEAD_DIM
    hw = 2 * hdim
    n_heads = q_ref.shape[1] // hw
    lane = lax.broadcasted_iota(jnp.int32, (1, hw), 1)
    first = lane < hdim
    row = lax.broadcasted_iota(jnp.int32, (tile, tile), 0)
    col = lax.broadcasted_iota(jnp.int32, (tile, tile), 1)
    causal = col <= row
    causal2 = jnp.concatenate([causal, causal], axis=0)
    ones = jnp.ones((tile, hw), BF16)
    lam = (jnp.exp(jnp.sum(lq1_ref[...] * lk1_ref[...], axis=1, keepdims=True))
           - jnp.exp(jnp.sum(lq2_ref[...] * lk2_ref[...], axis=1, keepdims=True)) + lambda_init)

    def k_tile(qs, kj, maxes, diagonal):
        rows = pl.ds(pl.multiple_of(kj * tile, tile), tile)
        out = []
        for hd in range(n_heads):
            k = k_ref[rows, hd * hw:(hd + 1) * hw]
            v1 = jnp.concatenate([v_ref[rows, hd * hw:(hd + 1) * hw], ones], axis=1)
            s = _dot_nt(qs[hd], k)
            if diagonal:
                s = jnp.where(causal2, s, -jnp.inf)
            mx_new = jnp.maximum(maxes[hd], jnp.max(s, axis=1, keepdims=True))
            alpha = jnp.exp2(maxes[hd] - mx_new)
            pr = jnp.exp2(s - mx_new)
            acc_ref[hd] = acc_ref[hd] * alpha + _dot(pr.astype(BF16), v1)
            out.append(mx_new)
        return tuple(out)

    def q_body(qi, _):
        qrows = pl.ds(pl.multiple_of(qi * tile, tile), tile)
        qs = []
        for hd in range(n_heads):
            q = q_ref[qrows, hd * hw:(hd + 1) * hw]
            zero = jnp.zeros_like(q)
            qs.append(jnp.concatenate([jnp.where(first, q, zero), jnp.where(first, zero, q)], axis=0))
            acc_ref[hd] = jnp.zeros(acc_ref.shape[1:], F32)
        init = tuple(jnp.full((2 * tile, 1), -jnp.inf, F32) for _ in range(n_heads))
        maxes = k_tile(qs, qi, init, True)
        lax.fori_loop(0, qi, lambda s, c: k_tile(qs, qi - 1 - s, c, False), maxes)
        for hd in range(n_heads):
            p = acc_ref[hd, :, 0:hw] / acc_ref[hd, :, hw:2 * hw]
            o = p[0:tile] - lam * p[tile:2 * tile]
            o = _rms(o, sn_ref[...]) * (1.0 - lambda_init)
            o_ref[qrows, hd * hw:(hd + 1) * hw] = o.astype(o_ref.dtype)
        return 0

    lax.fori_loop(0, nq, q_body, 0)


def _diff_mixer(h, bsz, seqlen, cos_t, sin_t, g_norm, w_in, q_norm, k_norm, lq1, lk1, lq2, lk2,
                sub_norm, w_out, lambda_init):
    t, d = h.shape
    d_q = w_in.shape[1] // 3
    hdim = DIFF_HEAD_DIM
    tm = ROW_TILE
    tw = cos_t.shape[1]
    q, k, v = pl.pallas_call(
        functools.partial(_diff_in_kernel, q_scale=hdim ** -0.5 * _LOG2_E),
        grid=(t // tm,),
        in_specs=[pl.BlockSpec((tm, d), lambda i: (i, 0)),
                  _const_spec((1, d)),
                  _const_spec((d, 3 * d_q)),
                  pl.BlockSpec((tm, tw), lambda i: (i, 0)),
                  pl.BlockSpec((tm, tw), lambda i: (i, 0)),
                  _const_spec((1, d_q)), _const_spec((1, d_q))],
        out_specs=[pl.BlockSpec((tm, d_q), lambda i: (i, 0))] * 3,
        out_shape=[jax.ShapeDtypeStruct((t, d_q), BF16)] * 3,
        compiler_params=_cparams("arbitrary"),
        name="diff_in",
    )(h, g_norm.reshape(1, d), w_in.astype(BF16), cos_t, sin_t,
      jnp.tile(q_norm, d_q // hdim).reshape(1, d_q), jnp.tile(k_norm, d_q // hdim).reshape(1, d_q))
    hw = 2 * hdim
    bw = DIFF_HEADS_PER_STEP * hw
    tile = min(ATTN_TILE, seqlen)
    spec = pl.BlockSpec((None, seqlen, bw), lambda b, j: (b, 0, j))
    vec = _const_spec((1, hdim))
    o = pl.pallas_call(
        functools.partial(_diff_attn_kernel, tile=tile, lambda_init=lambda_init),
        grid=(bsz, d_q // bw),
        in_specs=[spec, spec, spec, vec, vec, vec, vec, _const_spec((1, hw))],
        out_specs=spec,
        out_shape=jax.ShapeDtypeStruct((bsz, seqlen, d_q), BF16),
        scratch_shapes=[pltpu.VMEM((DIFF_HEADS_PER_STEP, 2 * tile, 2 * hw), F32)],
        compiler_params=_cparams("arbitrary", "arbitrary"),
        name="diff_attn",
    )(q.reshape(bsz, seqlen, d_q), k.reshape(bsz, seqlen, d_q), v.reshape(bsz, seqlen, d_q),
      lq1.reshape(1, hdim), lk1.reshape(1, hdim), lq2.reshape(1, hdim), lk2.reshape(1, hdim),
      sub_norm.reshape(1, hw))
    return _proj_residual(o.reshape(t, d_q), w_out.astype(BF16), h)


_R_E1, _R_E2, _R_W1, _R_W2, _R_RANK1, _R_RANK2 = range(6)
_C_LOC1, _C_LOC2, _C_W1, _C_W2 = range(4)


def _router_kernel(h_ref, g_ref, w_ref, b_ref, rec_ref, base_ref, cnt_ref, run_ref):
    tm = h_ref.shape[0]
    ne, ng, epg = MOE_EXPERTS, MOE_GROUPS, MOE_EXPERTS_PER_GROUP

    @pl.when(pl.program_id(0) == 0)
    def _():
        run_ref[...] = jnp.zeros(run_ref.shape, F32)

    base_ref[...] = run_ref[...]

    u = _rms(h_ref[...], g_ref[...])
    logits = _dot_split(u, u.astype(BF16), w_ref) + b_ref[...]
    lane = lax.broadcasted_iota(jnp.int32, logits.shape, 1)
    lane_f = lane.astype(F32)
    lane_group = (lane // epg).astype(F32)
    neg = -jnp.inf
    big = float(V7X_LANES)

    gl = jnp.where((lane >= ne) & (lane < ne + ng), logits, neg)
    gmax = jnp.max(gl, axis=1, keepdims=True)
    gsel = jnp.min(jnp.where(gl == gmax, lane_f - ne, big), axis=1, keepdims=True)
    g_gate = 1.0 / jnp.sum(jnp.exp(gl - gmax), axis=1, keepdims=True)

    el = jnp.where((lane < ne) & (lane_group == gsel), logits, neg)
    m1 = jnp.max(el, axis=1, keepdims=True)
    i1 = jnp.min(jnp.where(el == m1, lane_f, big), axis=1, keepdims=True)
    el2 = jnp.where(lane_f == i1, neg, el)
    m2 = jnp.max(el2, axis=1, keepdims=True)
    i2 = jnp.min(jnp.where(el2 == m2, lane_f, big), axis=1, keepdims=True)
    e21 = jnp.exp(m2 - m1)
    w1 = g_gate / (1.0 + e21)
    w2 = g_gate * e21 / (1.0 + e21)

    oh1 = lane_f == i1
    oh2 = lane_f == i2
    oh = jnp.where(oh1 | oh2, 1.0, 0.0)
    row = lax.broadcasted_iota(jnp.int32, (tm, tm), 0)
    col = lax.broadcasted_iota(jnp.int32, (tm, tm), 1)
    before = _dot((col < row).astype(BF16), oh.astype(BF16))
    pos = run_ref[0:1, :] + before
    rank1 = jnp.sum(jnp.where(oh1, pos, 0.0), axis=1, keepdims=True)
    rank2 = jnp.sum(jnp.where(oh2, pos, 0.0), axis=1, keepdims=True)
    run_ref[...] = run_ref[...] + jnp.sum(oh, axis=0, keepdims=True)
    cnt_ref[...] = run_ref[...]

    rec = jnp.zeros(logits.shape, F32)
    for idx, val in ((_R_E1, i1), (_R_E2, i2), (_R_W1, w1), (_R_W2, w2),
                     (_R_RANK1, rank1), (_R_RANK2, rank2)):
        rec = jnp.where(lane == idx, val, rec)
    rec_ref[...] = rec


def _pack_bf16_pair(x):
    n = x.shape[1] // 2
    xr = x.astype(BF16).astype(F32)
    return pltpu.bitcast(xr[:, :n], jnp.uint32) | (pltpu.bitcast(xr[:, n:], jnp.uint32) >> 16)


def _unpack_bf16_pair(w):
    hi = pltpu.bitcast(w & jnp.uint32(0xFFFF0000), F32)
    lo = pltpu.bitcast(w << 16, F32)
    return jnp.concatenate([hi, lo], axis=1)


def _chunk_rows(chunk):
    return pl.ds(pl.multiple_of(chunk * MOE_CHUNK, MOE_CHUNK), MOE_CHUNK)


def _start_chunk_copies(chunk_of_ref, tile, n_chunks, copy_of):
    def issue(j, _):
        copy_of(_chunk_rows(j), _chunk_rows(chunk_of_ref[tile * n_chunks + j])).start()
        return 0

    lax.fori_loop(0, n_chunks, issue, 0, unroll=DMA_LOOP_UNROLL)


def _wait_chunk_copies(n_chunks, copy_of):
    def drain(j, _):
        copy_of(_chunk_rows(0), _chunk_rows(0)).wait()
        return 0

    lax.fori_loop(0, n_chunks, drain, 0, unroll=DMA_LOOP_UNROLL)


def _dispatch_kernel(chunk_of_ref, h_ref, g_ref, loct_ref, rows_in_ref, rows_ref, sorted_ref, sems):
    del rows_in_ref
    i, nt = pl.program_id(0), pl.num_programs(0)
    slot = i % 2
    tm = h_ref.shape[0]
    n_chunks = sorted_ref.shape[1] // MOE_CHUNK

    def copy_of(slot_):
        return lambda lr, gr: pltpu.make_async_copy(sorted_ref.at[slot_, lr], rows_ref.at[gr],
                                                    sems.at[slot_])

    @pl.when(i >= 2)
    def _():
        _wait_chunk_copies(n_chunks, copy_of(slot))

    u = _rms(h_ref[...], g_ref[...]).astype(BF16)
    local_row = lax.broadcasted_iota(jnp.int32, (sorted_ref.shape[1], tm), 0)
    loc = loct_ref[...]
    pick = jnp.where(local_row == loc[0:1, :], 1.0, jnp.where(local_row == loc[1:2, :], 1.0, 0.0))
    sorted_ref[slot] = _pack_bf16_pair(_dot(pick.astype(BF16), u))
    _start_chunk_copies(chunk_of_ref, i, n_chunks, copy_of(slot))

    @pl.when(i == nt - 1)
    def _():
        _wait_chunk_copies(n_chunks, copy_of(slot))

    @pl.when(jnp.logical_and(i == nt - 1, i >= 1))
    def _():
        _wait_chunk_copies(n_chunks, copy_of(1 - slot))


def _expert_kernel(be_ref, nu_ref, nxt_ref, x_ref, wg_hbm, wu_hbm, wd_hbm, y_ref,
                   wgf_ref, wuf_ref, wdf_ref, wgb_ref, wub_ref, wdb_ref, sems, *, layer):
    i = pl.program_id(0)

    def weight_copies(e):
        return (pltpu.make_async_copy(wg_hbm.at[layer, e], wgf_ref, sems.at[0]),
                pltpu.make_async_copy(wu_hbm.at[layer, e], wuf_ref, sems.at[1]),
                pltpu.make_async_copy(wd_hbm.at[layer, e], wdf_ref, sems.at[2]))

    @pl.when(i < nu_ref[0])
    def _():
        e = be_ref[i]

        @pl.when(i == 0)
        def _():
            for c in weight_copies(e):
                c.start()

        @pl.when(jnp.logical_or(i == 0, e != be_ref[jnp.maximum(i - 1, 0)]))
        def _():
            for c in weight_copies(e):
                c.wait()
            wgb_ref[...] = wgf_ref[...].astype(BF16)
            wub_ref[...] = wuf_ref[...].astype(BF16)
            wdb_ref[...] = wdf_ref[...].astype(BF16)
            nxt = nxt_ref[i]

            @pl.when(nxt >= 0)
            def _():
                for c in weight_copies(nxt):
                    c.start()

        x = _unpack_bf16_pair(x_ref[...]).astype(BF16)
        gate = _dot(x, wgb_ref[...])
        up = _dot(x, wub_ref[...])
        hid = (gate * _sigmoid(gate) * up).astype(BF16)
        y_ref[...] = _pack_bf16_pair(_dot(hid, wdb_ref[...]))

    @pl.when(i >= nu_ref[0])
    def _():
        y_ref[...] = jnp.zeros(y_ref.shape, y_ref.dtype)


def _combine_kernel(chunk_of_ref, h_ref, crec_ref, rows_ref, o_ref, sorted_ref, sems):
    i, nt = pl.program_id(0), pl.num_programs(0)
    slot = i % 2
    tm = h_ref.shape[0]
    n_chunks = sorted_ref.shape[1] // MOE_CHUNK

    def copy_of(slot_):
        return lambda lr, gr: pltpu.make_async_copy(rows_ref.at[gr], sorted_ref.at[slot_, lr],
                                                    sems.at[slot_])

    @pl.when(i == 0)
    def _():
        _start_chunk_copies(chunk_of_ref, i, n_chunks, copy_of(slot))

    @pl.when(i + 1 < nt)
    def _():
        _start_chunk_copies(chunk_of_ref, i + 1, n_chunks, copy_of(1 - slot))

    _wait_chunk_copies(n_chunks, copy_of(slot))
    crec = crec_ref[...]
    loc0 = crec[:, _C_LOC1:_C_LOC1 + 1].astype(jnp.int32)
    loc1 = crec[:, _C_LOC2:_C_LOC2 + 1].astype(jnp.int32)
    w1 = crec[:, _C_W1:_C_W1 + 1]
    w2 = crec[:, _C_W2:_C_W2 + 1]
    local_row = lax.broadcasted_iota(jnp.int32, (tm, sorted_ref.shape[1]), 1)
    weights = jnp.where(local_row == loc0, w1, jnp.where(local_row == loc1, w2, 0.0)).astype(BF16)
    y = _unpack_bf16_pair(sorted_ref[slot]).astype(BF16)
    o_ref[...] = h_ref[...] + _dot(weights, y)


def _hier_moe(h, g_norm, w_group, b_group, w_expert, b_expert, layer, w_gate, w_up, w_down):
    t, d = h.shape
    ne, ng = MOE_EXPERTS, MOE_GROUPS
    ff = w_gate.shape[3]
    dp = d // 2
    tm = min(MOE_TOKEN_TILE, t)
    g2 = g_norm.reshape(1, d)
    lane_pad = V7X_LANES - ne - ng
    w_router = _split_weight(
        jnp.concatenate([w_expert, w_group, jnp.zeros((d, lane_pad), F32)], axis=1))
    b_router = jnp.concatenate([b_expert, b_group, jnp.zeros((lane_pad,), F32)]).reshape(1, V7X_LANES)
    nt = t // tm
    rec, base, cnt = pl.pallas_call(
        _router_kernel,
        grid=(nt,),
        in_specs=[pl.BlockSpec((tm, d), lambda i: (i, 0)),
                  _const_spec((1, d)),
                  _const_spec((d, 2 * V7X_LANES)),
                  _const_spec((1, V7X_LANES))],
        out_specs=[pl.BlockSpec((tm, V7X_LANES), lambda i: (i, 0)),
                   pl.BlockSpec((V7X_SUBLANES, V7X_LANES), lambda i: (i, 0)),
                   _const_spec((V7X_SUBLANES, V7X_LANES))],
        out_shape=[jax.ShapeDtypeStruct((t, V7X_LANES), F32),
                   jax.ShapeDtypeStruct((nt * V7X_SUBLANES, V7X_LANES), F32),
                   jax.ShapeDtypeStruct((V7X_SUBLANES, V7X_LANES), F32)],
        scratch_shapes=[pltpu.VMEM((V7X_SUBLANES, V7X_LANES), F32)],
        compiler_params=_cparams("arbitrary"),
        name="moe_router",
    )(h, g2, w_router, b_router)

    br, ck = MOE_ROW_BLOCK, MOE_CHUNK
    n_blocks = -(-(2 * t + nt * ne * (ck - 1) + ne * (br - 1)) // br)
    n_rows = n_blocks * br
    local_rows = 2 * tm + ne * ck
    before = base.reshape(nt, V7X_SUBLANES, V7X_LANES)[:, 0, :ne].astype(jnp.int32)
    total = cnt[0, :ne].astype(jnp.int32)
    tile_cnt = jnp.concatenate([before[1:], total[None, :]], axis=0) - before
    seg = (tile_cnt + ck - 1) // ck * ck
    region = jnp.sum(seg, axis=0)
    padded = (region + br - 1) // br * br
    pad_end = jnp.cumsum(padded)
    pad_start = pad_end - padded
    seg_global = pad_start[None, :] + jnp.cumsum(seg, axis=0) - seg
    seg_local = jnp.cumsum(seg, axis=1) - seg
    eid = rec[:, _R_E1:_R_E2 + 1].astype(jnp.int32)
    rank = rec[:, _R_RANK1:_R_RANK2 + 1].astype(jnp.int32)
    experts = jnp.arange(ne, dtype=jnp.int32)
    shift = jnp.repeat(seg_local - before, tm, axis=0)
    loc = rank + jnp.sum(jnp.where(eid[:, :, None] == experts, shift[:, None, :], 0), axis=-1)
    loct = jnp.concatenate([jnp.swapaxes(loc.reshape(nt, tm, 2), 1, 2),
                            jnp.full((nt, V7X_SUBLANES - 2, tm), -1, jnp.int32)], axis=1)
    crec = jnp.concatenate([loc.astype(F32), rec[:, _R_W1:_R_W2 + 1],
                            jnp.zeros((t, V7X_SUBLANES - 4), F32)], axis=1)
    n_local = local_rows // ck
    j = jnp.arange(n_local, dtype=jnp.int32)
    seg_end = (seg_local + seg) // ck
    owner = jnp.sum((seg_end[:, None, :] <= j[None, :, None]).astype(jnp.int32), axis=-1)
    offset = (seg_global - seg_local) // ck
    chunk_of = j[None, :] + jnp.sum(jnp.where(owner[:, :, None] == experts, offset[:, None, :], 0), axis=-1)
    in_use = owner < ne
    parity = (jnp.arange(nt, dtype=jnp.int32) % 2)[:, None]
    scatter_chunk = jnp.where(in_use, chunk_of, n_rows // ck + parity * n_local + j[None, :]).reshape(-1)
    gather_chunk = jnp.where(in_use, chunk_of, 0).reshape(-1)
    block_row0 = jnp.arange(n_blocks, dtype=jnp.int32) * br
    block_expert = jnp.minimum(
        jnp.sum((pad_end[None, :] <= block_row0[:, None]).astype(jnp.int32), axis=1), ne - 1)
    n_used = (pad_end[-1:] // br).astype(jnp.int32)
    blocks = jnp.arange(n_blocks, dtype=jnp.int32)
    later_run = ((block_expert[None, :] != block_expert[:, None]) & (blocks[None, :] > blocks[:, None])
                 & (blocks[None, :] < n_used))
    next_expert = jnp.where(jnp.any(later_run, axis=1),
                            block_expert[jnp.argmax(later_run, axis=1)], -1).astype(jnp.int32)

    x_rows = pl.pallas_call(
        _dispatch_kernel,
        grid_spec=pltpu.PrefetchScalarGridSpec(
            num_scalar_prefetch=1,
            grid=(nt,),
            in_specs=[pl.BlockSpec((tm, d), lambda i, *_: (i, 0)),
                      _const_spec((1, d)),
                      pl.BlockSpec((None, V7X_SUBLANES, tm), lambda i, *_: (i, 0, 0)),
                      pl.BlockSpec(memory_space=pl.ANY)],
            out_specs=pl.BlockSpec(memory_space=pl.ANY),
            scratch_shapes=[pltpu.VMEM((2, local_rows, dp), jnp.uint32), pltpu.SemaphoreType.DMA((2,))]),
        out_shape=jax.ShapeDtypeStruct((n_rows + 2 * local_rows, dp), jnp.uint32),
        input_output_aliases={4: 0},
        compiler_params=_cparams("arbitrary"),
        name="moe_dispatch",
    )(scatter_chunk, h, g2, loct, jnp.zeros((n_rows + 2 * local_rows, dp), jnp.uint32))

    def blk(i, be, nu, nxt):
        return jnp.maximum(jnp.minimum(i, nu[0] - 1), 0)

    hbm = pl.BlockSpec(memory_space=pl.ANY)
    y_rows = pl.pallas_call(
        functools.partial(_expert_kernel, layer=layer),
        grid_spec=pltpu.PrefetchScalarGridSpec(
            num_scalar_prefetch=3,
            grid=(n_blocks,),
            in_specs=[pl.BlockSpec((br, dp), lambda i, be, nu, nxt: (blk(i, be, nu, nxt), 0)),
                      hbm, hbm, hbm],
            out_specs=pl.BlockSpec((br, dp), lambda i, be, nu, nxt: (i, 0)),
            scratch_shapes=[pltpu.VMEM((d, ff), F32), pltpu.VMEM((d, ff), F32), pltpu.VMEM((ff, d), F32),
                            pltpu.VMEM((d, ff), BF16), pltpu.VMEM((d, ff), BF16), pltpu.VMEM((ff, d), BF16),
                            pltpu.SemaphoreType.DMA((3,))]),
        out_shape=jax.ShapeDtypeStruct((n_rows, dp), jnp.uint32),
        compiler_params=_cparams("arbitrary"),
        name="moe_experts",
    )(block_expert, n_used, next_expert, x_rows, w_gate, w_up, w_down)

    return pl.pallas_call(
        _combine_kernel,
        grid_spec=pltpu.PrefetchScalarGridSpec(
            num_scalar_prefetch=1,
            grid=(nt,),
            in_specs=[pl.BlockSpec((tm, d), lambda i, *_: (i, 0)),
                      pl.BlockSpec((tm, V7X_SUBLANES), lambda i, *_: (i, 0)),
                      pl.BlockSpec(memory_space=pl.ANY)],
            out_specs=pl.BlockSpec((tm, d), lambda i, *_: (i, 0)),
            scratch_shapes=[pltpu.VMEM((2, local_rows, dp), jnp.uint32), pltpu.SemaphoreType.DMA((2,))]),
        out_shape=jax.ShapeDtypeStruct((t, d), F32),
        compiler_params=_cparams("arbitrary"),
        name="moe_combine",
    )(gather_chunk, h, crec, y_rows)


def _rope_tables(positions):
    half = DIFF_HEAD_DIM // 2
    inv = ROPE_THETA ** (-jnp.arange(0, DIFF_HEAD_DIM, 2, dtype=F32) / DIFF_HEAD_DIM)
    ang = positions.astype(F32).reshape(-1, 1) * inv
    cos, sin = jnp.cos(ang), jnp.sin(ang)
    cos_t = jnp.concatenate([cos] * 4, axis=1)
    sin_t = jnp.concatenate([-sin, sin, -sin, sin], axis=1)
    assert cos_t.shape[1] == 4 * half
    return cos_t, sin_t


def kernel(x, positions, norm_mix, norm_ffn, ssd_w_in, ssd_conv_w, ssd_conv_b, ssd_dt_bias, ssd_a_log, ssd_d, ssd_norm, ssd_w_out, sb_w_in, sb_w_out, lru_w_in, lru_conv_w, lru_conv_b, lru_w_rg, lru_b_rg, lru_w_ig, lru_b_ig, lru_lambda, lru_w_out, diff_w_in, diff_q_norm, diff_k_norm, diff_lam_q1, diff_lam_k1, diff_lam_q2, diff_lam_k2, diff_sub_norm, diff_w_out, moe_w_group, moe_b_group, moe_w_expert, moe_b_expert, moe_w_gate, moe_w_up, moe_w_down):
    bsz, seqlen, d = x.shape
    depth = norm_mix.shape[0]
    h = x.reshape(bsz * seqlen, d)
    cos_t, sin_t = _rope_tables(positions)
    for i in range(depth):
        m, j = i % N_MIXERS, i // N_MIXERS
        if m == 0:
            h = _ssd_mixer(h, bsz, seqlen, norm_mix[i], ssd_w_in[j], ssd_conv_w[j], ssd_conv_b[j],
                           ssd_dt_bias[j], ssd_a_log[j], ssd_d[j], ssd_norm[j], ssd_w_out[j])
        elif m == 1:
            h = _sb_mixer(h, bsz, seqlen, norm_mix[i], sb_w_in[j], sb_w_out[j])
        elif m == 2:
            h = _lru_mixer(h, bsz, seqlen, norm_mix[i], lru_w_in[j], lru_conv_w[j], lru_conv_b[j],
                           lru_w_rg[j], lru_b_rg[j], lru_w_ig[j], lru_b_ig[j], lru_lambda[j],
                           lru_w_out[j])
        else:
            lambda_init = 0.8 - 0.6 * math.exp(-0.3 * i)
            h = _diff_mixer(h, bsz, seqlen, cos_t, sin_t, norm_mix[i], diff_w_in[j], diff_q_norm[j],
                            diff_k_norm[j], diff_lam_q1[j], diff_lam_k1[j], diff_lam_q2[j],
                            diff_lam_k2[j], diff_sub_norm[j], diff_w_out[j], lambda_init)
        h = _hier_moe(h, norm_ffn[i], moe_w_group[i], moe_b_group[i], moe_w_expert[i],
                      moe_b_expert[i], i, moe_w_gate, moe_w_up, moe_w_down)
    return h.reshape(bsz, seqlen, d)
```

```python
import functools
import math

import jax
import jax.numpy as jnp
from jax import lax
from jax.experimental import pallas as pl
from jax.experimental.pallas import tpu as pltpu

F32 = jnp.float32
BF16 = jnp.bfloat16
HIGHEST = lax.Precision.HIGHEST

RMS_EPS = 1e-6
_LOG2_E = math.log2(math.e)
N_MIXERS = 4

V7X_VMEM_BYTES = 64 * 1024 * 1024
V7X_LANES = 128
V7X_SUBLANES = 8
VMEM_LIMIT = (V7X_VMEM_BYTES * 7) // 8

SSD_HEAD_DIM = 64
SSD_GROUPS = 8
SSD_STATE = 128
SSD_CONV = 4
SSD_CHUNK = 128
SB_HEAD_DIM = 64
LRU_BLOCKS = 16
LRU_CONV = 4
LRU_C = 8.0
DIFF_HEAD_DIM = 64
ROPE_THETA = 10000.0
MOE_GROUPS = 4
MOE_EXPERTS_PER_GROUP = 8
MOE_EXPERTS = MOE_GROUPS * MOE_EXPERTS_PER_GROUP

ROW_TILE = 512
SSD_IN_TILE = 512
SSD_CONV_CHUNK = 512
ATTN_TILE = 256
SB_PAIRS_PER_STEP = 8
DIFF_HEADS_PER_STEP = 8
LRU_TIME_TILE = 64
LRU_PITCH_PAD = 4
LRU_BAND = 3 * V7X_LANES
LRU_SCAN_UNROLL = 4
MOE_TOKEN_TILE = 512
MOE_ROW_BLOCK = 512
MOE_CHUNK = V7X_SUBLANES
CONV_HALO = V7X_SUBLANES
DMA_LOOP_UNROLL = 8


def _cparams(*sem):
    return pltpu.CompilerParams(dimension_semantics=sem, vmem_limit_bytes=VMEM_LIMIT)


def _const_spec(shape):
    nd = len(shape)
    return pl.BlockSpec(shape, lambda *_: (0,) * nd)


def _resident_spec(shape):
    nd = len(shape)
    return pl.BlockSpec(shape, lambda *_: (0,) * nd, pipeline_mode=pl.Buffered(1))


def _rms(x, w):
    ms = jnp.mean(x * x, axis=-1, keepdims=True)
    return x * lax.rsqrt(ms + RMS_EPS) * w


def _sigmoid(x):
    return 1.0 / (1.0 + jnp.exp(-x))


def _softplus(x):
    return jnp.maximum(x, 0.0) + jnp.log(1.0 + jnp.exp(-jnp.abs(x)))


def _dot(a, b, **kw):
    return jnp.dot(a, b, preferred_element_type=F32, **kw)


def _dot_nt(a, b):
    return lax.dot_general(a, b, (((1,), (1,)), ((), ())), preferred_element_type=F32)


def _dot_tn(a, b):
    return lax.dot_general(a, b, (((0,), (0,)), ((), ())), preferred_element_type=F32)


def _split_bf16(x):
    hi = x.astype(BF16)
    return hi, (x - hi.astype(F32)).astype(BF16)


def _split_weight(w):
    return jnp.concatenate(_split_bf16(w), axis=1)


def _dot_split(x, x_hi, w2_ref):
    n = w2_ref.shape[1] // 2
    x_lo = (x - x_hi.astype(F32)).astype(BF16)
    t = _dot(x_hi, w2_ref[...])
    return t[:, :n] + t[:, n:] + _dot(x_lo, w2_ref[:, :n])


def _proj_residual_kernel(y_ref, w_ref, h_ref, o_ref):
    o_ref[...] = h_ref[...] + _dot(y_ref[...], w_ref[...])


def _proj_residual(y, w, h):
    t, k = y.shape
    d = w.shape[1]
    tm = ROW_TILE
    return pl.pallas_call(
        _proj_residual_kernel,
        grid=(t // tm,),
        in_specs=[pl.BlockSpec((tm, k), lambda i: (i, 0)),
                  _const_spec((k, d)),
                  pl.BlockSpec((tm, d), lambda i: (i, 0))],
        out_specs=pl.BlockSpec((tm, d), lambda i: (i, 0)),
        out_shape=jax.ShapeDtypeStruct((t, d), F32),
        compiler_params=_cparams("arbitrary"),
        name="proj_residual",
    )(y, w, h)


def _ssd_in_kernel(h_ref, g_ref, wz_ref, wx_ref, wdt_ref, cw_ref, cb_ref,
                   z_ref, xbc_ref, dt_ref, pad_ref, *, tiles_per_seq):
    i = pl.program_id(0)
    tm = h_ref.shape[0]
    u32 = _rms(h_ref[...], g_ref[...])
    u = u32.astype(BF16)
    z_ref[...] = _dot(u, wz_ref[...]).astype(z_ref.dtype)
    dt_ref[...] = _dot_split(u32, u, wdt_ref)

    @pl.when(i % tiles_per_seq == 0)
    def _():
        pad_ref[0:CONV_HALO, :] = jnp.zeros((CONV_HALO, pad_ref.shape[1]), F32)

    conv_dim = pad_ref.shape[1]
    for c0 in range(0, conv_dim, SSD_CONV_CHUNK):
        cs = slice(c0, c0 + SSD_CONV_CHUNK)
        pad_ref[CONV_HALO:CONV_HALO + tm, cs] = _dot(u, wx_ref[:, cs])
        acc = cb_ref[:, cs] + cw_ref[SSD_CONV - 1:SSD_CONV, cs] * pad_ref[CONV_HALO:CONV_HALO + tm, cs]
        for k in range(SSD_CONV - 1):
            off = CONV_HALO - (SSD_CONV - 1) + k
            acc = acc + cw_ref[k:k + 1, cs] * pad_ref[off:off + tm, cs]
        xbc_ref[:, cs] = (acc * _sigmoid(acc)).astype(xbc_ref.dtype)
        pad_ref[0:CONV_HALO, cs] = pad_ref[tm:tm + CONV_HALO, cs]


def _ssd_scan_kernel(xs_ref, b_ref, c_ref, z_ref, dt_ref, dtt_ref,
                     bias_ref, biast_ref, alog_ref, alogt_ref, dfull_ref, nw_ref,
                     o_ref, state_ref):
    q = xs_ref.shape[0]
    n_heads = dt_ref.shape[1]
    d_inner = xs_ref.shape[1]
    heads_per_group = n_heads // SSD_GROUPS
    gw = d_inner // SSD_GROUPS
    p = SSD_HEAD_DIM

    @pl.when(pl.program_id(1) == 0)
    def _():
        state_ref[...] = jnp.zeros(state_ref.shape, F32)

    dt = _softplus(dt_ref[...] + bias_ref[...])
    a = -jnp.exp(alog_ref[...]) * dt
    dtt = _softplus(dtt_ref[...] + biast_ref[...])
    at = -jnp.exp(alogt_ref[...]) * dtt

    row = lax.broadcasted_iota(jnp.int32, (q, q), 0)
    col = lax.broadcasted_iota(jnp.int32, (q, q), 1)
    lower = row >= col
    a_cum = _dot(lower.astype(F32), a, precision=HIGHEST)
    a_cumt = _dot(at, (row <= col).astype(F32), precision=HIGHEST)
    a_tot = a_cum[q - 1:q, :]

    hh = lax.broadcasted_iota(jnp.int32, (2 * n_heads, d_inner), 0)
    cc = lax.broadcasted_iota(jnp.int32, (2 * n_heads, d_inner), 1)
    expand = (cc // p == hh % n_heads).astype(BF16)
    per_head = jnp.concatenate(
        [dt, jnp.exp(a_cum), jnp.exp(a_tot - a_cum),
         jnp.broadcast_to(jnp.exp(a_tot), (V7X_SUBLANES, n_heads))], axis=0)
    per_channel = _dot(jnp.concatenate(_split_bf16(per_head), axis=1), expand)
    dt_full = per_channel[0:q]
    dec_out_full = per_channel[q:2 * q]
    dec_st_full = per_channel[2 * q:3 * q]
    chunk_dec_full = per_channel[3 * q:3 * q + 1]

    lane = lax.broadcasted_iota(jnp.int32, (1, gw), 1)
    for g in range(SSD_GROUPS):
        sl = slice(g * gw, (g + 1) * gw)
        bg = b_ref[:, g * SSD_STATE:(g + 1) * SSD_STATE]
        cg = c_ref[:, g * SSD_STATE:(g + 1) * SSD_STATE]
        xs_g = xs_ref[:, sl].astype(F32)
        xdt = xs_g * dt_full[:, sl]
        cb = _dot_nt(cg, bg)
        s_prev = state_ref[g]
        y = _dot(cg, s_prev.astype(BF16)) * dec_out_full[:, sl]
        for r in range(heads_per_group):
            hd = g * heads_per_group + r
            seg = a_cum[:, hd:hd + 1] - a_cumt[hd:hd + 1, :]
            m = jnp.where(lower, cb * jnp.exp(seg), 0.0).astype(BF16)
            xr = jnp.where((lane >= r * p) & (lane < (r + 1) * p), xdt, 0.0).astype(BF16)
            y = y + _dot(m, xr)
        xw = (xdt * dec_st_full[:, sl]).astype(BF16)
        state_ref[g] = s_prev * chunk_dec_full[:, sl] + _dot_tn(bg, xw)
        y = y + xs_g * dfull_ref[:, sl]
        zg = z_ref[:, sl].astype(F32)
        yz = y * (zg * _sigmoid(zg))
        o_ref[:, sl] = _rms(yz, nw_ref[:, sl]).astype(o_ref.dtype)


def _ssd_mixer(h, bsz, seqlen, g_norm, w_in, conv_w, conv_b, dt_bias, a_log, d_skip, norm_w, w_out):
    t, d = h.shape
    n_heads = dt_bias.shape[0]
    d_inner = n_heads * SSD_HEAD_DIM
    gn = SSD_GROUPS * SSD_STATE
    conv_dim = d_inner + 2 * gn
    wz = w_in[:, :d_inner].astype(BF16)
    wx = w_in[:, d_inner:d_inner + conv_dim].astype(BF16)
    wdt = _split_weight(w_in[:, d_inner + conv_dim:])
    tm = min(SSD_IN_TILE, seqlen)
    assert seqlen % tm == 0, "a row tile must not straddle two sequences (conv history)"
    z, xbc, dt_raw = pl.pallas_call(
        functools.partial(_ssd_in_kernel, tiles_per_seq=seqlen // tm),
        grid=(t // tm,),
        in_specs=[pl.BlockSpec((tm, d), lambda i: (i, 0)),
                  _const_spec((1, d)),
                  _resident_spec((d, d_inner)),
                  _resident_spec((d, conv_dim)),
                  _const_spec((d, 2 * n_heads)),
                  _const_spec((SSD_CONV, conv_dim)),
                  _const_spec((1, conv_dim))],
        out_specs=[pl.BlockSpec((tm, d_inner), lambda i: (i, 0)),
                   pl.BlockSpec((tm, conv_dim), lambda i: (i, 0)),
                   pl.BlockSpec((tm, n_heads), lambda i: (i, 0))],
        out_shape=[jax.ShapeDtypeStruct((t, d_inner), BF16),
                   jax.ShapeDtypeStruct((t, conv_dim), BF16),
                   jax.ShapeDtypeStruct((t, n_heads), F32)],
        scratch_shapes=[pltpu.VMEM((tm + CONV_HALO, conv_dim), F32)],
        compiler_params=_cparams("arbitrary"),
        name="ssd_in",
    )(h, g_norm.reshape(1, d), wz, wx, wdt, conv_w, conv_b.reshape(1, conv_dim))

    q = SSD_CHUNK
    nc = seqlen // q
    xbc3 = xbc.reshape(bsz, seqlen, conv_dim)
    dt3 = dt_raw.reshape(bsz, seqlen, n_heads)
    dtt3 = jnp.swapaxes(dt3, 1, 2)
    nb_x = d_inner // gn
    yg = pl.pallas_call(
        _ssd_scan_kernel,
        grid=(bsz, nc),
        in_specs=[pl.BlockSpec((None, q, d_inner), lambda b, c: (b, c, 0)),
                  pl.BlockSpec((None, q, gn), lambda b, c: (b, c, nb_x)),
                  pl.BlockSpec((None, q, gn), lambda b, c: (b, c, nb_x + 1)),
                  pl.BlockSpec((None, q, d_inner), lambda b, c: (b, c, 0)),
                  pl.BlockSpec((None, q, n_heads), lambda b, c: (b, c, 0)),
                  pl.BlockSpec((None, n_heads, q), lambda b, c: (b, 0, c)),
                  _const_spec((1, n_heads)), _const_spec((n_heads, 1)),
                  _const_spec((1, n_heads)), _const_spec((n_heads, 1)),
                  _const_spec((1, d_inner)), _const_spec((1, d_inner))],
        out_specs=pl.BlockSpec((None, q, d_inner), lambda b, c: (b, c, 0)),
        out_shape=jax.ShapeDtypeStruct((bsz, seqlen, d_inner), BF16),
        scratch_shapes=[pltpu.VMEM((SSD_GROUPS, SSD_STATE, d_inner // SSD_GROUPS), F32)],
        compiler_params=_cparams("arbitrary", "arbitrary"),
        name="ssd_scan",
    )(xbc3, xbc3, xbc3, z.reshape(bsz, seqlen, d_inner), dt3, dtt3,
      dt_bias.reshape(1, n_heads), dt_bias.reshape(n_heads, 1),
      a_log.reshape(1, n_heads), a_log.reshape(n_heads, 1),
      jnp.repeat(d_skip, SSD_HEAD_DIM).reshape(1, d_inner), norm_w.reshape(1, d_inner))
    return _proj_residual(yg.reshape(t, d_inner), w_out.astype(BF16), h)


def _sb_in_kernel(h_ref, g_ref, w_ref, o_ref, *, q_cols, q_scale):
    u = _rms(h_ref[...], g_ref[...]).astype(BF16)
    o_ref[:, :q_cols] = (_dot(u, w_ref[:, :q_cols]) * q_scale).astype(o_ref.dtype)
    o_ref[:, q_cols:] = _dot(u, w_ref[:, q_cols:]).astype(o_ref.dtype)


def _sb_attn_kernel(q_ref, k_ref, v_ref, o_ref, acc_ref, *, tile):
    seqlen = q_ref.shape[0]
    nq = seqlen // tile
    pw = 2 * SB_HEAD_DIM
    n_pairs = q_ref.shape[1] // pw
    lane = lax.broadcasted_iota(jnp.int32, (1, pw), 1)
    first = lane < SB_HEAD_DIM
    row = lax.broadcasted_iota(jnp.int32, (tile, tile), 0)
    col = lax.broadcasted_iota(jnp.int32, (tile, tile), 1)
    neg_from = jnp.where(row >= col, -1.0, 0.0).astype(BF16)
    past = col < row
    past2 = jnp.concatenate([past, past], axis=0)

    def k_tile(qs, kj, csums, diagonal):
        rows = pl.ds(pl.multiple_of(kj * tile, tile), tile)
        out = []
        for p in range(n_pairs):
            k = k_ref[rows, p * pw:(p + 1) * pw]
            v = v_ref[rows, p * pw:(p + 1) * pw]
            t = _dot_nt(qs[p], k)
            neg_abs = pltpu.bitcast(pltpu.bitcast(t, jnp.uint32) | jnp.uint32(0x80000000), F32)
            sp = jnp.maximum(t, 0.0) + jnp.log(1.0 + jnp.exp2(neg_abs)) * _LOG2_E
            if diagonal:
                sp = jnp.where(past2, sp, 0.0)
            w = jnp.exp2(t + _dot(sp.astype(BF16), neg_from) + csums[p])
            if diagonal:
                w = jnp.where(past2, w, 0.0)
            acc_ref[p] += _dot(w.astype(BF16), v)
            out.append(csums[p] - jnp.sum(sp, axis=1, keepdims=True))
        return tuple(out)

    def q_body(qi, _):
        qrows = pl.ds(pl.multiple_of(qi * tile, tile), tile)
        qs = []
        for p in range(n_pairs):
            q = q_ref[qrows, p * pw:(p + 1) * pw]
            zero = jnp.zeros_like(q)
            qs.append(jnp.concatenate([jnp.where(first, q, zero), jnp.where(first, zero, q)], axis=0))
            acc_ref[p] = jnp.zeros(acc_ref.shape[1:], F32)
        init = tuple(jnp.zeros((2 * tile, 1), F32) for _ in range(n_pairs))
        csums = k_tile(qs, qi, init, True)
        lax.fori_loop(0, qi, lambda s, c: k_tile(qs, qi - 1 - s, c, False), csums)
        for p in range(n_pairs):
            o = jnp.where(first, acc_ref[p, 0:tile, :], acc_ref[p, tile:2 * tile, :])
            o_ref[qrows, p * pw:(p + 1) * pw] = o.astype(o_ref.dtype)
        return 0

    lax.fori_loop(0, nq, q_body, 0)


def _sb_mixer(h, bsz, seqlen, g_norm, w_in, w_out):
    t, d = h.shape
    hd = w_in.shape[1] // 3
    tm = ROW_TILE
    qkv = pl.pallas_call(
        functools.partial(_sb_in_kernel, q_cols=hd, q_scale=SB_HEAD_DIM ** -0.5 * _LOG2_E),
        grid=(t // tm,),
        in_specs=[pl.BlockSpec((tm, d), lambda i: (i, 0)),
                  _const_spec((1, d)),
                  _const_spec((d, 3 * hd))],
        out_specs=pl.BlockSpec((tm, 3 * hd), lambda i: (i, 0)),
        out_shape=jax.ShapeDtypeStruct((t, 3 * hd), BF16),
        compiler_params=_cparams("arbitrary"),
        name="sb_in",
    )(h, g_norm.reshape(1, d), w_in.astype(BF16))
    qkv3 = qkv.reshape(bsz, seqlen, 3 * hd)
    bw = SB_PAIRS_PER_STEP * 2 * SB_HEAD_DIM
    nb = hd // bw
    tile = min(ATTN_TILE, seqlen)
    o = pl.pallas_call(
        functools.partial(_sb_attn_kernel, tile=tile),
        grid=(bsz, nb),
        in_specs=[pl.BlockSpec((None, seqlen, bw), lambda b, j: (b, 0, j)),
                  pl.BlockSpec((None, seqlen, bw), lambda b, j: (b, 0, nb + j)),
                  pl.BlockSpec((None, seqlen, bw), lambda b, j: (b, 0, 2 * nb + j))],
        out_specs=pl.BlockSpec((None, seqlen, bw), lambda b, j: (b, 0, j)),
        out_shape=jax.ShapeDtypeStruct((bsz, seqlen, hd), BF16),
        scratch_shapes=[pltpu.VMEM((SB_PAIRS_PER_STEP, 2 * tile, 2 * SB_HEAD_DIM), F32)],
        compiler_params=_cparams("arbitrary", "arbitrary"),
        name="sb_attn",
    )(qkv3, qkv3, qkv3)
    return _proj_residual(o.reshape(t, hd), w_out.astype(BF16), h)


def _lru_kernel(h_ref, g_ref, win_ref, cw_ref, cb_ref, wband_ref, br_ref, bi_ref, lam_ref,
                y_ref, pad_ref, xc_ref, a_ref, b_ref, hs_ref, carry_ref, *, band_starts):
    bsz, tt, d = h_ref.shape
    width = y_ref.shape[2]

    @pl.when(pl.program_id(0) == 0)
    def _():
        carry_ref[...] = jnp.zeros(carry_ref.shape, F32)
        for b in range(bsz):
            pad_ref[b, 0:CONV_HALO, :] = jnp.zeros((CONV_HALO, width), F32)

    u = _rms(h_ref[...].reshape(bsz * tt, d), g_ref[...]).astype(BF16)
    proj = _dot(u, win_ref[...])
    gate_branch = proj[:, :width]
    for b in range(bsz):
        pad_ref[b, CONV_HALO:CONV_HALO + tt, :] = proj[b * tt:(b + 1) * tt, width:]
        acc = cb_ref[...] + cw_ref[LRU_CONV - 1:LRU_CONV, :] * pad_ref[b, CONV_HALO:CONV_HALO + tt, :]
        for k in range(LRU_CONV - 1):
            off = CONV_HALO - (LRU_CONV - 1) + k
            acc = acc + cw_ref[k:k + 1, :] * pad_ref[b, off:off + tt, :]
        xc_ref[b * tt:(b + 1) * tt, :] = acc
        pad_ref[b, 0:CONV_HALO, :] = pad_ref[b, tt:tt + CONV_HALO, :]

    xcb = xc_ref[...].astype(BF16)
    n_lt = width // V7X_LANES
    pitch = a_ref.shape[1] // bsz
    neg_c_softplus = -LRU_C * _softplus(-lam_ref[...])
    for c in range(n_lt):
        cs = slice(c * V7X_LANES, (c + 1) * V7X_LANES)
        pre = _dot(xcb[:, band_starts[c]:band_starts[c] + LRU_BAND], wband_ref[c])
        r = _sigmoid(pre[:, :V7X_LANES] + br_ref[:, cs])
        ig = _sigmoid(pre[:, V7X_LANES:] + bi_ref[:, cs])
        a = jnp.exp(neg_c_softplus[:, cs] * r)
        bterm = jnp.sqrt(1.0 - a * a) * (ig * xc_ref[:, cs])
        for b in range(bsz):
            a_ref[c, b * pitch:b * pitch + tt, :] = a[b * tt:(b + 1) * tt, :]
            b_ref[c, b * pitch:b * pitch + tt, :] = bterm[b * tt:(b + 1) * tt, :]

    def step(j, hprev):
        hnew = []
        for c in range(n_lt):
            rows_j = pl.ds(j, bsz, stride=pitch)
            hc = a_ref[c, rows_j, :] * hprev[c] + b_ref[c, rows_j, :]
            hs_ref[c, rows_j, :] = hc
            hnew.append(hc)
        return tuple(hnew)

    hlast = lax.fori_loop(0, tt, step, tuple(carry_ref[c] for c in range(n_lt)), unroll=LRU_SCAN_UNROLL)
    for c in range(n_lt):
        carry_ref[c] = hlast[c]
    c0 = math.sqrt(2.0 / math.pi)
    gelu = 0.5 * gate_branch * (1.0 + jnp.tanh(c0 * (gate_branch + 0.044715 * gate_branch ** 3)))
    for b in range(bsz):
        hs_b = jnp.concatenate([hs_ref[c, b * pitch:b * pitch + tt, :] for c in range(n_lt)], axis=1)
        y_ref[b] = (hs_b * gelu[b * tt:(b + 1) * tt, :]).astype(y_ref.dtype)


def _lru_band_starts(width):
    bd = width // LRU_BLOCKS
    starts = []
    for j in range(width // V7X_LANES):
        b_lo = (j * V7X_LANES) // bd
        b_hi = ((j + 1) * V7X_LANES - 1) // bd
        lo = (b_lo * bd) // V7X_LANES * V7X_LANES
        assert (b_hi + 1) * bd - lo <= LRU_BAND
        starts.append(min(lo, width - LRU_BAND))
    return tuple(starts)


def _lru_band_weights(w_rg, w_ig, starts):
    def dense(w):
        nb, bd, _ = w.shape
        eye = jnp.eye(nb, dtype=w.dtype)
        return (eye[:, None, :, None] * w[:, :, None, :]).reshape(nb * bd, nb * bd)

    dense_r, dense_i = dense(w_rg), dense(w_ig)
    tiles = []
    for j, s in enumerate(starts):
        cs = slice(j * V7X_LANES, (j + 1) * V7X_LANES)
        tiles.append(jnp.concatenate([dense_r[s:s + LRU_BAND, cs], dense_i[s:s + LRU_BAND, cs]], axis=1))
    return jnp.stack(tiles).astype(BF16)


def _lru_mixer(h, bsz, seqlen, g_norm, w_in, conv_w, conv_b, w_rg, b_rg, w_ig, b_ig, lam, w_out):
    t, d = h.shape
    width = lam.shape[0]
    tt = min(LRU_TIME_TILE, seqlen)
    rows = bsz * tt
    n_lt = width // V7X_LANES
    scan_rows = bsz * (tt + LRU_PITCH_PAD)
    starts = _lru_band_starts(width)
    y = pl.pallas_call(
        functools.partial(_lru_kernel, band_starts=starts),
        grid=(seqlen // tt,),
        in_specs=[pl.BlockSpec((bsz, tt, d), lambda i: (0, i, 0)),
                  _const_spec((1, d)),
                  _const_spec((d, 2 * width)),
                  _const_spec((LRU_CONV, width)), _const_spec((1, width)),
                  _const_spec((n_lt, LRU_BAND, 2 * V7X_LANES)),
                  _const_spec((1, width)), _const_spec((1, width)),
                  _const_spec((1, width))],
        out_specs=pl.BlockSpec((bsz, tt, width), lambda i: (0, i, 0)),
        out_shape=jax.ShapeDtypeStruct((bsz, seqlen, width), BF16),
        scratch_shapes=[pltpu.VMEM((bsz, tt + CONV_HALO, width), F32),
                        pltpu.VMEM((rows, width), F32),
                        pltpu.VMEM((n_lt, scan_rows, V7X_LANES), F32),
                        pltpu.VMEM((n_lt, scan_rows, V7X_LANES), F32),
                        pltpu.VMEM((n_lt, scan_rows, V7X_LANES), F32),
                        pltpu.VMEM((n_lt, bsz, V7X_LANES), F32)],
        compiler_params=_cparams("arbitrary"),
        name="lru",
    )(h.reshape(bsz, seqlen, d), g_norm.reshape(1, d), w_in.astype(BF16),
      conv_w, conv_b.reshape(1, width), _lru_band_weights(w_rg, w_ig, starts),
      b_rg.reshape(1, width), b_ig.reshape(1, width), lam.reshape(1, width))
    return _proj_residual(y.reshape(t, width), w_out.astype(BF16), h)


def _diff_in_kernel(h_ref, g_ref, w_ref, cos_ref, sin_ref, qn_ref, kn_ref, q_ref, k_ref, v_ref,
                    *, q_scale):
    d_q = q_ref.shape[1]
    hdim = DIFF_HEAD_DIM
    n_grp = d_q // hdim
    u = _rms(h_ref[...], g_ref[...]).astype(BF16)
    v_ref[...] = _dot(u, w_ref[:, 2 * d_q:]).astype(v_ref.dtype)

    ci = lax.broadcasted_iota(jnp.int32, (d_q, n_grp), 0)
    gi = lax.broadcasted_iota(jnp.int32, (d_q, n_grp), 1)
    gsum = (ci // hdim == gi).astype(BF16)
    gj = lax.broadcasted_iota(jnp.int32, (2 * n_grp, d_q), 0)
    cj = lax.broadcasted_iota(jnp.int32, (2 * n_grp, d_q), 1)
    gexp = (cj // hdim == gj % n_grp).astype(BF16)
    lane = lax.broadcasted_iota(jnp.int32, (1, d_q), 1)
    first_half = (lane % hdim) < (hdim // 2)
    reps = d_q // cos_ref.shape[1]
    cos_t = jnp.tile(cos_ref[...], (1, reps))
    sin_t = jnp.tile(sin_ref[...], (1, reps))

    def norm_rope(x, nw):
        sq = x * x
        sq_hi = sq.astype(BF16)
        sq_lo = (sq - sq_hi.astype(F32)).astype(BF16)
        ms = (_dot(sq_hi, gsum) + _dot(sq_lo, gsum)) * (1.0 / hdim)
        inv = lax.rsqrt(ms + RMS_EPS)
        inv_hi = inv.astype(BF16)
        inv_lo = (inv - inv_hi.astype(F32)).astype(BF16)
        inv_full = _dot(jnp.concatenate([inv_hi, inv_lo], axis=1), gexp)
        xn = x * inv_full * nw
        rot = jnp.where(first_half, pltpu.roll(xn, d_q - hdim // 2, 1), pltpu.roll(xn, hdim // 2, 1))
        return xn * cos_t + rot * sin_t

    q_ref[...] = (norm_rope(_dot(u, w_ref[:, :d_q]), qn_ref[...]) * q_scale).astype(q_ref.dtype)
    k_ref[...] = norm_rope(_dot(u, w_ref[:, d_q:2 * d_q]), kn_ref[...]).astype(k_ref.dtype)


def _diff_attn_kernel(q_ref, k_ref, v_ref, lq1_ref, lk1_ref, lq2_ref, lk2_ref, sn_ref, o_ref,
                      acc_ref, *, tile, lambda_init):
    seqlen = q_ref.shape[0]
    nq = seqlen // tile
    hdim = DIFF_HEAD_DIM
    hw = 2 * hdim
    n_heads = q_ref.shape[1] // hw
    lane = lax.broadcasted_iota(jnp.int32, (1, hw), 1)
    first = lane < hdim
    row = lax.broadcasted_iota(jnp.int32, (tile, tile), 0)
    col = lax.broadcasted_iota(jnp.int32, (tile, tile), 1)
    causal = col <= row
    causal2 = jnp.concatenate([causal, causal], axis=0)
    ones = jnp.ones((tile, hw), BF16)
    lam = (jnp.exp(jnp.sum(lq1_ref[...] * lk1_ref[...], axis=1, keepdims=True))
           - jnp.exp(jnp.sum(lq2_ref[...] * lk2_ref[...], axis=1, keepdims=True)) + lambda_init)

    def k_tile(qs, kj, maxes, diagonal):
        rows = pl.ds(pl.multiple_of(kj * tile, tile), tile)
        out = []
        for hd in range(n_heads):
            k = k_ref[rows, hd * hw:(hd + 1) * hw]
            v1 = jnp.concatenate([v_ref[rows, hd * hw:(hd + 1) * hw], ones], axis=1)
            s = _dot_nt(qs[hd], k)
            if diagonal:
                s = jnp.where(causal2, s, -jnp.inf)
            mx_new = jnp.maximum(maxes[hd], jnp.max(s, axis=1, keepdims=True))
            alpha = jnp.exp2(maxes[hd] - mx_new)
            pr = jnp.exp2(s - mx_new)
            acc_ref[hd] = acc_ref[hd] * alpha + _dot(pr.astype(BF16), v1)
            out.append(mx_new)
        return tuple(out)

    def q_body(qi, _):
        qrows = pl.ds(pl.multiple_of(qi * tile, tile), tile)
        qs = []
        for hd in range(n_heads):
            q = q_ref[qrows, hd * hw:(hd + 1) * hw]
            zero = jnp.zeros_like(q)
            qs.append(jnp.concatenate([jnp.where(first, q, zero), jnp.where(first, zero, q)], axis=0))
            acc_ref[hd] = jnp.zeros(acc_ref.shape[1:], F32)
        init = tuple(jnp.full((2 * tile, 1), -jnp.inf, F32) for _ in range(n_heads))
        maxes = k_tile(qs, qi, init, True)
        lax.fori_loop(0, qi, lambda s, c: k_tile(qs, qi - 1 - s, c, False), maxes)
        for hd in range(n_heads):
            p = acc_ref[hd, :, 0:hw] / acc_ref[hd, :, hw:2 * hw]
            o = p[0:tile] - lam * p[tile:2 * tile]
            o = _rms(o, sn_ref[...]) * (1.0 - lambda_init)
            o_ref[qrows, hd * hw:(hd + 1) * hw] = o.astype(o_ref.dtype)
        return 0

    lax.fori_loop(0, nq, q_body, 0)


def _diff_mixer(h, bsz, seqlen, cos_t, sin_t, g_norm, w_in, q_norm, k_norm, lq1, lk1, lq2, lk2,
                sub_norm, w_out, lambda_init):
    t, d = h.shape
    d_q = w_in.shape[1] // 3
    hdim = DIFF_HEAD_DIM
    tm = ROW_TILE
    tw = cos_t.shape[1]
    q, k, v = pl.pallas_call(
        functools.partial(_diff_in_kernel, q_scale=hdim ** -0.5 * _LOG2_E),
        grid=(t // tm,),
        in_specs=[pl.BlockSpec((tm, d), lambda i: (i, 0)),
                  _const_spec((1, d)),
                  _const_spec((d, 3 * d_q)),
                  pl.BlockSpec((tm, tw), lambda i: (i, 0)),
                  pl.BlockSpec((tm, tw), lambda i: (i, 0)),
                  _const_spec((1, d_q)), _const_spec((1, d_q))],
        out_specs=[pl.BlockSpec((tm, d_q), lambda i: (i, 0))] * 3,
        out_shape=[jax.ShapeDtypeStruct((t, d_q), BF16)] * 3,
        compiler_params=_cparams("arbitrary"),
        name="diff_in",
    )(h, g_norm.reshape(1, d), w_in.astype(BF16), cos_t, sin_t,
      jnp.tile(q_norm, d_q // hdim).reshape(1, d_q), jnp.tile(k_norm, d_q // hdim).reshape(1, d_q))
    hw = 2 * hdim
    bw = DIFF_HEADS_PER_STEP * hw
    tile = min(ATTN_TILE, seqlen)
    spec = pl.BlockSpec((None, seqlen, bw), lambda b, j: (b, 0, j))
    vec = _const_spec((1, hdim))
    o = pl.pallas_call(
        functools.partial(_diff_attn_kernel, tile=tile, lambda_init=lambda_init),
        grid=(bsz, d_q // bw),
        in_specs=[spec, spec, spec, vec, vec, vec, vec, _const_spec((1, hw))],
        out_specs=spec,
        out_shape=jax.ShapeDtypeStruct((bsz, seqlen, d_q), BF16),
        scratch_shapes=[pltpu.VMEM((DIFF_HEADS_PER_STEP, 2 * tile, 2 * hw), F32)],
        compiler_params=_cparams("arbitrary", "arbitrary"),
        name="diff_attn",
    )(q.reshape(bsz, seqlen, d_q), k.reshape(bsz, seqlen, d_q), v.reshape(bsz, seqlen, d_q),
      lq1.reshape(1, hdim), lk1.reshape(1, hdim), lq2.reshape(1, hdim), lk2.reshape(1, hdim),
      sub_norm.reshape(1, hw))
    return _proj_residual(o.reshape(t, d_q), w_out.astype(BF16), h)


_R_E1, _R_E2, _R_W1, _R_W2, _R_RANK1, _R_RANK2 = range(6)
_C_LOC1, _C_LOC2, _C_W1, _C_W2 = range(4)


def _router_kernel(h_ref, g_ref, w_ref, b_ref, rec_ref, base_ref, cnt_ref, run_ref):
    tm = h_ref.shape[0]
    ne, ng, epg = MOE_EXPERTS, MOE_GROUPS, MOE_EXPERTS_PER_GROUP

    @pl.when(pl.program_id(0) == 0)
    def _():
        run_ref[...] = jnp.zeros(run_ref.shape, F32)

    base_ref[...] = run_ref[...]

    u = _rms(h_ref[...], g_ref[...])
    logits = _dot_split(u, u.astype(BF16), w_ref) + b_ref[...]
    lane = lax.broadcasted_iota(jnp.int32, logits.shape, 1)
    lane_f = lane.astype(F32)
    lane_group = (lane // epg).astype(F32)
    neg = -jnp.inf
    big = float(V7X_LANES)

    gl = jnp.where((lane >= ne) & (lane < ne + ng), logits, neg)
    gmax = jnp.max(gl, axis=1, keepdims=True)
    gsel = jnp.min(jnp.where(gl == gmax, lane_f - ne, big), axis=1, keepdims=True)
    g_gate = 1.0 / jnp.sum(jnp.exp(gl - gmax), axis=1, keepdims=True)

    el = jnp.where((lane < ne) & (lane_group == gsel), logits, neg)
    m1 = jnp.max(el, axis=1, keepdims=True)
    i1 = jnp.min(jnp.where(el == m1, lane_f, big), axis=1, keepdims=True)
    el2 = jnp.where(lane_f == i1, neg, el)
    m2 = jnp.max(el2, axis=1, keepdims=True)
    i2 = jnp.min(jnp.where(el2 == m2, lane_f, big), axis=1, keepdims=True)
    e21 = jnp.exp(m2 - m1)
    w1 = g_gate / (1.0 + e21)
    w2 = g_gate * e21 / (1.0 + e21)

    oh1 = lane_f == i1
    oh2 = lane_f == i2
    oh = jnp.where(oh1 | oh2, 1.0, 0.0)
    row = lax.broadcasted_iota(jnp.int32, (tm, tm), 0)
    col = lax.broadcasted_iota(jnp.int32, (tm, tm), 1)
    before = _dot((col < row).astype(BF16), oh.astype(BF16))
    pos = run_ref[0:1, :] + before
    rank1 = jnp.sum(jnp.where(oh1, pos, 0.0), axis=1, keepdims=True)
    rank2 = jnp.sum(jnp.where(oh2, pos, 0.0), axis=1, keepdims=True)
    run_ref[...] = run_ref[...] + jnp.sum(oh, axis=0, keepdims=True)
    cnt_ref[...] = run_ref[...]

    rec = jnp.zeros(logits.shape, F32)
    for idx, val in ((_R_E1, i1), (_R_E2, i2), (_R_W1, w1), (_R_W2, w2),
                     (_R_RANK1, rank1), (_R_RANK2, rank2)):
        rec = jnp.where(lane == idx, val, rec)
    rec_ref[...] = rec


def _pack_bf16_pair(x):
    n = x.shape[1] // 2
    xr = x.astype(BF16).astype(F32)
    return pltpu.bitcast(xr[:, :n], jnp.uint32) | (pltpu.bitcast(xr[:, n:], jnp.uint32) >> 16)


def _unpack_bf16_pair(w):
    hi = pltpu.bitcast(w & jnp.uint32(0xFFFF0000), F32)
    lo = pltpu.bitcast(w << 16, F32)
    return jnp.concatenate([hi, lo], axis=1)


def _chunk_rows(chunk):
    return pl.ds(pl.multiple_of(chunk * MOE_CHUNK, MOE_CHUNK), MOE_CHUNK)


def _start_chunk_copies(chunk_of_ref, tile, n_chunks, copy_of):
    def issue(j, _):
        copy_of(_chunk_rows(j), _chunk_rows(chunk_of_ref[tile * n_chunks + j])).start()
        return 0

    lax.fori_loop(0, n_chunks, issue, 0, unroll=DMA_LOOP_UNROLL)


def _wait_chunk_copies(n_chunks, copy_of):
    def drain(j, _):
        copy_of(_chunk_rows(0), _chunk_rows(0)).wait()
        return 0

    lax.fori_loop(0, n_chunks, drain, 0, unroll=DMA_LOOP_UNROLL)


def _dispatch_kernel(chunk_of_ref, h_ref, g_ref, loct_ref, rows_in_ref, rows_ref, sorted_ref, sems):
    del rows_in_ref
    i, nt = pl.program_id(0), pl.num_programs(0)
    slot = i % 2
    tm = h_ref.shape[0]
    n_chunks = sorted_ref.shape[1] // MOE_CHUNK

    def copy_of(slot_):
        return lambda lr, gr: pltpu.make_async_copy(sorted_ref.at[slot_, lr], rows_ref.at[gr],
                                                    sems.at[slot_])

    @pl.when(i >= 2)
    def _():
        _wait_chunk_copies(n_chunks, copy_of(slot))

    u = _rms(h_ref[...], g_ref[...]).astype(BF16)
    local_row = lax.broadcasted_iota(jnp.int32, (sorted_ref.shape[1], tm), 0)
    loc = loct_ref[...]
    pick = jnp.where(local_row == loc[0:1, :], 1.0, jnp.where(local_row == loc[1:2, :], 1.0, 0.0))
    sorted_ref[slot] = _pack_bf16_pair(_dot(pick.astype(BF16), u))
    _start_chunk_copies(chunk_of_ref, i, n_chunks, copy_of(slot))

    @pl.when(i == nt - 1)
    def _():
        _wait_chunk_copies(n_chunks, copy_of(slot))

    @pl.when(jnp.logical_and(i == nt - 1, i >= 1))
    def _():
        _wait_chunk_copies(n_chunks, copy_of(1 - slot))


def _expert_kernel(be_ref, nu_ref, nxt_ref, x_ref, wg_hbm, wu_hbm, wd_hbm, y_ref,
                   wgf_ref, wuf_ref, wdf_ref, wgb_ref, wub_ref, wdb_ref, sems, *, layer):
    i = pl.program_id(0)

    def weight_copies(e):
        return (pltpu.make_async_copy(wg_hbm.at[layer, e], wgf_ref, sems.at[0]),
                pltpu.make_async_copy(wu_hbm.at[layer, e], wuf_ref, sems.at[1]),
                pltpu.make_async_copy(wd_hbm.at[layer, e], wdf_ref, sems.at[2]))

    @pl.when(i < nu_ref[0])
    def _():
        e = be_ref[i]

        @pl.when(i == 0)
        def _():
            for c in weight_copies(e):
                c.start()

        @pl.when(jnp.logical_or(i == 0, e != be_ref[jnp.maximum(i - 1, 0)]))
        def _():
            for c in weight_copies(e):
                c.wait()
            wgb_ref[...] = wgf_ref[...].astype(BF16)
            wub_ref[...] = wuf_ref[...].astype(BF16)
            wdb_ref[...] = wdf_ref[...].astype(BF16)
            nxt = nxt_ref[i]

            @pl.when(nxt >= 0)
            def _():
                for c in weight_copies(nxt):
                    c.start()

        x = _unpack_bf16_pair(x_ref[...]).astype(BF16)
        gate = _dot(x, wgb_ref[...])
        up = _dot(x, wub_ref[...])
        hid = (gate * _sigmoid(gate) * up).astype(BF16)
        y_ref[...] = _pack_bf16_pair(_dot(hid, wdb_ref[...]))

    @pl.when(i >= nu_ref[0])
    def _():
        y_ref[...] = jnp.zeros(y_ref.shape, y_ref.dtype)


def _combine_kernel(chunk_of_ref, h_ref, crec_ref, rows_ref, o_ref, sorted_ref, sems):
    i, nt = pl.program_id(0), pl.num_programs(0)
    slot = i % 2
    tm = h_ref.shape[0]
    n_chunks = sorted_ref.shape[1] // MOE_CHUNK

    def copy_of(slot_):
        return lambda lr, gr: pltpu.make_async_copy(rows_ref.at[gr], sorted_ref.at[slot_, lr],
                                                    sems.at[slot_])

    @pl.when(i == 0)
    def _():
        _start_chunk_copies(chunk_of_ref, i, n_chunks, copy_of(slot))

    @pl.when(i + 1 < nt)
    def _():
        _start_chunk_copies(chunk_of_ref, i + 1, n_chunks, copy_of(1 - slot))

    _wait_chunk_copies(n_chunks, copy_of(slot))
    crec = crec_ref[...]
    loc0 = crec[:, _C_LOC1:_C_LOC1 + 1].astype(jnp.int32)
    loc1 = crec[:, _C_LOC2:_C_LOC2 + 1].astype(jnp.int32)
    w1 = crec[:, _C_W1:_C_W1 + 1]
    w2 = crec[:, _C_W2:_C_W2 + 1]
    local_row = lax.broadcasted_iota(jnp.int32, (tm, sorted_ref.shape[1]), 1)
    weights = jnp.where(local_row == loc0, w1, jnp.where(local_row == loc1, w2, 0.0)).astype(BF16)
    y = _unpack_bf16_pair(sorted_ref[slot]).astype(BF16)
    o_ref[...] = h_ref[...] + _dot(weights, y)


def _hier_moe(h, g_norm, w_group, b_group, w_expert, b_expert, layer, w_gate, w_up, w_down):
    t, d = h.shape
    ne, ng = MOE_EXPERTS, MOE_GROUPS
    ff = w_gate.shape[3]
    dp = d // 2
    tm = min(MOE_TOKEN_TILE, t)
    g2 = g_norm.reshape(1, d)
    lane_pad = V7X_LANES - ne - ng
    w_router = _split_weight(
        jnp.concatenate([w_expert, w_group, jnp.zeros((d, lane_pad), F32)], axis=1))
    b_router = jnp.concatenate([b_expert, b_group, jnp.zeros((lane_pad,), F32)]).reshape(1, V7X_LANES)
    nt = t // tm
    rec, base, cnt = pl.pallas_call(
        _router_kernel,
        grid=(nt,),
        in_specs=[pl.BlockSpec((tm, d), lambda i: (i, 0)),
                  _const_spec((1, d)),
                  _const_spec((d, 2 * V7X_LANES)),
                  _const_spec((1, V7X_LANES))],
        out_specs=[pl.BlockSpec((tm, V7X_LANES), lambda i: (i, 0)),
                   pl.BlockSpec((V7X_SUBLANES, V7X_LANES), lambda i: (i, 0)),
                   _const_spec((V7X_SUBLANES, V7X_LANES))],
        out_shape=[jax.ShapeDtypeStruct((t, V7X_LANES), F32),
                   jax.ShapeDtypeStruct((nt * V7X_SUBLANES, V7X_LANES), F32),
                   jax.ShapeDtypeStruct((V7X_SUBLANES, V7X_LANES), F32)],
        scratch_shapes=[pltpu.VMEM((V7X_SUBLANES, V7X_LANES), F32)],
        compiler_params=_cparams("arbitrary"),
        name="moe_router",
    )(h, g2, w_router, b_router)

    br, ck = MOE_ROW_BLOCK, MOE_CHUNK
    n_blocks = -(-(2 * t + nt * ne * (ck - 1) + ne * (br - 1)) // br)
    n_rows = n_blocks * br
    local_rows = 2 * tm + ne * ck
    before = base.reshape(nt, V7X_SUBLANES, V7X_LANES)[:, 0, :ne].astype(jnp.int32)
    total = cnt[0, :ne].astype(jnp.int32)
    tile_cnt = jnp.concatenate([before[1:], total[None, :]], axis=0) - before
    seg = (tile_cnt + ck - 1) // ck * ck
    region = jnp.sum(seg, axis=0)
    padded = (region + br - 1) // br * br
    pad_end = jnp.cumsum(padded)
    pad_start = pad_end - padded
    seg_global = pad_start[None, :] + jnp.cumsum(seg, axis=0) - seg
    seg_local = jnp.cumsum(seg, axis=1) - seg
    eid = rec[:, _R_E1:_R_E2 + 1].astype(jnp.int32)
    rank = rec[:, _R_RANK1:_R_RANK2 + 1].astype(jnp.int32)
    experts = jnp.arange(ne, dtype=jnp.int32)
    shift = jnp.repeat(seg_local - before, tm, axis=0)
    loc = rank + jnp.sum(jnp.where(eid[:, :, None] == experts, shift[:, None, :], 0), axis=-1)
    loct = jnp.concatenate([jnp.swapaxes(loc.reshape(nt, tm, 2), 1, 2),
                            jnp.full((nt, V7X_SUBLANES - 2, tm), -1, jnp.int32)], axis=1)
    crec = jnp.concatenate([loc.astype(F32), rec[:, _R_W1:_R_W2 + 1],
                            jnp.zeros((t, V7X_SUBLANES - 4), F32)], axis=1)
    n_local = local_rows // ck
    j = jnp.arange(n_local, dtype=jnp.int32)
    seg_end = (seg_local + seg) // ck
    owner = jnp.sum((seg_end[:, None, :] <= j[None, :, None]).astype(jnp.int32), axis=-1)
    offset = (seg_global - seg_local) // ck
    chunk_of = j[None, :] + jnp.sum(jnp.where(owner[:, :, None] == experts, offset[:, None, :], 0), axis=-1)
    in_use = owner < ne
    parity = (jnp.arange(nt, dtype=jnp.int32) % 2)[:, None]
    scatter_chunk = jnp.where(in_use, chunk_of, n_rows // ck + parity * n_local + j[None, :]).reshape(-1)
    gather_chunk = jnp.where(in_use, chunk_of, 0).reshape(-1)
    block_row0 = jnp.arange(n_blocks, dtype=jnp.int32) * br
    block_expert = jnp.minimum(
        jnp.sum((pad_end[None, :] <= block_row0[:, None]).astype(jnp.int32), axis=1), ne - 1)
    n_used = (pad_end[-1:] // br).astype(jnp.int32)
    blocks = jnp.arange(n_blocks, dtype=jnp.int32)
    later_run = ((block_expert[None, :] != block_expert[:, None]) & (blocks[None, :] > blocks[:, None])
                 & (blocks[None, :] < n_used))
    next_expert = jnp.where(jnp.any(later_run, axis=1),
                            block_expert[jnp.argmax(later_run, axis=1)], -1).astype(jnp.int32)

    x_rows = pl.pallas_call(
        _dispatch_kernel,
        grid_spec=pltpu.PrefetchScalarGridSpec(
            num_scalar_prefetch=1,
            grid=(nt,),
            in_specs=[pl.BlockSpec((tm, d), lambda i, *_: (i, 0)),
                      _const_spec((1, d)),
                      pl.BlockSpec((None, V7X_SUBLANES, tm), lambda i, *_: (i, 0, 0)),
                      pl.BlockSpec(memory_space=pl.ANY)],
            out_specs=pl.BlockSpec(memory_space=pl.ANY),
            scratch_shapes=[pltpu.VMEM((2, local_rows, dp), jnp.uint32), pltpu.SemaphoreType.DMA((2,))]),
        out_shape=jax.ShapeDtypeStruct((n_rows + 2 * local_rows, dp), jnp.uint32),
        input_output_aliases={4: 0},
        compiler_params=_cparams("arbitrary"),
        name="moe_dispatch",
    )(scatter_chunk, h, g2, loct, jnp.zeros((n_rows + 2 * local_rows, dp), jnp.uint32))

    def blk(i, be, nu, nxt):
        return jnp.maximum(jnp.minimum(i, nu[0] - 1), 0)

    hbm = pl.BlockSpec(memory_space=pl.ANY)
    y_rows = pl.pallas_call(
        functools.partial(_expert_kernel, layer=layer),
        grid_spec=pltpu.PrefetchScalarGridSpec(
            num_scalar_prefetch=3,
            grid=(n_blocks,),
            in_specs=[pl.BlockSpec((br, dp), lambda i, be, nu, nxt: (blk(i, be, nu, nxt), 0)),
                      hbm, hbm, hbm],
            out_specs=pl.BlockSpec((br, dp), lambda i, be, nu, nxt: (i, 0)),
            scratch_shapes=[pltpu.VMEM((d, ff), F32), pltpu.VMEM((d, ff), F32), pltpu.VMEM((ff, d), F32),
                            pltpu.VMEM((d, ff), BF16), pltpu.VMEM((d, ff), BF16), pltpu.VMEM((ff, d), BF16),
                            pltpu.SemaphoreType.DMA((3,))]),
        out_shape=jax.ShapeDtypeStruct((n_rows, dp), jnp.uint32),
        compiler_params=_cparams("arbitrary"),
        name="moe_experts",
    )(block_expert, n_used, next_expert, x_rows, w_gate, w_up, w_down)

    return pl.pallas_call(
        _combine_kernel,
        grid_spec=pltpu.PrefetchScalarGridSpec(
            num_scalar_prefetch=1,
            grid=(nt,),
            in_specs=[pl.BlockSpec((tm, d), lambda i, *_: (i, 0)),
                      pl.BlockSpec((tm, V7X_SUBLANES), lambda i, *_: (i, 0)),
                      pl.BlockSpec(memory_space=pl.ANY)],
            out_specs=pl.BlockSpec((tm, d), lambda i, *_: (i, 0)),
            scratch_shapes=[pltpu.VMEM((2, local_rows, dp), jnp.uint32), pltpu.SemaphoreType.DMA((2,))]),
        out_shape=jax.ShapeDtypeStruct((t, d), F32),
        compiler_params=_cparams("arbitrary"),
        name="moe_combine",
    )(gather_chunk, h, crec, y_rows)


def _rope_tables(positions):
    half = DIFF_HEAD_DIM // 2
    inv = ROPE_THETA ** (-jnp.arange(0, DIFF_HEAD_DIM, 2, dtype=F32) / DIFF_HEAD_DIM)
    ang = positions.astype(F32).reshape(-1, 1) * inv
    cos, sin = jnp.cos(ang), jnp.sin(ang)
    cos_t = jnp.concatenate([cos] * 4, axis=1)
    sin_t = jnp.concatenate([-sin, sin, -sin, sin], axis=1)
    assert cos_t.shape[1] == 4 * half
    return cos_t, sin_t


def kernel(x, positions, norm_mix, norm_ffn, ssd_w_in, ssd_conv_w, ssd_conv_b, ssd_dt_bias, ssd_a_log, ssd_d, ssd_norm, ssd_w_out, sb_w_in, sb_w_out, lru_w_in, lru_conv_w, lru_conv_b, lru_w_rg, lru_b_rg, lru_w_ig, lru_b_ig, lru_lambda, lru_w_out, diff_w_in, diff_q_norm, diff_k_norm, diff_lam_q1, diff_lam_k1, diff_lam_q2, diff_lam_k2, diff_sub_norm, diff_w_out, moe_w_group, moe_b_group, moe_w_expert, moe_b_expert, moe_w_gate, moe_w_up, moe_w_down):
    bsz, seqlen, d = x.shape
    depth = norm_mix.shape[0]
    h = x.reshape(bsz * seqlen, d)
    cos_t, sin_t = _rope_tables(positions)
    for i in range(depth):
        m, j = i % N_MIXERS, i // N_MIXERS
        if m == 0:
            h = _ssd_mixer(h, bsz, seqlen, norm_mix[i], ssd_w_in[j], ssd_conv_w[j], ssd_conv_b[j],
                           ssd_dt_bias[j], ssd_a_log[j], ssd_d[j], ssd_norm[j], ssd_w_out[j])
        elif m == 1:
            h = _sb_mixer(h, bsz, seqlen, norm_mix[i], sb_w_in[j], sb_w_out[j])
        elif m == 2:
            h = _lru_mixer(h, bsz, seqlen, norm_mix[i], lru_w_in[j], lru_conv_w[j], lru_conv_b[j],
                           lru_w_rg[j], lru_b_rg[j], lru_w_ig[j], lru_b_ig[j], lru_lambda[j],
                           lru_w_out[j])
        else:
            lambda_init = 0.8 - 0.6 * math.exp(-0.3 * i)
            h = _diff_mixer(h, bsz, seqlen, cos_t, sin_t, norm_mix[i], diff_w_in[j], diff_q_norm[j],
                            diff_k_norm[j], diff_lam_q1[j], diff_lam_k1[j], diff_lam_q2[j],
                            diff_lam_k2[j], diff_sub_norm[j], diff_w_out[j], lambda_init)
        h = _hier_moe(h, norm_ffn[i], moe_w_group[i], moe_b_group[i], moe_w_expert[i],
                      moe_b_expert[i], i, moe_w_gate, moe_w_up, moe_w_down)
    return h.reshape(bsz, seqlen, d)
```

```python
import functools
import math

import jax
import jax.numpy as jnp
from jax import lax
from jax.experimental import pallas as pl
from jax.experimental.pallas import tpu as pltpu

F32 = jnp.float32
BF16 = jnp.bfloat16
HIGHEST = lax.Precision.HIGHEST

RMS_EPS = 1e-6
_LOG2_E = math.log2(math.e)
N_MIXERS = 4

V7X_VMEM_BYTES = 64 * 1024 * 1024
V7X_LANES = 128
V7X_SUBLANES = 8
VMEM_LIMIT = (V7X_VMEM_BYTES * 7) // 8

SSD_HEAD_DIM = 64
SSD_GROUPS = 8
SSD_STATE = 128
SSD_CONV = 4
SSD_CHUNK = 128
SB_HEAD_DIM = 64
LRU_BLOCKS = 16
LRU_CONV = 4
LRU_C = 8.0
DIFF_HEAD_DIM = 64
ROPE_THETA = 10000.0
MOE_GROUPS = 4
MOE_EXPERTS_PER_GROUP = 8
MOE_EXPERTS = MOE_GROUPS * MOE_EXPERTS_PER_GROUP

ROW_TILE = 512
SSD_IN_TILE = 512
SSD_CONV_CHUNK = 512
ATTN_TILE = 256
SB_PAIRS_PER_STEP = 8
DIFF_HEADS_PER_STEP = 8
LRU_TIME_TILE = 64
LRU_PITCH_PAD = 4
LRU_BAND = 3 * V7X_LANES
LRU_SCAN_UNROLL = 4
MOE_TOKEN_TILE = 512
MOE_ROW_BLOCK = 512
MOE_CHUNK = V7X_SUBLANES
CONV_HALO = V7X_SUBLANES
DMA_LOOP_UNROLL = 8


def _cparams(*sem):
    return pltpu.CompilerParams(dimension_semantics=sem, vmem_limit_bytes=VMEM_LIMIT)


def _const_spec(shape):
    nd = len(shape)
    return pl.BlockSpec(shape, lambda *_: (0,) * nd)


def _resident_spec(shape):
    nd = len(shape)
    return pl.BlockSpec(shape, lambda *_: (0,) * nd, pipeline_mode=pl.Buffered(1))


def _rms(x, w):
    ms = jnp.mean(x * x, axis=-1, keepdims=True)
    return x * lax.rsqrt(ms + RMS_EPS) * w


def _sigmoid(x):
    return 1.0 / (1.0 + jnp.exp(-x))


def _softplus(x):
    return jnp.maximum(x, 0.0) + jnp.log(1.0 + jnp.exp(-jnp.abs(x)))


def _dot(a, b, **kw):
    return jnp.dot(a, b, preferred_element_type=F32, **kw)


def _dot_nt(a, b):
    return lax.dot_general(a, b, (((1,), (1,)), ((), ())), preferred_element_type=F32)


def _dot_tn(a, b):
    return lax.dot_general(a, b, (((0,), (0,)), ((), ())), preferred_element_type=F32)


def _split_bf16(x):
    hi = x.astype(BF16)
    return hi, (x - hi.astype(F32)).astype(BF16)


def _split_weight(w):
    return jnp.concatenate(_split_bf16(w), axis=1)


def _dot_split(x, x_hi, w2_ref):
    n = w2_ref.shape[1] // 2
    x_lo = (x - x_hi.astype(F32)).astype(BF16)
    t = _dot(x_hi, w2_ref[...])
    return t[:, :n] + t[:, n:] + _dot(x_lo, w2_ref[:, :n])


def _proj_residual_kernel(y_ref, w_ref, h_ref, o_ref):
    o_ref[...] = h_ref[...] + _dot(y_ref[...], w_ref[...])


def _proj_residual(y, w, h):
    t, k = y.shape
    d = w.shape[1]
    tm = ROW_TILE
    return pl.pallas_call(
        _proj_residual_kernel,
        grid=(t // tm,),
        in_specs=[pl.BlockSpec((tm, k), lambda i: (i, 0)),
                  _const_spec((k, d)),
                  pl.BlockSpec((tm, d), lambda i: (i, 0))],
        out_specs=pl.BlockSpec((tm, d), lambda i: (i, 0)),
        out_shape=jax.ShapeDtypeStruct((t, d), F32),
        compiler_params=_cparams("arbitrary"),
        name="proj_residual",
    )(y, w, h)


def _ssd_in_kernel(h_ref, g_ref, wz_ref, wx_ref, wdt_ref, cw_ref, cb_ref,
                   z_ref, xbc_ref, dt_ref, pad_ref, *, tiles_per_seq):
    i = pl.program_id(0)
    tm = h_ref.shape[0]
    u32 = _rms(h_ref[...], g_ref[...])
    u = u32.astype(BF16)
    z_ref[...] = _dot(u, wz_ref[...]).astype(z_ref.dtype)
    dt_ref[...] = _dot_split(u32, u, wdt_ref)

    @pl.when(i % tiles_per_seq == 0)
    def _():
        pad_ref[0:CONV_HALO, :] = jnp.zeros((CONV_HALO, pad_ref.shape[1]), F32)

    conv_dim = pad_ref.shape[1]
    for c0 in range(0, conv_dim, SSD_CONV_CHUNK):
        cs = slice(c0, c0 + SSD_CONV_CHUNK)
        pad_ref[CONV_HALO:CONV_HALO + tm, cs] = _dot(u, wx_ref[:, cs])
        acc = cb_ref[:, cs] + cw_ref[SSD_CONV - 1:SSD_CONV, cs] * pad_ref[CONV_HALO:CONV_HALO + tm, cs]
        for k in range(SSD_CONV - 1):
            off = CONV_HALO - (SSD_CONV - 1) + k
            acc = acc + cw_ref[k:k + 1, cs] * pad_ref[off:off + tm, cs]
        xbc_ref[:, cs] = (acc * _sigmoid(acc)).astype(xbc_ref.dtype)
        pad_ref[0:CONV_HALO, cs] = pad_ref[tm:tm + CONV_HALO, cs]


def _ssd_scan_kernel(xs_ref, b_ref, c_ref, z_ref, dt_ref, dtt_ref,
                     bias_ref, biast_ref, alog_ref, alogt_ref, dfull_ref, nw_ref,
                     o_ref, state_ref):
    q = xs_ref.shape[0]
    n_heads = dt_ref.shape[1]
    d_inner = xs_ref.shape[1]
    heads_per_group = n_heads // SSD_GROUPS
    gw = d_inner // SSD_GROUPS
    p = SSD_HEAD_DIM

    @pl.when(pl.program_id(1) == 0)
    def _():
        state_ref[...] = jnp.zeros(state_ref.shape, F32)

    dt = _softplus(dt_ref[...] + bias_ref[...])
    a = -jnp.exp(alog_ref[...]) * dt
    dtt = _softplus(dtt_ref[...] + biast_ref[...])
    at = -jnp.exp(alogt_ref[...]) * dtt

    row = lax.broadcasted_iota(jnp.int32, (q, q), 0)
    col = lax.broadcasted_iota(jnp.int32, (q, q), 1)
    lower = row >= col
    a_cum = _dot(lower.astype(F32), a, precision=HIGHEST)
    a_cumt = _dot(at, (row <= col).astype(F32), precision=HIGHEST)
    a_tot = a_cum[q - 1:q, :]

    hh = lax.broadcasted_iota(jnp.int32, (2 * n_heads, d_inner), 0)
    cc = lax.broadcasted_iota(jnp.int32, (2 * n_heads, d_inner), 1)
    expand = (cc // p == hh % n_heads).astype(BF16)
    per_head = jnp.concatenate(
        [dt, jnp.exp(a_cum), jnp.exp(a_tot - a_cum),
         jnp.broadcast_to(jnp.exp(a_tot), (V7X_SUBLANES, n_heads))], axis=0)
    per_channel = _dot(jnp.concatenate(_split_bf16(per_head), axis=1), expand)
    dt_full = per_channel[0:q]
    dec_out_full = per_channel[q:2 * q]
    dec_st_full = per_channel[2 * q:3 * q]
    chunk_dec_full = per_channel[3 * q:3 * q + 1]

    lane = lax.broadcasted_iota(jnp.int32, (1, gw), 1)
    for g in range(SSD_GROUPS):
        sl = slice(g * gw, (g + 1) * gw)
        bg = b_ref[:, g * SSD_STATE:(g + 1) * SSD_STATE]
        cg = c_ref[:, g * SSD_STATE:(g + 1) * SSD_STATE]
        xs_g = xs_ref[:, sl].astype(F32)
        xdt = xs_g * dt_full[:, sl]
        cb = _dot_nt(cg, bg)
        s_prev = state_ref[g]
        y = _dot(cg, s_prev.astype(BF16)) * dec_out_full[:, sl]
        for r in range(heads_per_group):
            hd = g * heads_per_group + r
            seg = a_cum[:, hd:hd + 1] - a_cumt[hd:hd + 1, :]
            m = jnp.where(lower, cb * jnp.exp(seg), 0.0).astype(BF16)
            xr = jnp.where((lane >= r * p) & (lane < (r + 1) * p), xdt, 0.0).astype(BF16)
            y = y + _dot(m, xr)
        xw = (xdt * dec_st_full[:, sl]).astype(BF16)
        state_ref[g] = s_prev * chunk_dec_full[:, sl] + _dot_tn(bg, xw)
        y = y + xs_g * dfull_ref[:, sl]
        zg = z_ref[:, sl].astype(F32)
        yz = y * (zg * _sigmoid(zg))
        o_ref[:, sl] = _rms(yz, nw_ref[:, sl]).astype(o_ref.dtype)


def _ssd_mixer(h, bsz, seqlen, g_norm, w_in, conv_w, conv_b, dt_bias, a_log, d_skip, norm_w, w_out):
    t, d = h.shape
    n_heads = dt_bias.shape[0]
    d_inner = n_heads * SSD_HEAD_DIM
    gn = SSD_GROUPS * SSD_STATE
    conv_dim = d_inner + 2 * gn
    wz = w_in[:, :d_inner].astype(BF16)
    wx = w_in[:, d_inner:d_inner + conv_dim].astype(BF16)
    wdt = _split_weight(w_in[:, d_inner + conv_dim:])
    tm = min(SSD_IN_TILE, seqlen)
    assert seqlen % tm == 0, "a row tile must not straddle two sequences (conv history)"
    z, xbc, dt_raw = pl.pallas_call(
        functools.partial(_ssd_in_kernel, tiles_per_seq=seqlen // tm),
        grid=(t // tm,),
        in_specs=[pl.BlockSpec((tm, d), lambda i: (i, 0)),
                  _const_spec((1, d)),
                  _resident_spec((d, d_inner)),
                  _resident_spec((d, conv_dim)),
                  _const_spec((d, 2 * n_heads)),
                  _const_spec((SSD_CONV, conv_dim)),
                  _const_spec((1, conv_dim))],
        out_specs=[pl.BlockSpec((tm, d_inner), lambda i: (i, 0)),
                   pl.BlockSpec((tm, conv_dim), lambda i: (i, 0)),
                   pl.BlockSpec((tm, n_heads), lambda i: (i, 0))],
        out_shape=[jax.ShapeDtypeStruct((t, d_inner), BF16),
                   jax.ShapeDtypeStruct((t, conv_dim), BF16),
                   jax.ShapeDtypeStruct((t, n_heads), F32)],
        scratch_shapes=[pltpu.VMEM((tm + CONV_HALO, conv_dim), F32)],
        compiler_params=_cparams("arbitrary"),
        name="ssd_in",
    )(h, g_norm.reshape(1, d), wz, wx, wdt, conv_w, conv_b.reshape(1, conv_dim))

    q = SSD_CHUNK
    nc = seqlen // q
    xbc3 = xbc.reshape(bsz, seqlen, conv_dim)
    dt3 = dt_raw.reshape(bsz, seqlen, n_heads)
    dtt3 = jnp.swapaxes(dt3, 1, 2)
    nb_x = d_inner // gn
    yg = pl.pallas_call(
        _ssd_scan_kernel,
        grid=(bsz, nc),
        in_specs=[pl.BlockSpec((None, q, d_inner), lambda b, c: (b, c, 0)),
                  pl.BlockSpec((None, q, gn), lambda b, c: (b, c, nb_x)),
                  pl.BlockSpec((None, q, gn), lambda b, c: (b, c, nb_x + 1)),
                  pl.BlockSpec((None, q, d_inner), lambda b, c: (b, c, 0)),
                  pl.BlockSpec((None, q, n_heads), lambda b, c: (b, c, 0)),
                  pl.BlockSpec((None, n_heads, q), lambda b, c: (b, 0, c)),
                  _const_spec((1, n_heads)), _const_spec((n_heads, 1)),
                  _const_spec((1, n_heads)), _const_spec((n_heads, 1)),
                  _const_spec((1, d_inner)), _const_spec((1, d_inner))],
        out_specs=pl.BlockSpec((None, q, d_inner), lambda b, c: (b, c, 0)),
        out_shape=jax.ShapeDtypeStruct((bsz, seqlen, d_inner), BF16),
        scratch_shapes=[pltpu.VMEM((SSD_GROUPS, SSD_STATE, d_inner // SSD_GROUPS), F32)],
        compiler_params=_cparams("arbitrary", "arbitrary"),
        name="ssd_scan",
    )(xbc3, xbc3, xbc3, z.reshape(bsz, seqlen, d_inner), dt3, dtt3,
      dt_bias.reshape(1, n_heads), dt_bias.reshape(n_heads, 1),
      a_log.reshape(1, n_heads), a_log.reshape(n_heads, 1),
      jnp.repeat(d_skip, SSD_HEAD_DIM).reshape(1, d_inner), norm_w.reshape(1, d_inner))
    return _proj_residual(yg.reshape(t, d_inner), w_out.astype(BF16), h)


def _sb_in_kernel(h_ref, g_ref, w_ref, o_ref, *, q_cols, q_scale):
    u = _rms(h_ref[...], g_ref[...]).astype(BF16)
    o_ref[:, :q_cols] = (_dot(u, w_ref[:, :q_cols]) * q_scale).astype(o_ref.dtype)
    o_ref[:, q_cols:] = _dot(u, w_ref[:, q_cols:]).astype(o_ref.dtype)


def _sb_attn_kernel(q_ref, k_ref, v_ref, o_ref, acc_ref, *, tile):
    seqlen = q_ref.shape[0]
    nq = seqlen // tile
    pw = 2 * SB_HEAD_DIM
    n_pairs = q_ref.shape[1] // pw
    lane = lax.broadcasted_iota(jnp.int32, (1, pw), 1)
    first = lane < SB_HEAD_DIM
    row = lax.broadcasted_iota(jnp.int32, (tile, tile), 0)
    col = lax.broadcasted_iota(jnp.int32, (tile, tile), 1)
    neg_from = jnp.where(row >= col, -1.0, 0.0).astype(BF16)
    past = col < row
    past2 = jnp.concatenate([past, past], axis=0)

    def k_tile(qs, kj, csums, diagonal):
        rows = pl.ds(pl.multiple_of(kj * tile, tile), tile)
        out = []
        for p in range(n_pairs):
            k = k_ref[rows, p * pw:(p + 1) * pw]
            v = v_ref[rows, p * pw:(p + 1) * pw]
            t = _dot_nt(qs[p], k)
            neg_abs = pltpu.bitcast(pltpu.bitcast(t, jnp.uint32) | jnp.uint32(0x80000000), F32)
            sp = jnp.maximum(t, 0.0) + jnp.log(1.0 + jnp.exp2(neg_abs)) * _LOG2_E
            if diagonal:
                sp = jnp.where(past2, sp, 0.0)
            w = jnp.exp2(t + _dot(sp.astype(BF16), neg_from) + csums[p])
            if diagonal:
                w = jnp.where(past2, w, 0.0)
            acc_ref[p] += _dot(w.astype(BF16), v)
            out.append(csums[p] - jnp.sum(sp, axis=1, keepdims=True))
        return tuple(out)

    def q_body(qi, _):
        qrows = pl.ds(pl.multiple_of(qi * tile, tile), tile)
        qs = []
        for p in range(n_pairs):
            q = q_ref[qrows, p * pw:(p + 1) * pw]
            zero = jnp.zeros_like(q)
            qs.append(jnp.concatenate([jnp.where(first, q, zero), jnp.where(first, zero, q)], axis=0))
            acc_ref[p] = jnp.zeros(acc_ref.shape[1:], F32)
        init = tuple(jnp.zeros((2 * tile, 1), F32) for _ in range(n_pairs))
        csums = k_tile(qs, qi, init, True)
        lax.fori_loop(0, qi, lambda s, c: k_tile(qs, qi - 1 - s, c, False), csums)
        for p in range(n_pairs):
            o = jnp.where(first, acc_ref[p, 0:tile, :], acc_ref[p, tile:2 * tile, :])
            o_ref[qrows, p * pw:(p + 1) * pw] = o.astype(o_ref.dtype)
        return 0

    lax.fori_loop(0, nq, q_body, 0)


def _sb_mixer(h, bsz, seqlen, g_norm, w_in, w_out):
    t, d = h.shape
    hd = w_in.shape[1] // 3
    tm = ROW_TILE
    qkv = pl.pallas_call(
        functools.partial(_sb_in_kernel, q_cols=hd, q_scale=SB_HEAD_DIM ** -0.5 * _LOG2_E),
        grid=(t // tm,),
        in_specs=[pl.BlockSpec((tm, d), lambda i: (i, 0)),
                  _const_spec((1, d)),
                  _const_spec((d, 3 * hd))],
        out_specs=pl.BlockSpec((tm, 3 * hd), lambda i: (i, 0)),
        out_shape=jax.ShapeDtypeStruct((t, 3 * hd), BF16),
        compiler_params=_cparams("arbitrary"),
        name="sb_in",
    )(h, g_norm.reshape(1, d), w_in.astype(BF16))
    qkv3 = qkv.reshape(bsz, seqlen, 3 * hd)
    bw = SB_PAIRS_PER_STEP * 2 * SB_HEAD_DIM
    nb = hd // bw
    tile = min(ATTN_TILE, seqlen)
    o = pl.pallas_call(
        functools.partial(_sb_attn_kernel, tile=tile),
        grid=(bsz, nb),
        in_specs=[pl.BlockSpec((None, seqlen, bw), lambda b, j: (b, 0, j)),
                  pl.BlockSpec((None, seqlen, bw), lambda b, j: (b, 0, nb + j)),
                  pl.BlockSpec((None, seqlen, bw), lambda b, j: (b, 0, 2 * nb + j))],
        out_specs=pl.BlockSpec((None, seqlen, bw), lambda b, j: (b, 0, j)),
        out_shape=jax.ShapeDtypeStruct((bsz, seqlen, hd), BF16),
        scratch_shapes=[pltpu.VMEM((SB_PAIRS_PER_STEP, 2 * tile, 2 * SB_HEAD_DIM), F32)],
        compiler_params=_cparams("arbitrary", "arbitrary"),
        name="sb_attn",
    )(qkv3, qkv3, qkv3)
    return _proj_residual(o.reshape(t, hd), w_out.astype(BF16), h)


def _lru_kernel(h_ref, g_ref, win_ref, cw_ref, cb_ref, wband_ref, br_ref, bi_ref, lam_ref,
                y_ref, pad_ref, xc_ref, a_ref, b_ref, hs_ref, carry_ref, *, band_starts):
    bsz, tt, d = h_ref.shape
    width = y_ref.shape[2]

    @pl.when(pl.program_id(0) == 0)
    def _():
        carry_ref[...] = jnp.zeros(carry_ref.shape, F32)
        for b in range(bsz):
            pad_ref[b, 0:CONV_HALO, :] = jnp.zeros((CONV_HALO, width), F32)

    u = _rms(h_ref[...].reshape(bsz * tt, d), g_ref[...]).astype(BF16)
    proj = _dot(u, win_ref[...])
    gate_branch = proj[:, :width]
    for b in range(bsz):
        pad_ref[b, CONV_HALO:CONV_HALO + tt, :] = proj[b * tt:(b + 1) * tt, width:]
        acc = cb_ref[...] + cw_ref[LRU_CONV - 1:LRU_CONV, :] * pad_ref[b, CONV_HALO:CONV_HALO + tt, :]
        for k in range(LRU_CONV - 1):
            off = CONV_HALO - (LRU_CONV - 1) + k
            acc = acc + cw_ref[k:k + 1, :] * pad_ref[b, off:off + tt, :]
        xc_ref[b * tt:(b + 1) * tt, :] = acc
        pad_ref[b, 0:CONV_HALO, :] = pad_ref[b, tt:tt + CONV_HALO, :]

    xcb = xc_ref[...].astype(BF16)
    n_lt = width // V7X_LANES
    pitch = a_ref.shape[1] // bsz
    neg_c_softplus = -LRU_C * _softplus(-lam_ref[...])
    for c in range(n_lt):
        cs = slice(c * V7X_LANES, (c + 1) * V7X_LANES)
        pre = _dot(xcb[:, band_starts[c]:band_starts[c] + LRU_BAND], wband_ref[c])
        r = _sigmoid(pre[:, :V7X_LANES] + br_ref[:, cs])
        ig = _sigmoid(pre[:, V7X_LANES:] + bi_ref[:, cs])
        a = jnp.exp(neg_c_softplus[:, cs] * r)
        bterm = jnp.sqrt(1.0 - a * a) * (ig * xc_ref[:, cs])
        for b in range(bsz):
            a_ref[c, b * pitch:b * pitch + tt, :] = a[b * tt:(b + 1) * tt, :]
            b_ref[c, b * pitch:b * pitch + tt, :] = bterm[b * tt:(b + 1) * tt, :]

    def step(j, hprev):
        hnew = []
        for c in range(n_lt):
            rows_j = pl.ds(j, bsz, stride=pitch)
            hc = a_ref[c, rows_j, :] * hprev[c] + b_ref[c, rows_j, :]
            hs_ref[c, rows_j, :] = hc
            hnew.append(hc)
        return tuple(hnew)

    hlast = lax.fori_loop(0, tt, step, tuple(carry_ref[c] for c in range(n_lt)), unroll=LRU_SCAN_UNROLL)
    for c in range(n_lt):
        carry_ref[c] = hlast[c]
    c0 = math.sqrt(2.0 / math.pi)
    gelu = 0.5 * gate_branch * (1.0 + jnp.tanh(c0 * (gate_branch + 0.044715 * gate_branch ** 3)))
    for b in range(bsz):
        hs_b = jnp.concatenate([hs_ref[c, b * pitch:b * pitch + tt, :] for c in range(n_lt)], axis=1)
        y_ref[b] = (hs_b * gelu[b * tt:(b + 1) * tt, :]).astype(y_ref.dtype)


def _lru_band_starts(width):
    bd = width // LRU_BLOCKS
    starts = []
    for j in range(width // V7X_LANES):
        b_lo = (j * V7X_LANES) // bd
        b_hi = ((j + 1) * V7X_LANES - 1) // bd
        lo = (b_lo * bd) // V7X_LANES * V7X_LANES
        assert (b_hi + 1) * bd - lo <= LRU_BAND
        starts.append(min(lo, width - LRU_BAND))
    return tuple(starts)


def _lru_band_weights(w_rg, w_ig, starts):
    def dense(w):
        nb, bd, _ = w.shape
        eye = jnp.eye(nb, dtype=w.dtype)
        return (eye[:, None, :, None] * w[:, :, None, :]).reshape(nb * bd, nb * bd)

    dense_r, dense_i = dense(w_rg), dense(w_ig)
    tiles = []
    for j, s in enumerate(starts):
        cs = slice(j * V7X_LANES, (j + 1) * V7X_LANES)
        tiles.append(jnp.concatenate([dense_r[s:s + LRU_BAND, cs], dense_i[s:s + LRU_BAND, cs]], axis=1))
    return jnp.stack(tiles).astype(BF16)


def _lru_mixer(h, bsz, seqlen, g_norm, w_in, conv_w, conv_b, w_rg, b_rg, w_ig, b_ig, lam, w_out):
    t, d = h.shape
    width = lam.shape[0]
    tt = min(LRU_TIME_TILE, seqlen)
    rows = bsz * tt
    n_lt = width // V7X_LANES
    scan_rows = bsz * (tt + LRU_PITCH_PAD)
    starts = _lru_band_starts(width)
    y = pl.pallas_call(
        functools.partial(_lru_kernel, band_starts=starts),
        grid=(seqlen // tt,),
        in_specs=[pl.BlockSpec((bsz, tt, d), lambda i: (0, i, 0)),
                  _const_spec((1, d)),
                  _const_spec((d, 2 * width)),
                  _const_spec((LRU_CONV, width)), _const_spec((1, width)),
                  _const_spec((n_lt, LRU_BAND, 2 * V7X_LANES)),
                  _const_spec((1, width)), _const_spec((1, width)),
                  _const_spec((1, width))],
        out_specs=pl.BlockSpec((bsz, tt, width), lambda i: (0, i, 0)),
        out_shape=jax.ShapeDtypeStruct((bsz, seqlen, width), BF16),
        scratch_shapes=[pltpu.VMEM((bsz, tt + CONV_HALO, width), F32),
                        pltpu.VMEM((rows, width), F32),
                        pltpu.VMEM((n_lt, scan_rows, V7X_LANES), F32),
                        pltpu.VMEM((n_lt, scan_rows, V7X_LANES), F32),
                        pltpu.VMEM((n_lt, scan_rows, V7X_LANES), F32),
                        pltpu.VMEM((n_lt, bsz, V7X_LANES), F32)],
        compiler_params=_cparams("arbitrary"),
        name="lru",
    )(h.reshape(bsz, seqlen, d), g_norm.reshape(1, d), w_in.astype(BF16),
      conv_w, conv_b.reshape(1, width), _lru_band_weights(w_rg, w_ig, starts),
      b_rg.reshape(1, width), b_ig.reshape(1, width), lam.reshape(1, width))
    return _proj_residual(y.reshape(t, width), w_out.astype(BF16), h)


def _diff_in_kernel(h_ref, g_ref, w_ref, cos_ref, sin_ref, qn_ref, kn_ref, q_ref, k_ref, v_ref,
                    *, q_scale):
    d_q = q_ref.shape[1]
    hdim = DIFF_HEAD_DIM
    n_grp = d_q // hdim
    u = _rms(h_ref[...], g_ref[...]).astype(BF16)
    v_ref[...] = _dot(u, w_ref[:, 2 * d_q:]).astype(v_ref.dtype)

    ci = lax.broadcasted_iota(jnp.int32, (d_q, n_grp), 0)
    gi = lax.broadcasted_iota(jnp.int32, (d_q, n_grp), 1)
    gsum = (ci // hdim == gi).astype(BF16)
    gj = lax.broadcasted_iota(jnp.int32, (2 * n_grp, d_q), 0)
    cj = lax.broadcasted_iota(jnp.int32, (2 * n_grp, d_q), 1)
    gexp = (cj // hdim == gj % n_grp).astype(BF16)
    lane = lax.broadcasted_iota(jnp.int32, (1, d_q), 1)
    first_half = (lane % hdim) < (hdim // 2)
    reps = d_q // cos_ref.shape[1]
    cos_t = jnp.tile(cos_ref[...], (1, reps))
    sin_t = jnp.tile(sin_ref[...], (1, reps))

    def norm_rope(x, nw):
        sq = x * x
        sq_hi = sq.astype(BF16)
        sq_lo = (sq - sq_hi.astype(F32)).astype(BF16)
        ms = (_dot(sq_hi, gsum) + _dot(sq_lo, gsum)) * (1.0 / hdim)
        inv = lax.rsqrt(ms + RMS_EPS)
        inv_hi = inv.astype(BF16)
        inv_lo = (inv - inv_hi.astype(F32)).astype(BF16)
        inv_full = _dot(jnp.concatenate([inv_hi, inv_lo], axis=1), gexp)
        xn = x * inv_full * nw
        rot = jnp.where(first_half, pltpu.roll(xn, d_q - hdim // 2, 1), pltpu.roll(xn, hdim // 2, 1))
        return xn * cos_t + rot * sin_t

    q_ref[...] = (norm_rope(_dot(u, w_ref[:, :d_q]), qn_ref[...]) * q_scale).astype(q_ref.dtype)
    k_ref[...] = norm_rope(_dot(u, w_ref[:, d_q:2 * d_q]), kn_ref[...]).astype(k_ref.dtype)


def _diff_attn_kernel(q_ref, k_ref, v_ref, lq1_ref, lk1_ref, lq2_ref, lk2_ref, sn_ref, o_ref,
                      acc_ref, *, tile, lambda_init):
    seqlen = q_ref.shape[0]
    nq = seqlen // tile
    hdim = DIFF_HEAD_DIM
    hw = 2 * hdim
    n_heads = q_ref.shape[1] // hw
    lane = lax.broadcasted_iota(jnp.int32, (1, hw), 1)
    first = lane < hdim
    row = lax.broadcasted_iota(jnp.int32, (tile, tile), 0)
    col = lax.broadcasted_iota(jnp.int32, (tile, tile), 1)
    causal = col <= row
    causal2 = jnp.concatenate([causal, causal], axis=0)
    ones = jnp.ones((tile, hw), BF16)
    lam = (jnp.exp(jnp.sum(lq1_ref[...] * lk1_ref[...], axis=1, keepdims=True))
           - jnp.exp(jnp.sum(lq2_ref[...] * lk2_ref[...], axis=1, keepdims=True)) + lambda_init)

    def k_tile(qs, kj, maxes, diagonal):
        rows = pl.ds(pl.multiple_of(kj * tile, tile), tile)
        out = []
        for hd in range(n_heads):
            k = k_ref[rows, hd * hw:(hd + 1) * hw]
            v1 = jnp.concatenate([v_ref[rows, hd * hw:(hd + 1) * hw], ones], axis=1)
            s = _dot_nt(qs[hd], k)
            if diagonal:
                s = jnp.where(causal2, s, -jnp.inf)
            mx_new = jnp.maximum(maxes[hd], jnp.max(s, axis=1, keepdims=True))
            alpha = jnp.exp2(maxes[hd] - mx_new)
            pr = jnp.exp2(s - mx_new)
            acc_ref[hd] = acc_ref[hd] * alpha + _dot(pr.astype(BF16), v1)
            out.append(mx_new)
        return tuple(out)

    def q_body(qi, _):
        qrows = pl.ds(pl.multiple_of(qi * tile, tile), tile)
        qs = []
        for hd in range(n_heads):
            q = q_ref[qrows, hd * hw:(hd + 1) * hw]
            zero = jnp.zeros_like(q)
            qs.append(jnp.concatenate([jnp.where(first, q, zero), jnp.where(first, zero, q)], axis=0))
            acc_ref[hd] = jnp.zeros(acc_ref.shape[1:], F32)
        init = tuple(jnp.full((2 * tile, 1), -jnp.inf, F32) for _ in range(n_heads))
        maxes = k_tile(qs, qi, init, True)
        lax.fori_loop(0, qi, lambda s, c: k_tile(qs, qi - 1 - s, c, False), maxes)
        for hd in range(n_heads):
            p = acc_ref[hd, :, 0:hw] / acc_ref[hd, :, hw:2 * hw]
            o = p[0:tile] - lam * p[tile:2 * tile]
            o = _rms(o, sn_ref[...]) * (1.0 - lambda_init)
            o_ref[qrows, hd * hw:(hd + 1) * hw] = o.astype(o_ref.dtype)
        return 0

    lax.fori_loop(0, nq, q_body, 0)


def _diff_mixer(h, bsz, seqlen, cos_t, sin_t, g_norm, w_in, q_norm, k_norm, lq1, lk1, lq2, lk2,
                sub_norm, w_out, lambda_init):
    t, d = h.shape
    d_q = w_in.shape[1] // 3
    hdim = DIFF_HEAD_DIM
    tm = ROW_TILE
    tw = cos_t.shape[1]
    q, k, v = pl.pallas_call(
        functools.partial(_diff_in_kernel, q_scale=hdim ** -0.5 * _LOG2_E),
        grid=(t // tm,),
        in_specs=[pl.BlockSpec((tm, d), lambda i: (i, 0)),
                  _const_spec((1, d)),
                  _const_spec((d, 3 * d_q)),
                  pl.BlockSpec((tm, tw), lambda i: (i, 0)),
                  pl.BlockSpec((tm, tw), lambda i: (i, 0)),
                  _const_spec((1, d_q)), _const_spec((1, d_q))],
        out_specs=[pl.BlockSpec((tm, d_q), lambda i: (i, 0))] * 3,
        out_shape=[jax.ShapeDtypeStruct((t, d_q), BF16)] * 3,
        compiler_params=_cparams("arbitrary"),
        name="diff_in",
    )(h, g_norm.reshape(1, d), w_in.astype(BF16), cos_t, sin_t,
      jnp.tile(q_norm, d_q // hdim).reshape(1, d_q), jnp.tile(k_norm, d_q // hdim).reshape(1, d_q))
    hw = 2 * hdim
    bw = DIFF_HEADS_PER_STEP * hw
    tile = min(ATTN_TILE, seqlen)
    spec = pl.BlockSpec((None, seqlen, bw), lambda b, j: (b, 0, j))
    vec = _const_spec((1, hdim))
    o = pl.pallas_call(
        functools.partial(_diff_attn_kernel, tile=tile, lambda_init=lambda_init),
        grid=(bsz, d_q // bw),
        in_specs=[spec, spec, spec, vec, vec, vec, vec, _const_spec((1, hw))],
        out_specs=spec,
        out_shape=jax.ShapeDtypeStruct((bsz, seqlen, d_q), BF16),
        scratch_shapes=[pltpu.VMEM((DIFF_HEADS_PER_STEP, 2 * tile, 2 * hw), F32)],
        compiler_params=_cparams("arbitrary", "arbitrary"),
        name="diff_attn",
    )(q.reshape(bsz, seqlen, d_q), k.reshape(bsz, seqlen, d_q), v.reshape(bsz, seqlen, d_q),
      lq1.reshape(1, hdim), lk1.reshape(1, hdim), lq2.reshape(1, hdim), lk2.reshape(1, hdim),
      sub_norm.reshape(1, hw))
    return _proj_residual(o.reshape(t, d_q), w_out.astype(BF16), h)


_R_E1, _R_E2, _R_W1, _R_W2, _R_RANK1, _R_RANK2 = range(6)
_C_LOC1, _C_LOC2, _C_W1, _C_W2 = range(4)


def _router_kernel(h_ref, g_ref, w_ref, b_ref, rec_ref, base_ref, cnt_ref, run_ref):
    tm = h_ref.shape[0]
    ne, ng, epg = MOE_EXPERTS, MOE_GROUPS, MOE_EXPERTS_PER_GROUP

    @pl.when(pl.program_id(0) == 0)
    def _():
        run_ref[...] = jnp.zeros(run_ref.shape, F32)

    base_ref[...] = run_ref[...]

    u = _rms(h_ref[...], g_ref[...])
    logits = _dot_split(u, u.astype(BF16), w_ref) + b_ref[...]
    lane = lax.broadcasted_iota(jnp.int32, logits.shape, 1)
    lane_f = lane.astype(F32)
    lane_group = (lane // epg).astype(F32)
    neg = -jnp.inf
    big = float(V7X_LANES)

    gl = jnp.where((lane >= ne) & (lane < ne + ng), logits, neg)
    gmax = jnp.max(gl, axis=1, keepdims=True)
    gsel = jnp.min(jnp.where(gl == gmax, lane_f - ne, big), axis=1, keepdims=True)
    g_gate = 1.0 / jnp.sum(jnp.exp(gl - gmax), axis=1, keepdims=True)

    el = jnp.where((lane < ne) & (lane_group == gsel), logits, neg)
    m1 = jnp.max(el, axis=1, keepdims=True)
    i1 = jnp.min(jnp.where(el == m1, lane_f, big), axis=1, keepdims=True)
    el2 = jnp.where(lane_f == i1, neg, el)
    m2 = jnp.max(el2, axis=1, keepdims=True)
    i2 = jnp.min(jnp.where(el2 == m2, lane_f, big), axis=1, keepdims=True)
    e21 = jnp.exp(m2 - m1)
    w1 = g_gate / (1.0 + e21)
    w2 = g_gate * e21 / (1.0 + e21)

    oh1 = lane_f == i1
    oh2 = lane_f == i2
    oh = jnp.where(oh1 | oh2, 1.0, 0.0)
    row = lax.broadcasted_iota(jnp.int32, (tm, tm), 0)
    col = lax.broadcasted_iota(jnp.int32, (tm, tm), 1)
    before = _dot((col < row).astype(BF16), oh.astype(BF16))
    pos = run_ref[0:1, :] + before
    rank1 = jnp.sum(jnp.where(oh1, pos, 0.0), axis=1, keepdims=True)
    rank2 = jnp.sum(jnp.where(oh2, pos, 0.0), axis=1, keepdims=True)
    run_ref[...] = run_ref[...] + jnp.sum(oh, axis=0, keepdims=True)
    cnt_ref[...] = run_ref[...]

    rec = jnp.zeros(logits.shape, F32)
    for idx, val in ((_R_E1, i1), (_R_E2, i2), (_R_W1, w1), (_R_W2, w2),
                     (_R_RANK1, rank1), (_R_RANK2, rank2)):
        rec = jnp.where(lane == idx, val, rec)
    rec_ref[...] = rec


def _pack_bf16_pair(x):
    n = x.shape[1] // 2
    xr = x.astype(BF16).astype(F32)
    return pltpu.bitcast(xr[:, :n], jnp.uint32) | (pltpu.bitcast(xr[:, n:], jnp.uint32) >> 16)


def _unpack_bf16_pair(w):
    hi = pltpu.bitcast(w & jnp.uint32(0xFFFF0000), F32)
    lo = pltpu.bitcast(w << 16, F32)
    return jnp.concatenate([hi, lo], axis=1)


def _chunk_rows(chunk):
    return pl.ds(pl.multiple_of(chunk * MOE_CHUNK, MOE_CHUNK), MOE_CHUNK)


def _start_chunk_copies(chunk_of_ref, tile, n_chunks, copy_of):
    def issue(j, _):
        copy_of(_chunk_rows(j), _chunk_rows(chunk_of_ref[tile * n_chunks + j])).start()
        return 0

    lax.fori_loop(0, n_chunks, issue, 0, unroll=DMA_LOOP_UNROLL)


def _wait_chunk_copies(n_chunks, copy_of):
    def drain(j, _):
        copy_of(_chunk_rows(0), _chunk_rows(0)).wait()
        return 0

    lax.fori_loop(0, n_chunks, drain, 0, unroll=DMA_LOOP_UNROLL)


def _dispatch_kernel(chunk_of_ref, h_ref, g_ref, loct_ref, rows_in_ref, rows_ref, sorted_ref, sems):
    del rows_in_ref
    i, nt = pl.program_id(0), pl.num_programs(0)
    slot = i % 2
    tm = h_ref.shape[0]
    n_chunks = sorted_ref.shape[1] // MOE_CHUNK

    def copy_of(slot_):
        return lambda lr, gr: pltpu.make_async_copy(sorted_ref.at[slot_, lr], rows_ref.at[gr],
                                                    sems.at[slot_])

    @pl.when(i >= 2)
    def _():
        _wait_chunk_copies(n_chunks, copy_of(slot))

    u = _rms(h_ref[...], g_ref[...]).astype(BF16)
    local_row = lax.broadcasted_iota(jnp.int32, (sorted_ref.shape[1], tm), 0)
    loc = loct_ref[...]
    pick = jnp.where(local_row == loc[0:1, :], 1.0, jnp.where(local_row == loc[1:2, :], 1.0, 0.0))
    sorted_ref[slot] = _pack_bf16_pair(_dot(pick.astype(BF16), u))
    _start_chunk_copies(chunk_of_ref, i, n_chunks, copy_of(slot))

    @pl.when(i == nt - 1)
    def _():
        _wait_chunk_copies(n_chunks, copy_of(slot))

    @pl.when(jnp.logical_and(i == nt - 1, i >= 1))
    def _():
        _wait_chunk_copies(n_chunks, copy_of(1 - slot))


def _expert_kernel(be_ref, nu_ref, nxt_ref, x_ref, wg_hbm, wu_hbm, wd_hbm, y_ref,
                   wgf_ref, wuf_ref, wdf_ref, wgub_ref, wdb_ref, sems, *, layer):
    i = pl.program_id(0)

    def weight_copies(e):
        return (pltpu.make_async_copy(wg_hbm.at[layer, e], wgf_ref, sems.at[0]),
                pltpu.make_async_copy(wu_hbm.at[layer, e], wuf_ref, sems.at[1]),
                pltpu.make_async_copy(wd_hbm.at[layer, e], wdf_ref, sems.at[2]))

    @pl.when(i < nu_ref[0])
    def _():
        e = be_ref[i]

        @pl.when(i == 0)
        def _():
            for c in weight_copies(e):
                c.start()

        @pl.when(jnp.logical_or(i == 0, e != be_ref[jnp.maximum(i - 1, 0)]))
        def _():
            for c in weight_copies(e):
                c.wait()
            ff = wgf_ref.shape[1]
            wgub_ref[:, :ff] = wgf_ref[...].astype(BF16)
            wgub_ref[:, ff:] = wuf_ref[...].astype(BF16)
            wdb_ref[...] = wdf_ref[...].astype(BF16)
            nxt = nxt_ref[i]

            @pl.when(nxt >= 0)
            def _():
                for c in weight_copies(nxt):
                    c.start()

        x = _unpack_bf16_pair(x_ref[...]).astype(BF16)
        gate_up = _dot(x, wgub_ref[...])
        gate = gate_up[:, :wgf_ref.shape[1]]
        up = gate_up[:, wgf_ref.shape[1]:]
        hid = (gate * _sigmoid(gate) * up).astype(BF16)
        y_ref[...] = _pack_bf16_pair(_dot(hid, wdb_ref[...]))

    @pl.when(i >= nu_ref[0])
    def _():
        y_ref[...] = jnp.zeros(y_ref.shape, y_ref.dtype)


def _combine_kernel(chunk_of_ref, h_ref, crec_ref, rows_ref, o_ref, sorted_ref, sems):
    i, nt = pl.program_id(0), pl.num_programs(0)
    slot = i % 2
    tm = h_ref.shape[0]
    n_chunks = sorted_ref.shape[1] // MOE_CHUNK

    def copy_of(slot_):
        return lambda lr, gr: pltpu.make_async_copy(rows_ref.at[gr], sorted_ref.at[slot_, lr],
                                                    sems.at[slot_])

    @pl.when(i == 0)
    def _():
        _start_chunk_copies(chunk_of_ref, i, n_chunks, copy_of(slot))

    @pl.when(i + 1 < nt)
    def _():
        _start_chunk_copies(chunk_of_ref, i + 1, n_chunks, copy_of(1 - slot))

    _wait_chunk_copies(n_chunks, copy_of(slot))
    crec = crec_ref[...]
    loc0 = crec[:, _C_LOC1:_C_LOC1 + 1].astype(jnp.int32)
    loc1 = crec[:, _C_LOC2:_C_LOC2 + 1].astype(jnp.int32)
    w1 = crec[:, _C_W1:_C_W1 + 1]
    w2 = crec[:, _C_W2:_C_W2 + 1]
    local_row = lax.broadcasted_iota(jnp.int32, (tm, sorted_ref.shape[1]), 1)
    weights = jnp.where(local_row == loc0, w1, jnp.where(local_row == loc1, w2, 0.0)).astype(BF16)
    y = _unpack_bf16_pair(sorted_ref[slot]).astype(BF16)
    o_ref[...] = h_ref[...] + _dot(weights, y)


def _hier_moe(h, g_norm, w_group, b_group, w_expert, b_expert, layer, w_gate, w_up, w_down):
    t, d = h.shape
    ne, ng = MOE_EXPERTS, MOE_GROUPS
    ff = w_gate.shape[3]
    dp = d // 2
    tm = min(MOE_TOKEN_TILE, t)
    g2 = g_norm.reshape(1, d)
    lane_pad = V7X_LANES - ne - ng
    w_router = _split_weight(
        jnp.concatenate([w_expert, w_group, jnp.zeros((d, lane_pad), F32)], axis=1))
    b_router = jnp.concatenate([b_expert, b_group, jnp.zeros((lane_pad,), F32)]).reshape(1, V7X_LANES)
    nt = t // tm
    rec, base, cnt = pl.pallas_call(
        _router_kernel,
        grid=(nt,),
        in_specs=[pl.BlockSpec((tm, d), lambda i: (i, 0)),
                  _const_spec((1, d)),
                  _const_spec((d, 2 * V7X_LANES)),
                  _const_spec((1, V7X_LANES))],
        out_specs=[pl.BlockSpec((tm, V7X_LANES), lambda i: (i, 0)),
                   pl.BlockSpec((V7X_SUBLANES, V7X_LANES), lambda i: (i, 0)),
                   _const_spec((V7X_SUBLANES, V7X_LANES))],
        out_shape=[jax.ShapeDtypeStruct((t, V7X_LANES), F32),
                   jax.ShapeDtypeStruct((nt * V7X_SUBLANES, V7X_LANES), F32),
                   jax.ShapeDtypeStruct((V7X_SUBLANES, V7X_LANES), F32)],
        scratch_shapes=[pltpu.VMEM((V7X_SUBLANES, V7X_LANES), F32)],
        compiler_params=_cparams("arbitrary"),
        name="moe_router",
    )(h, g2, w_router, b_router)

    br, ck = MOE_ROW_BLOCK, MOE_CHUNK
    n_blocks = -(-(2 * t + nt * ne * (ck - 1) + ne * (br - 1)) // br)
    n_rows = n_blocks * br
    local_rows = 2 * tm + ne * ck
    before = base.reshape(nt, V7X_SUBLANES, V7X_LANES)[:, 0, :ne].astype(jnp.int32)
    total = cnt[0, :ne].astype(jnp.int32)
    tile_cnt = jnp.concatenate([before[1:], total[None, :]], axis=0) - before
    seg = (tile_cnt + ck - 1) // ck * ck
    region = jnp.sum(seg, axis=0)
    padded = (region + br - 1) // br * br
    pad_end = jnp.cumsum(padded)
    pad_start = pad_end - padded
    seg_global = pad_start[None, :] + jnp.cumsum(seg, axis=0) - seg
    seg_local = jnp.cumsum(seg, axis=1) - seg
    eid = rec[:, _R_E1:_R_E2 + 1].astype(jnp.int32)
    rank = rec[:, _R_RANK1:_R_RANK2 + 1].astype(jnp.int32)
    experts = jnp.arange(ne, dtype=jnp.int32)
    shift = jnp.repeat(seg_local - before, tm, axis=0)
    loc = rank + jnp.sum(jnp.where(eid[:, :, None] == experts, shift[:, None, :], 0), axis=-1)
    loct = jnp.concatenate([jnp.swapaxes(loc.reshape(nt, tm, 2), 1, 2),
                            jnp.full((nt, V7X_SUBLANES - 2, tm), -1, jnp.int32)], axis=1)
    crec = jnp.concatenate([loc.astype(F32), rec[:, _R_W1:_R_W2 + 1],
                            jnp.zeros((t, V7X_SUBLANES - 4), F32)], axis=1)
    n_local = local_rows // ck
    j = jnp.arange(n_local, dtype=jnp.int32)
    seg_end = (seg_local + seg) // ck
    owner = jnp.sum((seg_end[:, None, :] <= j[None, :, None]).astype(jnp.int32), axis=-1)
    offset = (seg_global - seg_local) // ck
    chunk_of = j[None, :] + jnp.sum(jnp.where(owner[:, :, None] == experts, offset[:, None, :], 0), axis=-1)
    in_use = owner < ne
    parity = (jnp.arange(nt, dtype=jnp.int32) % 2)[:, None]
    scatter_chunk = jnp.where(in_use, chunk_of, n_rows // ck + parity * n_local + j[None, :]).reshape(-1)
    gather_chunk = jnp.where(in_use, chunk_of, 0).reshape(-1)
    block_row0 = jnp.arange(n_blocks, dtype=jnp.int32) * br
    block_expert = jnp.minimum(
        jnp.sum((pad_end[None, :] <= block_row0[:, None]).astype(jnp.int32), axis=1), ne - 1)
    n_used = (pad_end[-1:] // br).astype(jnp.int32)
    blocks = jnp.arange(n_blocks, dtype=jnp.int32)
    later_run = ((block_expert[None, :] != block_expert[:, None]) & (blocks[None, :] > blocks[:, None])
                 & (blocks[None, :] < n_used))
    next_expert = jnp.where(jnp.any(later_run, axis=1),
                            block_expert[jnp.argmax(later_run, axis=1)], -1).astype(jnp.int32)

    x_rows = pl.pallas_call(
        _dispatch_kernel,
        grid_spec=pltpu.PrefetchScalarGridSpec(
            num_scalar_prefetch=1,
            grid=(nt,),
            in_specs=[pl.BlockSpec((tm, d), lambda i, *_: (i, 0)),
                      _const_spec((1, d)),
                      pl.BlockSpec((None, V7X_SUBLANES, tm), lambda i, *_: (i, 0, 0)),
                      pl.BlockSpec(memory_space=pl.ANY)],
            out_specs=pl.BlockSpec(memory_space=pl.ANY),
            scratch_shapes=[pltpu.VMEM((2, local_rows, dp), jnp.uint32), pltpu.SemaphoreType.DMA((2,))]),
        out_shape=jax.ShapeDtypeStruct((n_rows + 2 * local_rows, dp), jnp.uint32),
        input_output_aliases={4: 0},
        compiler_params=_cparams("arbitrary"),
        name="moe_dispatch",
    )(scatter_chunk, h, g2, loct, jnp.zeros((n_rows + 2 * local_rows, dp), jnp.uint32))

    def blk(i, be, nu, nxt):
        return jnp.maximum(jnp.minimum(i, nu[0] - 1), 0)

    hbm = pl.BlockSpec(memory_space=pl.ANY)
    y_rows = pl.pallas_call(
        functools.partial(_expert_kernel, layer=layer),
        grid_spec=pltpu.PrefetchScalarGridSpec(
            num_scalar_prefetch=3,
            grid=(n_blocks,),
            in_specs=[pl.BlockSpec((br, dp), lambda i, be, nu, nxt: (blk(i, be, nu, nxt), 0)),
                      hbm, hbm, hbm],
            out_specs=pl.BlockSpec((br, dp), lambda i, be, nu, nxt: (i, 0)),
            scratch_shapes=[pltpu.VMEM((d, ff), F32), pltpu.VMEM((d, ff), F32), pltpu.VMEM((ff, d), F32),
                            pltpu.VMEM((d, 2 * ff), BF16), pltpu.VMEM((ff, d), BF16),
                            pltpu.SemaphoreType.DMA((3,))]),
        out_shape=jax.ShapeDtypeStruct((n_rows, dp), jnp.uint32),
        compiler_params=_cparams("arbitrary"),
        name="moe_experts",
    )(block_expert, n_used, next_expert, x_rows, w_gate, w_up, w_down)

    return pl.pallas_call(
        _combine_kernel,
        grid_spec=pltpu.PrefetchScalarGridSpec(
            num_scalar_prefetch=1,
            grid=(nt,),
            in_specs=[pl.BlockSpec((tm, d), lambda i, *_: (i, 0)),
                      pl.BlockSpec((tm, V7X_SUBLANES), lambda i, *_: (i, 0)),
                      pl.BlockSpec(memory_space=pl.ANY)],
            out_specs=pl.BlockSpec((tm, d), lambda i, *_: (i, 0)),
            scratch_shapes=[pltpu.VMEM((2, local_rows, dp), jnp.uint32), pltpu.SemaphoreType.DMA((2,))]),
        out_shape=jax.ShapeDtypeStruct((t, d), F32),
        compiler_params=_cparams("arbitrary"),
        name="moe_combine",
    )(gather_chunk, h, crec, y_rows)


def _rope_tables(positions):
    half = DIFF_HEAD_DIM // 2
    inv = ROPE_THETA ** (-jnp.arange(0, DIFF_HEAD_DIM, 2, dtype=F32) / DIFF_HEAD_DIM)
    ang = positions.astype(F32).reshape(-1, 1) * inv
    cos, sin = jnp.cos(ang), jnp.sin(ang)
    cos_t = jnp.concatenate([cos] * 4, axis=1)
    sin_t = jnp.concatenate([-sin, sin, -sin, sin], axis=1)
    assert cos_t.shape[1] == 4 * half
    return cos_t, sin_t


def kernel(x, positions, norm_mix, norm_ffn, ssd_w_in, ssd_conv_w, ssd_conv_b, ssd_dt_bias, ssd_a_log, ssd_d, ssd_norm, ssd_w_out, sb_w_in, sb_w_out, lru_w_in, lru_conv_w, lru_conv_b, lru_w_rg, lru_b_rg, lru_w_ig, lru_b_ig, lru_lambda, lru_w_out, diff_w_in, diff_q_norm, diff_k_norm, diff_lam_q1, diff_lam_k1, diff_lam_q2, diff_lam_k2, diff_sub_norm, diff_w_out, moe_w_group, moe_b_group, moe_w_expert, moe_b_expert, moe_w_gate, moe_w_up, moe_w_down):
    bsz, seqlen, d = x.shape
    depth = norm_mix.shape[0]
    h = x.reshape(bsz * seqlen, d)
    cos_t, sin_t = _rope_tables(positions)
    for i in range(depth):
        m, j = i % N_MIXERS, i // N_MIXERS
        if m == 0:
            h = _ssd_mixer(h, bsz, seqlen, norm_mix[i], ssd_w_in[j], ssd_conv_w[j], ssd_conv_b[j],
                           ssd_dt_bias[j], ssd_a_log[j], ssd_d[j], ssd_norm[j], ssd_w_out[j])
        elif m == 1:
            h = _sb_mixer(h, bsz, seqlen, norm_mix[i], sb_w_in[j], sb_w_out[j])
        elif m == 2:
            h = _lru_mixer(h, bsz, seqlen, norm_mix[i], lru_w_in[j], lru_conv_w[j], lru_conv_b[j],
                           lru_w_rg[j], lru_b_rg[j], lru_w_ig[j], lru_b_ig[j], lru_lambda[j],
                           lru_w_out[j])
        else:
            lambda_init = 0.8 - 0.6 * math.exp(-0.3 * i)
            h = _diff_mixer(h, bsz, seqlen, cos_t, sin_t, norm_mix[i], diff_w_in[j], diff_q_norm[j],
                            diff_k_norm[j], diff_lam_q1[j], diff_lam_k1[j], diff_lam_q2[j],
                            diff_lam_k2[j], diff_sub_norm[j], diff_w_out[j], lambda_init)
        h = _hier_moe(h, norm_ffn[i], moe_w_group[i], moe_b_group[i], moe_w_expert[i],
                      moe_b_expert[i], i, moe_w_gate, moe_w_up, moe_w_down)
    return h.reshape(bsz, seqlen, d)
```
